```python
import math
import jax, jax.numpy as jnp
from jax import lax
import numpy as np

D_MODEL = 1024
BATCH = 4
SEQ = 8192
DEPTH = 4

GRID_W = 64
CTX_LEN = 256
N_MIXERS = 3
MIXER_POOL = 0
MIXER_NA = 1
MIXER_GQA = 2
POOL_WINDOWS = (2, 4, 8, 16)
POOL_GROUPS = len(POOL_WINDOWS)
POOL_CH = D_MODEL // POOL_GROUPS
HEAD_DIM = 64
NA_HEADS = D_MODEL // HEAD_DIM
NA_KH = 8
NA_KW = 16
GQA_Q_HEADS = D_MODEL // HEAD_DIM
GQA_KV_HEADS = 4
GQA_GROUP = GQA_Q_HEADS // GQA_KV_HEADS
Q_BLOCK = 128
ROPE_THETA = 10000.0
N_EXPERTS = 16
EXPERT_FF = 2 * D_MODEL
EC_CAPACITY = 2
N_MOD = 6
LN_EPS = 1e-5
RMS_EPS = 1e-6

kernel_name = "hybrid_pool_na_gqa_ecmoe_dit"


def _layers_of(m):
    return [i for i in range(DEPTH) if i % N_MIXERS == m]


def _ctx_needed_after(i):
    return any(j % N_MIXERS != MIXER_POOL for j in range(i + 1, DEPTH))


def _layer_norm(x, g, b):
    xf = x.astype(jnp.float32)
    mu = xf.mean(-1, keepdims=True)
    xc = xf - mu
    var = (xc * xc).mean(-1, keepdims=True)
    return (xc * lax.rsqrt(var + LN_EPS) * g.astype(jnp.float32) + b.astype(jnp.float32)).astype(x.dtype)


def _rms_norm(x, g):
    xf = x.astype(jnp.float32)
    ms = (xf * xf).mean(-1, keepdims=True)
    return (xf * lax.rsqrt(ms + RMS_EPS) * g.astype(jnp.float32)).astype(x.dtype)


def _modulate(x, shift, scale):
    return x * (1 + scale) + shift


def _joint_softmax(s_a, s_b):
    p = jax.nn.softmax(jnp.concatenate([s_a, s_b], axis=-1).astype(jnp.float32), axis=-1)
    na = s_a.shape[-1]
    return p[..., :na], p[..., na:]


def _axial_rope_tables(n):
    t = jnp.arange(n, dtype=jnp.int32)
    row = (t // GRID_W).astype(jnp.float32)
    col = (t % GRID_W).astype(jnp.float32)
    axis_dims = HEAD_DIM // 2
    inv_freq = jnp.power(ROPE_THETA, -jnp.arange(0, axis_dims, 2, dtype=jnp.float32) / axis_dims)
    ang = jnp.concatenate([row[:, None] * inv_freq, col[:, None] * inv_freq], axis=-1)
    return jnp.cos(ang), jnp.sin(ang)


def _apply_rope(x, cos, sin):
    xf = x.astype(jnp.float32).reshape(x.shape[:-1] + (HEAD_DIM // 2, 2))
    x1, x2 = xf[..., 0], xf[..., 1]
    c = cos[None, :, None, :]
    s = sin[None, :, None, :]
    out = jnp.stack([x1 * c - x2 * s, x1 * s + x2 * c], axis=-1).reshape(x.shape)
    return out.astype(x.dtype)


def _pool_mixer(h, w_groups, layer_scale):
    B, n, D = h.shape
    hf = h.astype(jnp.float32)
    cs = jnp.concatenate([jnp.zeros((B, 1, D), jnp.float32), jnp.cumsum(hf, axis=1)], axis=1)
    pos = jnp.arange(n)
    parts = []
    for g, w in enumerate(POOL_WINDOWS):
        lo = jnp.clip(pos - w // 2, 0, n - 1)
        hi = jnp.clip(pos + (w - w // 2 - 1), 0, n - 1)
        csg = cs[:, :, g * POOL_CH:(g + 1) * POOL_CH]
        count = (hi - lo + 1).astype(jnp.float32)[None, :, None]
        mean = (csg[:, hi + 1] - csg[:, lo]) / count
        parts.append(mean - hf[:, :, g * POOL_CH:(g + 1) * POOL_CH])
    d = jnp.stack(parts, axis=2).astype(h.dtype)
    y = jnp.einsum('bngc,gcd->bngd', d, w_groups).reshape(B, n, D)
    return y * layer_scale


def _na_mixer(h, hc, wqkv, wo, rpb, update_ctx):
    B, n, D = h.shape
    L = hc.shape[1]
    rows = n // GRID_W
    kh = min(NA_KH, rows)
    scale = HEAD_DIM ** -0.5
    qkv = (h @ wqkv).reshape(B, rows, GRID_W, 3, NA_HEADS, HEAD_DIM)
    q = qkv[:, :, :, 0].transpose(0, 3, 1, 2, 4)
    k = qkv[:, :, :, 1].transpose(0, 3, 1, 2, 4)
    v = qkv[:, :, :, 2].transpose(0, 3, 1, 2, 4)
    qkv_c = (hc @ wqkv).reshape(B, L, 3, NA_HEADS, HEAD_DIM)
    qc = qkv_c[:, :, 0].transpose(0, 2, 1, 3)
    kc = qkv_c[:, :, 1].transpose(0, 2, 1, 3)
    vc = qkv_c[:, :, 2].transpose(0, 2, 1, 3)
    col = np.arange(GRID_W)
    c0 = np.clip(col - NA_KW // 2, 0, GRID_W - NA_KW)
    col_idx = c0[:, None] + np.arange(NA_KW)[None, :]
    dc_idx = col_idx - col[:, None] + (NA_KW - 1)

    def row_block(r):
        r0 = jnp.clip(r - kh // 2, 0, rows - kh)
        k_rows = lax.dynamic_slice_in_dim(k, r0, kh, axis=2)
        v_rows = lax.dynamic_slice_in_dim(v, r0, kh, axis=2)
        k_nb = k_rows[:, :, :, col_idx, :].transpose(0, 1, 3, 2, 4, 5).reshape(B, NA_HEADS, GRID_W, kh * NA_KW, HEAD_DIM)
        v_nb = v_rows[:, :, :, col_idx, :].transpose(0, 1, 3, 2, 4, 5).reshape(B, NA_HEADS, GRID_W, kh * NA_KW, HEAD_DIM)
        q_r = lax.dynamic_index_in_dim(q, r, axis=2, keepdims=False)
        dr_idx = r0 + jnp.arange(kh) - r + (NA_KH - 1)
        bias = rpb[:, dr_idx][:, :, dc_idx]
        bias = bias.transpose(0, 2, 1, 3).reshape(NA_HEADS, GRID_W, kh * NA_KW).astype(jnp.float32)
        s_loc = jnp.einsum('bhwd,bhwkd->bhwk', q_r, k_nb).astype(jnp.float32) * scale + bias[None]
        s_ctx = jnp.einsum('bhwd,bhld->bhwl', q_r, kc).astype(jnp.float32) * scale
        p_loc, p_ctx = _joint_softmax(s_loc, s_ctx)
        return (jnp.einsum('bhwk,bhwkd->bhwd', p_loc.astype(v.dtype), v_nb)
                + jnp.einsum('bhwl,bhld->bhwd', p_ctx.astype(v.dtype), vc))

    o = lax.map(row_block, jnp.arange(rows))
    o = o.transpose(1, 0, 3, 2, 4).reshape(B, n, D)
    y = o @ wo
    yc = None
    if update_ctx:
        sc = jnp.einsum('bhld,bhmd->bhlm', qc, kc).astype(jnp.float32) * scale
        pc = jax.nn.softmax(sc, axis=-1).astype(vc.dtype)
        oc = jnp.einsum('bhlm,bhmd->bhld', pc, vc).transpose(0, 2, 1, 3).reshape(B, L, D)
        yc = oc @ wo
    return y, yc


def _gqa_mixer(h, hc, wqkv, q_g, k_g, wo, cos, sin, update_ctx):
    B, n, D = h.shape
    L = hc.shape[1]
    scale = HEAD_DIM ** -0.5
    dq = GQA_Q_HEADS * HEAD_DIM
    dkv = GQA_KV_HEADS * HEAD_DIM

    def split_heads(t, length):
        q = t[..., :dq].reshape(B, length, GQA_Q_HEADS, HEAD_DIM)
        k = t[..., dq:dq + dkv].reshape(B, length, GQA_KV_HEADS, HEAD_DIM)
        v = t[..., dq + dkv:].reshape(B, length, GQA_KV_HEADS, HEAD_DIM)
        return _rms_norm(q, q_g), _rms_norm(k, k_g), v

    q, k, v = split_heads(h @ wqkv, n)
    q = _apply_rope(q, cos, sin)
    k = _apply_rope(k, cos, sin)
    qc, kc, vc = split_heads(hc @ wqkv, L)
    q = q.reshape(B, n, GQA_KV_HEADS, GQA_GROUP, HEAD_DIM).transpose(0, 2, 3, 1, 4)
    k = k.transpose(0, 2, 1, 3)
    v = v.transpose(0, 2, 1, 3)
    kc = kc.transpose(0, 2, 1, 3)
    vc = vc.transpose(0, 2, 1, 3)
    nb = n // Q_BLOCK
    q_blocks = jnp.moveaxis(q.reshape(B, GQA_KV_HEADS, GQA_GROUP, nb, Q_BLOCK, HEAD_DIM), 3, 0)

    def block(qi):
        s_lat = jnp.einsum('bkgqd,bknd->bkgqn', qi, k).astype(jnp.float32) * scale
        s_ctx = jnp.einsum('bkgqd,bkld->bkgql', qi, kc).astype(jnp.float32) * scale
        p_lat, p_ctx = _joint_softmax(s_lat, s_ctx)
        return (jnp.einsum('bkgqn,bknd->bkgqd', p_lat.astype(v.dtype), v)
                + jnp.einsum('bkgql,bkld->bkgqd', p_ctx.astype(v.dtype), vc))

    o = lax.map(block, q_blocks)
    o = jnp.moveaxis(o, 0, 3).reshape(B, GQA_KV_HEADS, GQA_GROUP, n, HEAD_DIM)
    o = o.transpose(0, 3, 1, 2, 4).reshape(B, n, dq)
    y = o @ wo
    yc = None
    if update_ctx:
        qcg = qc.reshape(B, L, GQA_KV_HEADS, GQA_GROUP, HEAD_DIM).transpose(0, 2, 3, 1, 4)
        sc = jnp.einsum('bkgld,bkmd->bkglm', qcg, kc).astype(jnp.float32) * scale
        pc = jax.nn.softmax(sc, axis=-1).astype(vc.dtype)
        oc = jnp.einsum('bkglm,bkmd->bkgld', pc, vc).transpose(0, 3, 1, 2, 4).reshape(B, L, dq)
        yc = oc @ wo
    return y, yc


def _ec_moe(h, router, w1, w3, w2):
    B, n, D = h.shape
    cap = EC_CAPACITY * n // N_EXPERTS
    aff = jax.nn.softmax((h @ router).astype(jnp.float32), axis=-1)
    gate, idx = lax.top_k(jnp.swapaxes(aff, 1, 2), cap)
    b_idx = jnp.arange(B)[:, None, None]
    xs = h[b_idx, idx]
    hid = jax.nn.silu(jnp.einsum('becd,edf->becf', xs, w1)) * jnp.einsum('becd,edf->becf', xs, w3)
    ys = jnp.einsum('becf,efd->becd', hid, w2) * gate[..., None].astype(h.dtype)
    return jnp.zeros_like(h).at[b_idx, idx].add(ys)


def setup_inputs(seed: int = 0) -> dict:
    key = jax.random.key(seed)
    ks = jax.random.split(key, 22)
    f32 = jnp.float32
    D = D_MODEL
    beta = (8.0 * DEPTH) ** -0.25
    n_pool, n_na, n_gqa = (len(_layers_of(m)) for m in range(N_MIXERS))
    qkv_width = (GQA_Q_HEADS + 2 * GQA_KV_HEADS) * HEAD_DIM

    def nrm(k, shape, s):
        return jax.random.normal(k, shape, f32) * s

    return {
        "x": nrm(ks[0], (BATCH, SEQ, D), 1.0),
        "c": nrm(ks[1], (BATCH, D), 1.0),
        "ctx": nrm(ks[2], (BATCH, CTX_LEN, D), 1.0),
        "c_ctx": nrm(ks[3], (D,), 1.0),
        "mod_w": nrm(ks[4], (DEPTH, D, N_MOD * D), 0.5 * D ** -0.5),
        "mod_b": nrm(ks[5], (DEPTH, N_MOD * D), 0.02),
        "ln_g": 1.0 + nrm(ks[6], (DEPTH, 2, D), 0.02),
        "ln_b": nrm(ks[7], (DEPTH, 2, D), 0.02),
        "pool_w": nrm(ks[8], (n_pool, POOL_GROUPS, POOL_CH, POOL_CH), beta * POOL_CH ** -0.5),
        "pool_scale": 1.0 + nrm(ks[9], (n_pool, D), 0.02),
        "na_wqkv": nrm(ks[10], (n_na, D, 3 * D), D ** -0.5),
        "na_wo": nrm(ks[11], (n_na, D, D), beta * D ** -0.5),
        "na_rpb": nrm(ks[12], (n_na, NA_HEADS, 2 * NA_KH - 1, 2 * NA_KW - 1), 0.1),
        "gqa_wqkv": nrm(ks[13], (n_gqa, D, qkv_width), D ** -0.5),
        "gqa_q_norm": 1.0 + nrm(ks[14], (n_gqa, HEAD_DIM), 0.02),
        "gqa_k_norm": 1.0 + nrm(ks[15], (n_gqa, HEAD_DIM), 0.02),
        "gqa_wo": nrm(ks[16], (n_gqa, GQA_Q_HEADS * HEAD_DIM, D), beta * (GQA_Q_HEADS * HEAD_DIM) ** -0.5),
        "moe_router": nrm(ks[17], (DEPTH, D, N_EXPERTS), D ** -0.5),
        "moe_w1": nrm(ks[18], (DEPTH, N_EXPERTS, D, EXPERT_FF), D ** -0.5),
        "moe_w3": nrm(ks[19], (DEPTH, N_EXPERTS, D, EXPERT_FF), D ** -0.5),
        "moe_w2": nrm(ks[20], (DEPTH, N_EXPERTS, EXPERT_FF, D), beta * EXPERT_FF ** -0.5),
    }


def reference(x, c, ctx, c_ctx, mod_w, mod_b, ln_g, ln_b, pool_w, pool_scale, na_wqkv, na_wo, na_rpb,
              gqa_wqkv, gqa_q_norm, gqa_k_norm, gqa_wo, moe_router, moe_w1, moe_w3, moe_w2):
    n = x.shape[1]
    alpha = (2.0 * DEPTH) ** 0.25
    cos, sin = _axial_rope_tables(n)
    for i in range(DEPTH):
        m = i % N_MIXERS
        j = i // N_MIXERS
        update_ctx = _ctx_needed_after(i)
        ctx_keys = m != MIXER_POOL
        mod = jax.nn.silu(c) @ mod_w[i] + mod_b[i]
        sh_a, sc_a, g_a, sh_f, sc_f, g_f = jnp.split(mod[:, None, :], N_MOD, axis=-1)
        h = _modulate(x, sh_a, sc_a)
        hc = None
        if update_ctx or ctx_keys:
            mod_c = jax.nn.silu(c_ctx) @ mod_w[i] + mod_b[i]
            csh_a, csc_a, cg_a, csh_f, csc_f, cg_f = jnp.split(mod_c, N_MOD)
            hc = _modulate(ctx, csh_a, csc_a)
        if m == MIXER_POOL:
            y = _pool_mixer(h, pool_w[j], pool_scale[j])
            yc = _pool_mixer(hc, pool_w[j], pool_scale[j]) if update_ctx else None
        elif m == MIXER_NA:
            y, yc = _na_mixer(h, hc, na_wqkv[j], na_wo[j], na_rpb[j], update_ctx)
        else:
            y, yc = _gqa_mixer(h, hc, gqa_wqkv[j], gqa_q_norm[j], gqa_k_norm[j], gqa_wo[j], cos, sin, update_ctx)
        x = _layer_norm(alpha * x + g_a * y, ln_g[i, 0], ln_b[i, 0])
        f = _ec_moe(_modulate(x, sh_f, sc_f), moe_router[i], moe_w1[i], moe_w3[i], moe_w2[i])
        x = _layer_norm(alpha * x + g_f * f, ln_g[i, 1], ln_b[i, 1])
        if update_ctx:
            ctx = _layer_norm(alpha * ctx + cg_a * yc, ln_g[i, 0], ln_b[i, 0])
            fc = _ec_moe(_modulate(ctx, csh_f, csc_f), moe_router[i], moe_w1[i], moe_w3[i], moe_w2[i])
            ctx = _layer_norm(alpha * ctx + cg_f * fc, ln_g[i, 1], ln_b[i, 1])
    return x
```

```python
import functools
import math

import jax
import jax.numpy as jnp
import numpy as np
from jax import lax
from jax.experimental import pallas as pl
from jax.experimental.pallas import tpu as pltpu

F32 = jnp.float32
BF16 = jnp.bfloat16
I32 = jnp.int32
HIGHEST = lax.Precision.HIGHEST

DEPTH = 4
N_MIXERS = 3
GRID_W = 64
HEAD_DIM = 64
POOL_WINDOWS = (2, 4, 8, 16)
POOL_HALO = 8
NA_KH = 8
NA_KW = 16
NA_QROWS = 4
NA_KBLOCKS = 3
GQA_KV_HEADS = 4
ROPE_THETA = 10000.0
EC_CAPACITY = 2
N_MOD = 6
LN_EPS = 1e-5
RMS_EPS = 1e-6
ALPHA = (2.0 * DEPTH) ** 0.25
LANES = 128
ROUTE_T = 128
ROUTE_K = 32
ROW_ALIGN = 16
GATHER_ROWS = ROUTE_K + ROW_ALIGN
VMEM_LIMIT = 56 * 1024 * 1024
NEG_INF = -1e30


def _cparams(sem, vmem=VMEM_LIMIT):
    return pltpu.CompilerParams(dimension_semantics=sem, vmem_limit_bytes=vmem)


def _silu(v):
    return v / (1.0 + jnp.exp(-v))


def _mod_kernel(c_ref, w_ref, b_ref, o_ref):
    s = _silu(c_ref[...])
    o_ref[0] = jnp.dot(s, w_ref[0], precision=HIGHEST, preferred_element_type=F32) + b_ref[0]


def _modulation(cc, mod_w, mod_b):
    depth, d, nd = mod_w.shape
    tn = nd // 4
    return pl.pallas_call(
        _mod_kernel,
        grid=(depth, nd // tn),
        in_specs=[
            pl.BlockSpec((8, d), lambda i, j: (0, 0)),
            pl.BlockSpec((1, d, tn), lambda i, j: (i, 0, j)),
            pl.BlockSpec((1, 1, tn), lambda i, j: (i, 0, j)),
        ],
        out_specs=pl.BlockSpec((1, 8, tn), lambda i, j: (i, 0, j)),
        out_shape=jax.ShapeDtypeStruct((depth, 8, nd), F32),
        compiler_params=_cparams(("parallel", "parallel")),
    )(cc, mod_w, mod_b.reshape(depth, 1, nd))


def _layer_norm(z, g, b):
    mu = jnp.mean(z, axis=-1, keepdims=True)
    zc = z - mu
    var = jnp.mean(zc * zc, axis=-1, keepdims=True)
    return zc * lax.rsqrt(var + LN_EPS) * g + b


def _post_mixer(x, y, mod, lng, lnb, router, n_exp):
    x1 = _layer_norm(ALPHA * x + mod[2:3] * y, lng, lnb)
    hf = x1 * (1.0 + mod[4:5]) + mod[3:4]
    logits = jnp.dot(hf, router, precision=HIGHEST, preferred_element_type=F32)
    lt = logits.T[:n_exp]
    m = jnp.max(lt, axis=0, keepdims=True)
    p = jnp.exp(lt - m)
    aff = p / jnp.sum(p, axis=0, keepdims=True)
    return x1, hf.astype(BF16), aff


def _post_outs(b, n, d, n_exp):
    return (jax.ShapeDtypeStruct((b, n, d), F32),
            jax.ShapeDtypeStruct((b, n, d), BF16),
            jax.ShapeDtypeStruct((b, n_exp, n), F32))


def _post_out_specs(tm, d, n_exp):
    return (pl.BlockSpec((1, tm, d), lambda b, t: (b, t, 0)),
            pl.BlockSpec((1, tm, d), lambda b, t: (b, t, 0)),
            pl.BlockSpec((1, n_exp, tm), lambda b, t: (b, 0, t)))


def _pad_router(router):
    d, e = router.shape
    return jnp.zeros((d, LANES), F32).at[:, :e].set(router)


def _pool_kernel(x_ref, xp_ref, xn_ref, mod_ref, pw_ref, ps_ref, lng_ref, lnb_ref, r_ref,
                 x1_ref, hf_ref, aff_ref, buf, *, n, tm, n_exp):
    t = pl.program_id(1)
    nt = pl.num_programs(1)
    mod = mod_ref[0]
    x = x_ref[0]
    sc = 1.0 + mod[1:2]
    sh = mod[0:1]
    h = x * sc + sh
    hp = xp_ref[0] * sc + sh
    hn = xn_ref[0] * sc + sh
    buf[0:POOL_HALO, :] = jnp.where(t > 0, hp, 0.0)
    buf[POOL_HALO:POOL_HALO + tm, :] = h
    buf[POOL_HALO + tm:, :] = jnp.where(t < nt - 1, hn, 0.0)
    pos = t * tm + lax.broadcasted_iota(I32, (tm, 1), 0)
    ch = x.shape[1] // len(POOL_WINDOWS)
    parts = []
    for g, w in enumerate(POOL_WINDOWS):
        cols = slice(g * ch, (g + 1) * ch)
        acc = None
        for o in range(-(w // 2), w - w // 2):
            v = buf[POOL_HALO + o:POOL_HALO + o + tm, cols]
            acc = v if acc is None else acc + v
        lo = jnp.maximum(pos - w // 2, 0)
        hi = jnp.minimum(pos + (w - w // 2 - 1), n - 1)
        cnt = (hi - lo + 1).astype(F32)
        dlt = acc / cnt - h[:, cols]
        parts.append(jnp.dot(dlt.astype(BF16), pw_ref[g], preferred_element_type=F32))
    y = jnp.concatenate(parts, axis=1) * ps_ref[...]
    x1, hf, aff = _post_mixer(x, y, mod, lng_ref[...], lnb_ref[...], r_ref[...], n_exp)
    x1_ref[0] = x1
    hf_ref[0] = hf
    aff_ref[0] = aff


def _pool_layer(x, mod, pool_w, pool_scale, lng, lnb, router_p, n_exp):
    b, n, d = x.shape
    tm = min(256, n)
    hb = tm // POOL_HALO
    nhb = n // POOL_HALO
    g, ch, _ = pool_w.shape
    kern = functools.partial(_pool_kernel, n=n, tm=tm, n_exp=n_exp)
    return pl.pallas_call(
        kern,
        grid=(b, n // tm),
        in_specs=[
            pl.BlockSpec((1, tm, d), lambda i, t: (i, t, 0)),
            pl.BlockSpec((1, POOL_HALO, d), lambda i, t: (i, jnp.maximum(t * hb - 1, 0), 0)),
            pl.BlockSpec((1, POOL_HALO, d), lambda i, t: (i, jnp.minimum((t + 1) * hb, nhb - 1), 0)),
            pl.BlockSpec((1, 8, d), lambda i, t: (i, 0, 0)),
            pl.BlockSpec((g, ch, ch), lambda i, t: (0, 0, 0)),
            pl.BlockSpec((1, d), lambda i, t: (0, 0)),
            pl.BlockSpec((1, d), lambda i, t: (0, 0)),
            pl.BlockSpec((1, d), lambda i, t: (0, 0)),
            pl.BlockSpec((d, LANES), lambda i, t: (0, 0)),
        ],
        out_specs=_post_out_specs(tm, d, n_exp),
        out_shape=_post_outs(b, n, d, n_exp),
        scratch_shapes=[pltpu.VMEM((tm + 2 * POOL_HALO, d), F32)],
        compiler_params=_cparams(("parallel", "parallel")),
    )(x, x, x, mod, pool_w.astype(BF16), pool_scale.reshape(1, d), lng.reshape(1, d),
      lnb.reshape(1, d), router_p)


def _proj_in_kernel(x_ref, mod_ref, w_ref, o_ref, *, q_cols, tn):
    mod = mod_ref[0]
    h = (x_ref[0] * (1.0 + mod[1:2]) + mod[0:1]).astype(BF16)
    ncol = w_ref.shape[1]
    for j in range(ncol // tn):
        y = jnp.dot(h, w_ref[:, j * tn:(j + 1) * tn], preferred_element_type=F32)
        if (j + 1) * tn <= q_cols:
            y = y * (HEAD_DIM ** -0.5)
        o_ref[0, :, j * tn:(j + 1) * tn] = y.astype(o_ref.dtype)


def _proj_in(x, mod, w_bf16, q_cols):
    b, n, d = x.shape
    ncol = w_bf16.shape[1]
    tm = min(512, n)
    tn = 512
    kern = functools.partial(_proj_in_kernel, q_cols=q_cols, tn=tn)
    return pl.pallas_call(
        kern,
        grid=(b, n // tm),
        in_specs=[
            pl.BlockSpec((1, tm, d), lambda i, t: (i, t, 0)),
            pl.BlockSpec((1, 8, d), lambda i, t: (i, 0, 0)),
            pl.BlockSpec((d, ncol), lambda i, t: (0, 0)),
        ],
        out_specs=pl.BlockSpec((1, tm, ncol), lambda i, t: (i, t, 0)),
        out_shape=jax.ShapeDtypeStruct((b, n, ncol), BF16),
        compiler_params=_cparams(("parallel", "parallel")),
    )(x, mod, w_bf16)


def _gqa_proj_kernel(x_ref, mod_ref, w_ref, gain_ref, seg_ref, cos_ref, sin_ref, o_ref, *, n_norm):
    mod = mod_ref[0]
    h = (x_ref[0] * (1.0 + mod[1:2]) + mod[0:1]).astype(BF16)
    seg = seg_ref[...]
    cos = cos_ref[...]
    sin = sin_ref[...]
    ncol = w_ref.shape[1]
    for j in range(ncol // LANES):
        cols = slice(j * LANES, (j + 1) * LANES)
        y = jnp.dot(h, w_ref[:, cols], preferred_element_type=F32)
        if j < n_norm:
            sq = y * y
            sq_hi = sq.astype(BF16)
            sq_lo = (sq - sq_hi.astype(F32)).astype(BF16)
            ms = (jnp.dot(sq_hi, seg, preferred_element_type=F32)
                  + jnp.dot(sq_lo, seg, preferred_element_type=F32))
            yn = y * lax.rsqrt(ms + RMS_EPS) * gain_ref[:, cols]
            y = yn * cos + pltpu.roll(yn, LANES // 2, axis=1) * sin
        o_ref[0, :, cols] = y.astype(o_ref.dtype)


def _gqa_proj(x, mod, w_bf16, gain_row, seg, cos_t, sin_t, n_norm):
    b, n, d = x.shape
    ncol = w_bf16.shape[1]
    tm = min(512, n)
    kern = functools.partial(_gqa_proj_kernel, n_norm=n_norm)
    return pl.pallas_call(
        kern,
        grid=(b, n // tm),
        in_specs=[
            pl.BlockSpec((1, tm, d), lambda i, t: (i, t, 0)),
            pl.BlockSpec((1, 8, d), lambda i, t: (i, 0, 0)),
            pl.BlockSpec((d, ncol), lambda i, t: (0, 0)),
            pl.BlockSpec((1, ncol), lambda i, t: (0, 0)),
            pl.BlockSpec((LANES, LANES), lambda i, t: (0, 0)),
            pl.BlockSpec((tm, LANES), lambda i, t: (t, 0)),
            pl.BlockSpec((tm, LANES), lambda i, t: (t, 0)),
        ],
        out_specs=pl.BlockSpec((1, tm, ncol), lambda i, t: (i, t, 0)),
        out_shape=jax.ShapeDtypeStruct((b, n, ncol), BF16),
        compiler_params=_cparams(("parallel", "parallel")),
    )(x, mod, w_bf16, gain_row, seg, cos_t, sin_t)


def _proj_out_kernel(o_ref, x_ref, mod_ref, w_ref, lng_ref, lnb_ref, r_ref,
                     x1_ref, hf_ref, aff_ref, *, n_exp):
    y = jnp.dot(o_ref[0], w_ref[...], preferred_element_type=F32)
    x1, hf, aff = _post_mixer(x_ref[0], y, mod_ref[0], lng_ref[...], lnb_ref[...], r_ref[...], n_exp)
    x1_ref[0] = x1
    hf_ref[0] = hf
    aff_ref[0] = aff


def _proj_out(o, x, mod, w_bf16, lng, lnb, router_p, n_exp):
    b, n, d = x.shape
    tm = min(256, n)
    kern = functools.partial(_proj_out_kernel, n_exp=n_exp)
    return pl.pallas_call(
        kern,
        grid=(b, n // tm),
        in_specs=[
            pl.BlockSpec((1, tm, o.shape[2]), lambda i, t: (i, t, 0)),
            pl.BlockSpec((1, tm, d), lambda i, t: (i, t, 0)),
            pl.BlockSpec((1, 8, d), lambda i, t: (i, 0, 0)),
            pl.BlockSpec(w_bf16.shape, lambda i, t: (0, 0)),
            pl.BlockSpec((1, d), lambda i, t: (0, 0)),
            pl.BlockSpec((1, d), lambda i, t: (0, 0)),
            pl.BlockSpec((d, LANES), lambda i, t: (0, 0)),
        ],
        out_specs=_post_out_specs(tm, d, n_exp),
        out_shape=_post_outs(b, n, d, n_exp),
        compiler_params=_cparams(("parallel", "parallel")),
    )(o, x, mod, w_bf16, lng.reshape(1, d), lnb.reshape(1, d), router_p)


def _softmax_pv(scores, values, lane_lo):
    m = functools.reduce(jnp.maximum, [jnp.max(s, axis=1, keepdims=True) for s in scores])
    ps = [jnp.exp(s - m) for s in scores]
    l = functools.reduce(lambda a, c: a + c, [jnp.sum(p, axis=1, keepdims=True) for p in ps])
    o = None
    for p, v in zip(ps, values):
        c = jnp.dot(p.astype(BF16), v, preferred_element_type=F32)
        o = c if o is None else o + c
    del lane_lo
    return o / l


def _na_kernel(q_ref, ka_ref, kb_ref, kc_ref, va_ref, vb_ref, vc_ref, kx_ref, vx_ref, bias_ref,
               o_ref, *, rows):
    i = pl.program_id(1)
    nb = pl.num_programs(1)
    tq = q_ref.shape[1]
    rq0 = i * NA_QROWS
    ks = jnp.clip(i - 1, 0, nb - NA_KBLOCKS) * NA_QROWS
    qi = lax.broadcasted_iota(I32, (tq, tq), 0)
    ki = lax.broadcasted_iota(I32, (tq, tq), 1)
    q_row = rq0 + qi // GRID_W
    q_col = qi % GRID_W
    k_col = ki % GRID_W
    r0 = jnp.clip(q_row - NA_KH // 2, 0, rows - NA_KH)
    c0 = jnp.clip(q_col - NA_KW // 2, 0, GRID_W - NA_KW)
    col_ok = (k_col >= c0) & (k_col < c0 + NA_KW)
    mask_add = []
    for blk in range(NA_KBLOCKS):
        k_row = ks + blk * NA_QROWS + ki // GRID_W
        ok = col_ok & (k_row >= r0) & (k_row < r0 + NA_KH)
        mask_add.append(jnp.where(ok, 0.0, NEG_INF))
    lane = lax.broadcasted_iota(I32, (1, LANES), 1)
    k_refs = (ka_ref, kb_ref, kc_ref)
    v_refs = (va_ref, vb_ref, vc_ref)
    n_pairs = q_ref.shape[2] // LANES
    for p in range(n_pairs):
        cols = slice(p * LANES, (p + 1) * LANES)
        q2 = q_ref[0, :, cols]
        kt = [r[0, :, cols] for r in k_refs] + [kx_ref[0, :, cols]]
        vt = [r[0, :, cols] for r in v_refs] + [vx_ref[0, :, cols]]
        outs = []
        for hh in range(2):
            head = 2 * p + hh
            hmask = (lane // HEAD_DIM) == hh
            qm = jnp.where(hmask, q2, jnp.zeros_like(q2))
            scores = []
            for blk in range(NA_KBLOCKS):
                s = lax.dot_general(qm, kt[blk], (((1,), (1,)), ((), ())),
                                    preferred_element_type=F32)
                bands = []
                for qr in range(NA_QROWS):
                    halves = []
                    for kp in range(NA_QROWS // 2):
                        dr = ks + blk * NA_QROWS + 2 * kp - (rq0 + qr)
                        idx = jnp.clip(dr + NA_KH, 0, 2 * NA_KH - 1)
                        halves.append(bias_ref[head, idx])
                    bands.append(jnp.concatenate(halves, axis=1))
                bias = jnp.concatenate(bands, axis=0)
                scores.append(s + bias + mask_add[blk])
            scores.append(lax.dot_general(qm, kt[NA_KBLOCKS], (((1,), (1,)), ((), ())),
                                          preferred_element_type=F32))
            outs.append(_softmax_pv(scores, vt, None))
        o_ref[0, :, cols] = jnp.where((lane // HEAD_DIM) == 0, outs[0], outs[1]).astype(o_ref.dtype)


def _na_bias_table(rpb):
    col = np.arange(GRID_W)
    dc = np.clip(col[None, :] - col[:, None] + (NA_KW - 1), 0, 2 * NA_KW - 2)
    t = rpb[:, :, dc]
    t_first = jnp.concatenate([t[:, :1], t], axis=1)
    t_next = jnp.concatenate([t, t[:, -1:]], axis=1)
    return jnp.concatenate([t_first, t_next], axis=-1).astype(F32)


def _na_attention(qkv, qkv_c, bias_tab):
    b, n, d3 = qkv.shape
    d = d3 // 3
    l = qkv_c.shape[1]
    rows = n // GRID_W
    tq = NA_QROWS * GRID_W
    nb = n // tq
    heads = d // HEAD_DIM

    def kmap(off, col):
        return lambda i, t: (i, jnp.clip(t - 1, 0, nb - NA_KBLOCKS) + off, col)

    kern = functools.partial(_na_kernel, rows=rows)
    return pl.pallas_call(
        kern,
        grid=(b, nb),
        in_specs=[
            pl.BlockSpec((1, tq, d), lambda i, t: (i, t, 0)),
            pl.BlockSpec((1, tq, d), kmap(0, 1)),
            pl.BlockSpec((1, tq, d), kmap(1, 1)),
            pl.BlockSpec((1, tq, d), kmap(2, 1)),
            pl.BlockSpec((1, tq, d), kmap(0, 2)),
            pl.BlockSpec((1, tq, d), kmap(1, 2)),
            pl.BlockSpec((1, tq, d), kmap(2, 2)),
            pl.BlockSpec((1, l, d), lambda i, t: (i, 0, 1)),
            pl.BlockSpec((1, l, d), lambda i, t: (i, 0, 2)),
            pl.BlockSpec((heads, 2 * NA_KH, GRID_W, 2 * GRID_W), lambda i, t: (0, 0, 0, 0)),
        ],
        out_specs=pl.BlockSpec((1, tq, d), lambda i, t: (i, t, 0)),
        out_shape=jax.ShapeDtypeStruct((b, n, d), BF16),
        compiler_params=_cparams(("parallel", "parallel")),
    )(qkv, qkv, qkv, qkv, qkv, qkv, qkv, qkv_c, qkv_c, bias_tab)


def _ctx_attn_kernel(q_ref, k_ref, v_ref, o_ref):
    lane = lax.broadcasted_iota(I32, (1, LANES), 1)
    for p in range(q_ref.shape[2] // LANES):
        cols = slice(p * LANES, (p + 1) * LANES)
        q2 = q_ref[0, :, cols]
        k2 = k_ref[0, :, cols]
        v2 = v_ref[0, :, cols]
        outs = []
        for hh in range(2):
            qm = jnp.where((lane // HEAD_DIM) == hh, q2, jnp.zeros_like(q2))
            s = lax.dot_general(qm, k2, (((1,), (1,)), ((), ())), preferred_element_type=F32)
            outs.append(_softmax_pv([s], [v2], None))
        o_ref[0, :, cols] = jnp.where((lane // HEAD_DIM) == 0, outs[0], outs[1]).astype(o_ref.dtype)


def _ctx_attention(qkv_c):
    b, l, d3 = qkv_c.shape
    d = d3 // 3
    return pl.pallas_call(
        _ctx_attn_kernel,
        grid=(b,),
        in_specs=[pl.BlockSpec((1, l, d), lambda i: (i, 0, 0)),
                  pl.BlockSpec((1, l, d), lambda i: (i, 0, 1)),
                  pl.BlockSpec((1, l, d), lambda i: (i, 0, 2))],
        out_specs=pl.BlockSpec((1, l, d), lambda i: (i, 0, 0)),
        out_shape=jax.ShapeDtypeStruct((b, l, d), BF16),
        compiler_params=_cparams(("parallel",)),
    )(qkv_c, qkv_c, qkv_c)


def _gqa_kernel(q_ref, k_ref, v_ref, o_ref, *, tk):
    tq = q_ref.shape[1]
    nk = k_ref.shape[1] // tk
    lane = lax.broadcasted_iota(I32, (1, LANES), 1)
    for sl in range(q_ref.shape[2] // LANES):
        cols = slice(sl * LANES, (sl + 1) * LANES)
        q2 = q_ref[0, :, cols]
        outs = []
        for hh in range(2):
            qm = jnp.where(((lane // (HEAD_DIM // 2)) % 2) == hh, q2, jnp.zeros_like(q2))

            def body(j, carry, qm=qm):
                m, l, acc = carry
                k0 = pl.multiple_of(j * tk, tk)
                kt = k_ref[0, pl.ds(k0, tk), :]
                vt = v_ref[0, pl.ds(k0, tk), :]
                s = lax.dot_general(qm, kt, (((1,), (1,)), ((), ())), preferred_element_type=F32)
                m_new = jnp.maximum(m, jnp.max(s, axis=1, keepdims=True))
                a = jnp.exp(m - m_new)
                p = jnp.exp(s - m_new)
                l = a * l + jnp.sum(p, axis=1, keepdims=True)
                acc = a * acc + jnp.dot(p.astype(BF16), vt, preferred_element_type=F32)
                return m_new, l, acc

            m0 = jnp.full((tq, 1), NEG_INF, F32)
            l0 = jnp.zeros((tq, 1), F32)
            a0 = jnp.zeros((tq, LANES), F32)
            _, l, acc = lax.fori_loop(0, nk, body, (m0, l0, a0))
            outs.append(acc / l)
        o_ref[0, :, cols] = jnp.where((lane // HEAD_DIM) == 0, outs[0], outs[1]).astype(o_ref.dtype)


def _gqa_attention(q, kcat, vcat):
    b, n, dq = q.shape
    nk_tot = kcat.shape[1]
    n_slab = kcat.shape[2] // LANES
    q_per = dq // n_slab
    tq = min(256, n)
    tk = 256
    for cand in (1024, 768, 512):
        if nk_tot % cand == 0:
            tk = cand
            break
    kern = functools.partial(_gqa_kernel, tk=tk)
    return pl.pallas_call(
        kern,
        grid=(b, n_slab, n // tq),
        in_specs=[
            pl.BlockSpec((1, tq, q_per), lambda i, p, t: (i, t, p)),
            pl.BlockSpec((1, nk_tot, LANES), lambda i, p, t: (i, 0, p)),
            pl.BlockSpec((1, nk_tot, LANES), lambda i, p, t: (i, 0, p)),
        ],
        out_specs=pl.BlockSpec((1, tq, q_per), lambda i, p, t: (i, t, p)),
        out_shape=jax.ShapeDtypeStruct((b, n, dq), BF16),
        compiler_params=_cparams(("parallel", "parallel", "parallel")),
    )(q, kcat, vcat)


def _prefix_count(mask_f32, tri, out_cb):
    e, n = mask_f32.shape
    c = tri.shape[0]
    carry = jnp.zeros((e, 1), F32)
    for j in range(n // c):
        chunk = mask_f32[:, j * c:(j + 1) * c]
        inc = jnp.dot(chunk.astype(BF16), tri, preferred_element_type=F32) + carry
        out_cb(j, c, chunk, inc)
        carry = inc[:, c - 1:c]


def _topk_kernel(aff_ref, rank_ref, cum_ref, eqx_ref, *, cap):
    a = aff_ref[0]
    e, n = a.shape
    bits = lax.bitcast_convert_type(a, I32)

    def body(i, thr):
        cand = thr | (jnp.int32(1) << (30 - i))
        cnt = jnp.sum(jnp.where(bits >= cand, 1.0, 0.0), axis=1, keepdims=True)
        return jnp.where(cnt >= cap, cand, thr)

    thr = lax.fori_loop(0, 31, body, jnp.zeros((e, 1), I32))
    gt = bits > thr
    eq = bits == thr
    need = cap - jnp.sum(jnp.where(gt, 1.0, 0.0), axis=1, keepdims=True)
    c = min(256, n)
    tri = jnp.where(lax.broadcasted_iota(I32, (c, c), 0) <= lax.broadcasted_iota(I32, (c, c), 1),
                    1.0, 0.0).astype(BF16)

    def eq_cb(j, c, chunk, inc):
        eqx_ref[:, j * c:(j + 1) * c] = inc - chunk

    _prefix_count(jnp.where(eq, 1.0, 0.0), tri, eq_cb)
    sel = gt | (eq & (eqx_ref[...] < need))

    def sel_cb(j, c, chunk, inc):
        excl = (inc - chunk).astype(I32)
        cum_ref[0, :, j * c:(j + 1) * c] = excl
        rank_ref[0, :, j * c:(j + 1) * c] = jnp.where(chunk > 0.5, excl, -1)

    _prefix_count(jnp.where(sel, 1.0, 0.0), tri, sel_cb)


def _topk(aff, cap):
    b, e, n = aff.shape
    kern = functools.partial(_topk_kernel, cap=cap)
    return pl.pallas_call(
        kern,
        grid=(b,),
        in_specs=[pl.BlockSpec((1, e, n), lambda i: (i, 0, 0))],
        out_specs=(pl.BlockSpec((1, e, n), lambda i: (i, 0, 0)),
                   pl.BlockSpec((1, e, n), lambda i: (i, 0, 0))),
        out_shape=(jax.ShapeDtypeStruct((b, e, n), I32), jax.ShapeDtypeStruct((b, e, n), I32)),
        scratch_shapes=[pltpu.VMEM((e, n), F32)],
        compiler_params=_cparams(("parallel",)),
    )(aff)


def _slot_onehot(rank, starts, lows, k_new, rows_per, n_exp):
    t = rank.shape[1]
    j_iota = lax.broadcasted_iota(I32, (rows_per, t), 0)
    rows = []
    for e in range(n_exp):
        r = rank[e:e + 1, :]
        hit = (r - starts[e] == j_iota) & (r >= lows[e]) & (r < lows[e] + k_new)
        rows.append(jnp.where(hit, 1.0, 0.0))
    return jnp.concatenate(rows, axis=0)


def _align_down(v, a):
    return pl.multiple_of((v // a) * a, a)


def _gather_kernel(base_ref, rank_ref, hf_ref, xs_hbm, stage, carry, sem, *, n_exp, n_tiles, cap):
    b = pl.program_id(0)
    t = pl.program_id(1)
    off = (b * (n_tiles + 1) + t) * n_exp
    base = [base_ref[off + e] for e in range(n_exp)]
    cnt = [base_ref[off + n_exp + e] - base[e] for e in range(n_exp)]
    kmax = functools.reduce(jnp.maximum, cnt)
    rank = rank_ref[0]
    hf = hf_ref[0]

    @pl.when(t == 0)
    def _():
        carry[...] = jnp.zeros_like(carry)
        stage[...] = jnp.zeros_like(stage)
        pads = [pltpu.make_async_copy(stage.at[e * GATHER_ROWS:(e + 1) * GATHER_ROWS, :],
                                      xs_hbm.at[e, b, cap:cap + GATHER_ROWS, :], sem.at[e])
                for e in range(n_exp)]
        for cp in pads:
            cp.start()
        for cp in pads:
            cp.wait()

    def round_body(r, c):
        lows = [base[e] + r * ROUTE_K for e in range(n_exp)]
        starts = [_align_down(lows[e], ROW_ALIGN) for e in range(n_exp)]
        onehot = _slot_onehot(rank, starts, lows, ROUTE_K, GATHER_ROWS, n_exp).astype(BF16)
        stage[...] = jnp.dot(onehot, hf, preferred_element_type=F32).astype(BF16)
        copies = []
        for e in range(n_exp):
            r0 = e * GATHER_ROWS
            stage[r0:r0 + ROW_ALIGN, :] += carry[e * ROW_ALIGN:(e + 1) * ROW_ALIGN, :]
            copies.append(pltpu.make_async_copy(
                stage.at[r0:r0 + GATHER_ROWS, :],
                xs_hbm.at[e, b, pl.ds(starts[e], GATHER_ROWS), :],
                sem.at[e]))
        for e in range(n_exp):
            @pl.when(cnt[e] > r * ROUTE_K)
            def _(e=e):
                copies[e].start()
                filled = lows[e] + jnp.minimum(cnt[e] - r * ROUTE_K, ROUTE_K)
                shift = _align_down(filled, ROW_ALIGN) - starts[e]
                src = pl.multiple_of(e * GATHER_ROWS + shift, ROW_ALIGN)
                carry[e * ROW_ALIGN:(e + 1) * ROW_ALIGN, :] = stage[pl.ds(src, ROW_ALIGN), :]
        for e in range(n_exp):
            @pl.when(cnt[e] > r * ROUTE_K)
            def _(e=e):
                copies[e].wait()
        return c

    lax.fori_loop(0, (kmax + ROUTE_K - 1) // ROUTE_K, round_body, 0)


def _gather(base_flat, rank, hf, cap):
    b, n, d = hf.shape
    n_exp = rank.shape[1]
    n_tiles = n // ROUTE_T
    kern = functools.partial(_gather_kernel, n_exp=n_exp, n_tiles=n_tiles, cap=cap)
    return pl.pallas_call(
        kern,
        grid_spec=pltpu.PrefetchScalarGridSpec(
            num_scalar_prefetch=1,
            grid=(b, n_tiles),
            in_specs=[
                pl.BlockSpec((1, n_exp, ROUTE_T), lambda i, t, s: (i, 0, t)),
                pl.BlockSpec((1, ROUTE_T, d), lambda i, t, s: (i, t, 0)),
            ],
            out_specs=pl.BlockSpec(memory_space=pl.ANY),
            scratch_shapes=[pltpu.VMEM((n_exp * GATHER_ROWS, d), BF16),
                            pltpu.VMEM((n_exp * ROW_ALIGN, d), BF16),
                            pltpu.SemaphoreType.DMA((n_exp,))],
        ),
        out_shape=jax.ShapeDtypeStruct((n_exp, b, cap + GATHER_ROWS, d), BF16),
        compiler_params=_cparams(("arbitrary", "arbitrary")),
    )(base_flat, rank, hf)


def _ffn_kernel(x_ref, w1_ref, w3_ref, w2_ref, y_ref):
    f = pl.program_id(2)
    bb, cap, d = x_ref.shape[1:]
    x = x_ref[0].reshape(bb * cap, d)
    h1 = jnp.dot(x, w1_ref[0].astype(BF16), preferred_element_type=F32)
    h3 = jnp.dot(x, w3_ref[0].astype(BF16), preferred_element_type=F32)
    hid = (_silu(h1) * h3).astype(BF16)
    y = jnp.dot(hid, w2_ref[0].astype(BF16), preferred_element_type=F32).reshape(bb, cap, d)

    @pl.when(f == 0)
    def _():
        y_ref[0] = y

    @pl.when(f > 0)
    def _():
        y_ref[0] += y


def _ffn(xs, w1, w3, w2, cap):
    n_exp, b, _, d = xs.shape
    ff = w1.shape[2]
    bb = b if b * cap <= 1024 else 1
    tf = min(512, ff)
    return pl.pallas_call(
        _ffn_kernel,
        grid=(n_exp, b // bb, ff // tf),
        in_specs=[
            pl.BlockSpec((1, bb, cap, d), lambda e, i, f: (e, i, 0, 0)),
            pl.BlockSpec((1, d, tf), lambda e, i, f: (e, 0, f)),
            pl.BlockSpec((1, d, tf), lambda e, i, f: (e, 0, f)),
            pl.BlockSpec((1, tf, d), lambda e, i, f: (e, f, 0)),
        ],
        out_specs=pl.BlockSpec((1, bb, cap, d), lambda e, i, f: (e, i, 0, 0)),
        out_shape=jax.ShapeDtypeStruct((n_exp, b, cap, d), F32),
        compiler_params=_cparams(("parallel", "parallel", "arbitrary")),
    )(xs, w1, w3, w2)


def _combine_kernel(base_ref, rank_ref, aff_ref, x1_ref, mod_ref, lng_ref, lnb_ref, y_hbm,
                    out_ref, ybuf, sem, *, n_exp, n_tiles, cap, win):
    k_new = win - ROW_ALIGN
    b = pl.program_id(0)
    t = pl.program_id(1)
    off = (b * (n_tiles + 1) + t) * n_exp
    base = [base_ref[off + e] for e in range(n_exp)]
    cnt = [base_ref[off + n_exp + e] - base[e] for e in range(n_exp)]
    kmax = functools.reduce(jnp.maximum, cnt)
    rank = rank_ref[0]
    aff = aff_ref[0]
    tt = rank.shape[1]

    @pl.when((b == 0) & (t == 0))
    def _():
        ybuf[...] = jnp.zeros_like(ybuf)

    aff_rows = jnp.concatenate(
        [jnp.broadcast_to(aff[e:e + 1, :], (win, tt)) for e in range(n_exp)], axis=0)

    def round_body(r, f):
        lows = [base[e] + r * k_new for e in range(n_exp)]
        starts = [pl.multiple_of(jnp.minimum(_align_down(lows[e], ROW_ALIGN), cap - win), ROW_ALIGN)
                  for e in range(n_exp)]
        copies = []
        for e in range(n_exp):
            copies.append(pltpu.make_async_copy(
                y_hbm.at[e, b, pl.ds(starts[e], win), :],
                ybuf.at[e * win:(e + 1) * win, :],
                sem.at[e]))
        for e in range(n_exp):
            @pl.when(cnt[e] > r * k_new)
            def _(e=e):
                copies[e].start()
        onehot = _slot_onehot(rank, starts, lows, k_new, win, n_exp)
        gate = jnp.sum(onehot * aff_rows, axis=1, keepdims=True)
        used = jnp.sum(onehot, axis=1, keepdims=True) > 0.5
        for e in range(n_exp):
            @pl.when(cnt[e] > r * k_new)
            def _(e=e):
                copies[e].wait()
        yv = jnp.where(used, ybuf[...] * gate, 0.0)
        hi = yv.astype(BF16)
        lo = (yv - hi.astype(F32)).astype(BF16)
        oh_t = onehot.T.astype(BF16)
        return (f + jnp.dot(oh_t, hi, preferred_element_type=F32)
                + jnp.dot(oh_t, lo, preferred_element_type=F32))

    d = x1_ref.shape[2]
    f = lax.fori_loop(0, (kmax + k_new - 1) // k_new, round_body, jnp.zeros((tt, d), F32))
    mod = mod_ref[0]
    out_ref[0] = _layer_norm(ALPHA * x1_ref[0] + mod[5:6] * f, lng_ref[...], lnb_ref[...])


def _combine(base_flat, rank, aff, x1, mod, lng, lnb, y, cap):
    b, n, d = x1.shape
    n_exp = rank.shape[1]
    n_tiles = n // ROUTE_T
    win = min(GATHER_ROWS, cap)
    assert cap % ROW_ALIGN == 0 and win > ROW_ALIGN
    kern = functools.partial(_combine_kernel, n_exp=n_exp, n_tiles=n_tiles, cap=cap, win=win)
    return pl.pallas_call(
        kern,
        grid_spec=pltpu.PrefetchScalarGridSpec(
            num_scalar_prefetch=1,
            grid=(b, n_tiles),
            in_specs=[
                pl.BlockSpec((1, n_exp, ROUTE_T), lambda i, t, s: (i, 0, t)),
                pl.BlockSpec((1, n_exp, ROUTE_T), lambda i, t, s: (i, 0, t)),
                pl.BlockSpec((1, ROUTE_T, d), lambda i, t, s: (i, t, 0)),
                pl.BlockSpec((1, 8, d), lambda i, t, s: (i, 0, 0)),
                pl.BlockSpec((1, d), lambda i, t, s: (0, 0)),
                pl.BlockSpec((1, d), lambda i, t, s: (0, 0)),
                pl.BlockSpec(memory_space=pl.ANY),
            ],
            out_specs=pl.BlockSpec((1, ROUTE_T, d), lambda i, t, s: (i, t, 0)),
            scratch_shapes=[pltpu.VMEM((n_exp * win, d), F32),
                            pltpu.SemaphoreType.DMA((n_exp,))],
        ),
        out_shape=jax.ShapeDtypeStruct((b, n, d), F32),
        compiler_params=_cparams(("arbitrary", "arbitrary")),
    )(base_flat, rank, aff, x1, mod, lng.reshape(1, d), lnb.reshape(1, d), y)


def _moe(x1, hf, aff, mod, lng, lnb, w1, w3, w2):
    b, n, d = x1.shape
    n_exp = aff.shape[1]
    cap = EC_CAPACITY * n // n_exp
    rank, cum = _topk(aff, cap)
    tile_base = jnp.swapaxes(cum[:, :, ::ROUTE_T], 1, 2)
    base = jnp.concatenate([tile_base, jnp.full((b, 1, n_exp), cap, I32)], axis=1).reshape(-1)
    xs = _gather(base, rank, hf, cap)
    y = _ffn(xs, w1, w3, w2, cap)
    return _combine(base, rank, aff, x1, mod, lng, lnb, y, cap)


def _gqa_layout(n_q_heads):
    group = n_q_heads // GQA_KV_HEADS
    half = HEAD_DIM // 2
    ev = np.arange(half) * 2
    od = ev + 1

    def slab(col_a, col_b):
        return np.concatenate([col_a + ev, col_b + ev, col_a + od, col_b + od])

    q_cols, k_cols, o_rows, gain_idx = [], [], [], []
    dq = n_q_heads * HEAD_DIM
    for p in range(GQA_KV_HEADS // 2):
        for i in range(group):
            a = (2 * p) * group + i
            c = (2 * p + 1) * group + i
            q_cols.append(slab(a * HEAD_DIM, c * HEAD_DIM))
            o_rows.append(np.concatenate([a * HEAD_DIM + np.arange(HEAD_DIM),
                                          c * HEAD_DIM + np.arange(HEAD_DIM)]))
    for p in range(GQA_KV_HEADS // 2):
        k_cols.append(dq + slab(2 * p * HEAD_DIM, (2 * p + 1) * HEAD_DIM))
    lane_dim = np.concatenate([ev, ev, od, od])
    return (np.concatenate(q_cols), np.concatenate(k_cols), np.concatenate(o_rows), lane_dim)


def _rope_tables(n):
    t = jnp.arange(n, dtype=I32)
    row = (t // GRID_W).astype(F32)
    col = (t % GRID_W).astype(F32)
    axis_dims = HEAD_DIM // 2
    inv_freq = jnp.power(ROPE_THETA, -jnp.arange(0, axis_dims, 2, dtype=F32) / axis_dims)
    ang = jnp.concatenate([row[:, None] * inv_freq, col[:, None] * inv_freq], axis=-1)
    cos, sin = jnp.cos(ang), jnp.sin(ang)
    cos_t = jnp.concatenate([cos] * 4, axis=1)
    sin_t = jnp.concatenate([-sin, -sin, sin, sin], axis=1)
    return cos_t, sin_t


def kernel(x, c, ctx, c_ctx, mod_w, mod_b, ln_g, ln_b, pool_w, pool_scale, na_wqkv, na_wo, na_rpb,
           gqa_wqkv, gqa_q_norm, gqa_k_norm, gqa_wo, moe_router, moe_w1, moe_w3, moe_w2):
    bsz, n, d = x.shape
    l = ctx.shape[1]
    n_exp = moe_router.shape[2]
    depth = mod_w.shape[0]
    cc = jnp.zeros((8, d), F32).at[:bsz].set(c).at[bsz].set(c_ctx)
    mod_all = _modulation(cc, mod_w, mod_b)

    for i in range(depth):
        m = i % N_MIXERS
        j = i // N_MIXERS
        update_ctx = any(k % N_MIXERS != 0 for k in range(i + 1, depth))
        ctx_keys = m != 0
        mod6 = mod_all[i].reshape(8, N_MOD, d)
        mod_x = jnp.zeros((bsz, 8, d), F32).at[:, :N_MOD].set(mod6[:bsz])
        mod_c = jnp.zeros((bsz, 8, d), F32).at[:, :N_MOD].set(
            jnp.broadcast_to(mod6[bsz][None], (bsz, N_MOD, d)))
        router_p = _pad_router(moe_router[i])
        lng0, lnb0, lng1, lnb1 = ln_g[i, 0], ln_b[i, 0], ln_g[i, 1], ln_b[i, 1]
        post_c = None
        if m == 0:
            post_x = _pool_layer(x, mod_x, pool_w[j], pool_scale[j], lng0, lnb0, router_p, n_exp)
            if update_ctx:
                post_c = _pool_layer(ctx, mod_c, pool_w[j], pool_scale[j], lng0, lnb0, router_p,
                                     n_exp)
        elif m == 1:
            wqkv = na_wqkv[j].astype(BF16)
            wo = na_wo[j].astype(BF16)
            qkv = _proj_in(x, mod_x, wqkv, d)
            qkv_c = _proj_in(ctx, mod_c, wqkv, d)
            o = _na_attention(qkv, qkv_c, _na_bias_table(na_rpb[j]))
            post_x = _proj_out(o, x, mod_x, wo, lng0, lnb0, router_p, n_exp)
            if update_ctx:
                oc = _ctx_attention(qkv_c)
                post_c = _proj_out(oc, ctx, mod_c, wo, lng0, lnb0, router_p, n_exp)
        else:
            n_q_heads = gqa_wo.shape[1] // HEAD_DIM
            dq = n_q_heads * HEAD_DIM
            dkv = GQA_KV_HEADS * HEAD_DIM
            q_cols, k_cols, o_rows, lane_dim = _gqa_layout(n_q_heads)
            v_cols = dq + dkv + np.arange(dkv)
            w_perm = gqa_wqkv[j][:, np.concatenate([q_cols, k_cols, v_cols])].astype(BF16)
            wo = gqa_wo[j][o_rows].astype(BF16)
            gain_row = jnp.concatenate(
                [jnp.tile(gqa_q_norm[j][lane_dim] * (HEAD_DIM ** -0.5), dq // LANES),
                 jnp.tile(gqa_k_norm[j][lane_dim], dkv // LANES),
                 jnp.ones((dkv,), F32)]).reshape(1, -1)
            lane_head = (np.arange(LANES) // (HEAD_DIM // 2)) % 2
            seg = jnp.asarray((lane_head[:, None] == lane_head[None, :]) / HEAD_DIM, BF16)
            cos_t, sin_t = _rope_tables(n)
            n_norm = (dq + dkv) // LANES
            qkv = _gqa_proj(x, mod_x, w_perm, gain_row, seg, cos_t, sin_t, n_norm)
            qkv_c = _gqa_proj(ctx, mod_c, w_perm, gain_row, seg, jnp.ones((l, LANES), F32),
                              jnp.zeros((l, LANES), F32), n_norm)
            kcat = jnp.concatenate([qkv[:, :, dq:dq + dkv], qkv_c[:, :, dq:dq + dkv]], axis=1)
            vcat = jnp.concatenate([qkv[:, :, dq + dkv:], qkv_c[:, :, dq + dkv:]], axis=1)
            o = _gqa_attention(qkv[:, :, :dq], kcat, vcat)
            post_x = _proj_out(o, x, mod_x, wo, lng0, lnb0, router_p, n_exp)
            if update_ctx:
                raise NotImplementedError("context update after a GQA layer is not part of this stack")
        x = _moe(*post_x, mod_x, lng1, lnb1, moe_w1[i], moe_w3[i], moe_w2[i])
        if update_ctx:
            ctx = _moe(*post_c, mod_c, lng1, lnb1, moe_w1[i], moe_w3[i], moe_w2[i])
    return x
```

```python
import functools
import math

import jax
import jax.numpy as jnp
import numpy as np
from jax import lax
from jax.experimental import pallas as pl
from jax.experimental.pallas import tpu as pltpu

F32 = jnp.float32
BF16 = jnp.bfloat16
I32 = jnp.int32
HIGHEST = lax.Precision.HIGHEST

DEPTH = 4
N_MIXERS = 3
GRID_W = 64
HEAD_DIM = 64
POOL_WINDOWS = (2, 4, 8, 16)
POOL_HALO = 8
NA_KH = 8
NA_KW = 16
NA_QROWS = 4
NA_KBLOCKS = 3
GQA_KV_HEADS = 4
ROPE_THETA = 10000.0
EC_CAPACITY = 2
N_MOD = 6
LN_EPS = 1e-5
RMS_EPS = 1e-6
ALPHA = (2.0 * DEPTH) ** 0.25
LANES = 128
ROUTE_T = 128
ROUTE_K = 32
ROW_ALIGN = 16
GATHER_ROWS = ROUTE_K + ROW_ALIGN
VMEM_LIMIT = 56 * 1024 * 1024
NEG_INF = -1e30
LOG2_E = math.log2(math.e)
EXP2_HEADROOM = 64.0


def _cparams(sem, vmem=VMEM_LIMIT):
    return pltpu.CompilerParams(dimension_semantics=sem, vmem_limit_bytes=vmem)


def _silu(v):
    return v / (1.0 + jnp.exp(-v))


def _mod_kernel(c_ref, w_ref, b_ref, o_ref):
    s = _silu(c_ref[...])
    o_ref[0] = jnp.dot(s, w_ref[0], precision=HIGHEST, preferred_element_type=F32) + b_ref[0]


def _modulation(cc, mod_w, mod_b):
    depth, d, nd = mod_w.shape
    tn = nd // 4
    return pl.pallas_call(
        _mod_kernel,
        grid=(depth, nd // tn),
        in_specs=[
            pl.BlockSpec((8, d), lambda i, j: (0, 0)),
            pl.BlockSpec((1, d, tn), lambda i, j: (i, 0, j)),
            pl.BlockSpec((1, 1, tn), lambda i, j: (i, 0, j)),
        ],
        out_specs=pl.BlockSpec((1, 8, tn), lambda i, j: (i, 0, j)),
        out_shape=jax.ShapeDtypeStruct((depth, 8, nd), F32),
        compiler_params=_cparams(("parallel", "parallel")),
    )(cc, mod_w, mod_b.reshape(depth, 1, nd))


def _layer_norm(z, g, b):
    mu = jnp.mean(z, axis=-1, keepdims=True)
    zc = z - mu
    var = jnp.mean(zc * zc, axis=-1, keepdims=True)
    return zc * lax.rsqrt(var + LN_EPS) * g + b


def _post_mixer(x, y, mod, lng, lnb, router, n_exp):
    x1 = _layer_norm(ALPHA * x + mod[2:3] * y, lng, lnb)
    hf = x1 * (1.0 + mod[4:5]) + mod[3:4]
    logits = jnp.dot(hf, router, precision=HIGHEST, preferred_element_type=F32)
    lt = logits.T[:n_exp]
    m = jnp.max(lt, axis=0, keepdims=True)
    p = jnp.exp(lt - m)
    aff = p / jnp.sum(p, axis=0, keepdims=True)
    return x1, hf.astype(BF16), aff


def _post_outs(b, n, d, n_exp):
    return (jax.ShapeDtypeStruct((b, n, d), F32),
            jax.ShapeDtypeStruct((b, n, d), BF16),
            jax.ShapeDtypeStruct((b, n_exp, n), F32))


def _post_out_specs(tm, d, n_exp):
    return (pl.BlockSpec((1, tm, d), lambda b, t: (b, t, 0)),
            pl.BlockSpec((1, tm, d), lambda b, t: (b, t, 0)),
            pl.BlockSpec((1, n_exp, tm), lambda b, t: (b, 0, t)))


def _pad_router(router):
    d, e = router.shape
    return jnp.zeros((d, LANES), F32).at[:, :e].set(router)


def _pool_kernel(x_ref, xp_ref, xn_ref, mod_ref, pw_ref, ps_ref, lng_ref, lnb_ref, r_ref,
                 x1_ref, hf_ref, aff_ref, buf, *, n, tm, n_exp):
    t = pl.program_id(1)
    nt = pl.num_programs(1)
    mod = mod_ref[0]
    x = x_ref[0]
    sc = 1.0 + mod[1:2]
    sh = mod[0:1]
    h = x * sc + sh
    hp = xp_ref[0] * sc + sh
    hn = xn_ref[0] * sc + sh
    buf[0:POOL_HALO, :] = jnp.where(t > 0, hp, 0.0)
    buf[POOL_HALO:POOL_HALO + tm, :] = h
    buf[POOL_HALO + tm:, :] = jnp.where(t < nt - 1, hn, 0.0)
    pos = t * tm + lax.broadcasted_iota(I32, (tm, 1), 0)
    ch = x.shape[1] // len(POOL_WINDOWS)
    parts = []
    for g, w in enumerate(POOL_WINDOWS):
        cols = slice(g * ch, (g + 1) * ch)
        acc = None
        for o in range(-(w // 2), w - w // 2):
            v = buf[POOL_HALO + o:POOL_HALO + o + tm, cols]
            acc = v if acc is None else acc + v
        lo = jnp.maximum(pos - w // 2, 0)
        hi = jnp.minimum(pos + (w - w // 2 - 1), n - 1)
        cnt = (hi - lo + 1).astype(F32)
        dlt = acc / cnt - h[:, cols]
        parts.append(jnp.dot(dlt.astype(BF16), pw_ref[g], preferred_element_type=F32))
    y = jnp.concatenate(parts, axis=1) * ps_ref[...]
    x1, hf, aff = _post_mixer(x, y, mod, lng_ref[...], lnb_ref[...], r_ref[...], n_exp)
    x1_ref[0] = x1
    hf_ref[0] = hf
    aff_ref[0] = aff


def _pool_layer(x, mod, pool_w, pool_scale, lng, lnb, router_p, n_exp):
    b, n, d = x.shape
    tm = min(256, n)
    hb = tm // POOL_HALO
    nhb = n // POOL_HALO
    g, ch, _ = pool_w.shape
    kern = functools.partial(_pool_kernel, n=n, tm=tm, n_exp=n_exp)
    return pl.pallas_call(
        kern,
        grid=(b, n // tm),
        in_specs=[
            pl.BlockSpec((1, tm, d), lambda i, t: (i, t, 0)),
            pl.BlockSpec((1, POOL_HALO, d), lambda i, t: (i, jnp.maximum(t * hb - 1, 0), 0)),
            pl.BlockSpec((1, POOL_HALO, d), lambda i, t: (i, jnp.minimum((t + 1) * hb, nhb - 1), 0)),
            pl.BlockSpec((1, 8, d), lambda i, t: (i, 0, 0)),
            pl.BlockSpec((g, ch, ch), lambda i, t: (0, 0, 0)),
            pl.BlockSpec((1, d), lambda i, t: (0, 0)),
            pl.BlockSpec((1, d), lambda i, t: (0, 0)),
            pl.BlockSpec((1, d), lambda i, t: (0, 0)),
            pl.BlockSpec((d, LANES), lambda i, t: (0, 0)),
        ],
        out_specs=_post_out_specs(tm, d, n_exp),
        out_shape=_post_outs(b, n, d, n_exp),
        scratch_shapes=[pltpu.VMEM((tm + 2 * POOL_HALO, d), F32)],
        compiler_params=_cparams(("parallel", "parallel")),
    )(x, x, x, mod, pool_w.astype(BF16), pool_scale.reshape(1, d), lng.reshape(1, d),
      lnb.reshape(1, d), router_p)


def _proj_in_kernel(x_ref, mod_ref, w_ref, o_ref, *, q_cols, tn):
    mod = mod_ref[0]
    h = (x_ref[0] * (1.0 + mod[1:2]) + mod[0:1]).astype(BF16)
    ncol = w_ref.shape[1]
    for j in range(ncol // tn):
        y = jnp.dot(h, w_ref[:, j * tn:(j + 1) * tn], preferred_element_type=F32)
        if (j + 1) * tn <= q_cols:
            y = y * (HEAD_DIM ** -0.5)
        o_ref[0, :, j * tn:(j + 1) * tn] = y.astype(o_ref.dtype)


def _proj_in(x, mod, w_bf16, q_cols):
    b, n, d = x.shape
    ncol = w_bf16.shape[1]
    tm = min(512, n)
    tn = 512
    kern = functools.partial(_proj_in_kernel, q_cols=q_cols, tn=tn)
    return pl.pallas_call(
        kern,
        grid=(b, n // tm),
        in_specs=[
            pl.BlockSpec((1, tm, d), lambda i, t: (i, t, 0)),
            pl.BlockSpec((1, 8, d), lambda i, t: (i, 0, 0)),
            pl.BlockSpec((d, ncol), lambda i, t: (0, 0)),
        ],
        out_specs=pl.BlockSpec((1, tm, ncol), lambda i, t: (i, t, 0)),
        out_shape=jax.ShapeDtypeStruct((b, n, ncol), BF16),
        compiler_params=_cparams(("parallel", "parallel")),
    )(x, mod, w_bf16)


def _gqa_proj_kernel(x_ref, mod_ref, w_ref, gain_ref, seg_ref, cos_ref, sin_ref, o_ref, *, n_norm):
    mod = mod_ref[0]
    h = (x_ref[0] * (1.0 + mod[1:2]) + mod[0:1]).astype(BF16)
    seg = seg_ref[...]
    cos = cos_ref[...]
    sin = sin_ref[...]
    ncol = w_ref.shape[1]
    for j in range(ncol // LANES):
        cols = slice(j * LANES, (j + 1) * LANES)
        y = jnp.dot(h, w_ref[:, cols], preferred_element_type=F32)
        if j < n_norm:
            sq = y * y
            sq_hi = sq.astype(BF16)
            sq_lo = (sq - sq_hi.astype(F32)).astype(BF16)
            ms = (jnp.dot(sq_hi, seg, preferred_element_type=F32)
                  + jnp.dot(sq_lo, seg, preferred_element_type=F32))
            yn = y * lax.rsqrt(ms + RMS_EPS) * gain_ref[:, cols]
            y = yn * cos + pltpu.roll(yn, LANES // 2, axis=1) * sin
        o_ref[0, :, cols] = y.astype(o_ref.dtype)


def _gqa_proj(x, mod, w_bf16, gain_row, seg, cos_t, sin_t, n_norm):
    b, n, d = x.shape
    ncol = w_bf16.shape[1]
    tm = min(512, n)
    kern = functools.partial(_gqa_proj_kernel, n_norm=n_norm)
    return pl.pallas_call(
        kern,
        grid=(b, n // tm),
        in_specs=[
            pl.BlockSpec((1, tm, d), lambda i, t: (i, t, 0)),
            pl.BlockSpec((1, 8, d), lambda i, t: (i, 0, 0)),
            pl.BlockSpec((d, ncol), lambda i, t: (0, 0)),
            pl.BlockSpec((1, ncol), lambda i, t: (0, 0)),
            pl.BlockSpec((LANES, LANES), lambda i, t: (0, 0)),
            pl.BlockSpec((tm, LANES), lambda i, t: (t, 0)),
            pl.BlockSpec((tm, LANES), lambda i, t: (t, 0)),
        ],
        out_specs=pl.BlockSpec((1, tm, ncol), lambda i, t: (i, t, 0)),
        out_shape=jax.ShapeDtypeStruct((b, n, ncol), BF16),
        compiler_params=_cparams(("parallel", "parallel")),
    )(x, mod, w_bf16, gain_row, seg, cos_t, sin_t)


def _proj_out_kernel(o_ref, x_ref, mod_ref, w_ref, lng_ref, lnb_ref, r_ref,
                     x1_ref, hf_ref, aff_ref, *, n_exp):
    y = jnp.dot(o_ref[0], w_ref[...], preferred_element_type=F32)
    x1, hf, aff = _post_mixer(x_ref[0], y, mod_ref[0], lng_ref[...], lnb_ref[...], r_ref[...], n_exp)
    x1_ref[0] = x1
    hf_ref[0] = hf
    aff_ref[0] = aff


def _proj_out(o, x, mod, w_bf16, lng, lnb, router_p, n_exp):
    b, n, d = x.shape
    tm = min(256, n)
    kern = functools.partial(_proj_out_kernel, n_exp=n_exp)
    return pl.pallas_call(
        kern,
        grid=(b, n // tm),
        in_specs=[
            pl.BlockSpec((1, tm, o.shape[2]), lambda i, t: (i, t, 0)),
            pl.BlockSpec((1, tm, d), lambda i, t: (i, t, 0)),
            pl.BlockSpec((1, 8, d), lambda i, t: (i, 0, 0)),
            pl.BlockSpec(w_bf16.shape, lambda i, t: (0, 0)),
            pl.BlockSpec((1, d), lambda i, t: (0, 0)),
            pl.BlockSpec((1, d), lambda i, t: (0, 0)),
            pl.BlockSpec((d, LANES), lambda i, t: (0, 0)),
        ],
        out_specs=_post_out_specs(tm, d, n_exp),
        out_shape=_post_outs(b, n, d, n_exp),
        compiler_params=_cparams(("parallel", "parallel")),
    )(o, x, mod, w_bf16, lng.reshape(1, d), lnb.reshape(1, d), router_p)


def _softmax_pv(scores, values, lane_lo):
    m = functools.reduce(jnp.maximum, [jnp.max(s, axis=1, keepdims=True) for s in scores])
    ps = [jnp.exp(s - m) for s in scores]
    l = functools.reduce(lambda a, c: a + c, [jnp.sum(p, axis=1, keepdims=True) for p in ps])
    o = None
    for p, v in zip(ps, values):
        c = jnp.dot(p.astype(BF16), v, preferred_element_type=F32)
        o = c if o is None else o + c
    del lane_lo
    return o / l


def _na_kernel(q_ref, ka_ref, kb_ref, kc_ref, va_ref, vb_ref, vc_ref, kx_ref, vx_ref, bias_ref,
               o_ref, *, rows):
    i = pl.program_id(1)
    nb = pl.num_programs(1)
    tq = q_ref.shape[1]
    rq0 = i * NA_QROWS
    ks = jnp.clip(i - 1, 0, nb - NA_KBLOCKS) * NA_QROWS
    qi = lax.broadcasted_iota(I32, (tq, tq), 0)
    ki = lax.broadcasted_iota(I32, (tq, tq), 1)
    q_row = rq0 + qi // GRID_W
    q_col = qi % GRID_W
    k_col = ki % GRID_W
    r0 = jnp.clip(q_row - NA_KH // 2, 0, rows - NA_KH)
    c0 = jnp.clip(q_col - NA_KW // 2, 0, GRID_W - NA_KW)
    col_ok = (k_col >= c0) & (k_col < c0 + NA_KW)
    mask_add = []
    for blk in range(NA_KBLOCKS):
        k_row = ks + blk * NA_QROWS + ki // GRID_W
        ok = col_ok & (k_row >= r0) & (k_row < r0 + NA_KH)
        mask_add.append(jnp.where(ok, 0.0, NEG_INF))
    lane = lax.broadcasted_iota(I32, (1, LANES), 1)
    k_refs = (ka_ref, kb_ref, kc_ref)
    v_refs = (va_ref, vb_ref, vc_ref)
    n_pairs = q_ref.shape[2] // LANES
    for p in range(n_pairs):
        cols = slice(p * LANES, (p + 1) * LANES)
        q2 = q_ref[0, :, cols]
        kt = [r[0, :, cols] for r in k_refs] + [kx_ref[0, :, cols]]
        vt = [r[0, :, cols] for r in v_refs] + [vx_ref[0, :, cols]]
        outs = []
        for hh in range(2):
            head = 2 * p + hh
            hmask = (lane // HEAD_DIM) == hh
            qm = jnp.where(hmask, q2, jnp.zeros_like(q2))
            scores = []
            for blk in range(NA_KBLOCKS):
                s = lax.dot_general(qm, kt[blk], (((1,), (1,)), ((), ())),
                                    preferred_element_type=F32)
                bands = []
                for qr in range(NA_QROWS):
                    halves = []
                    for kp in range(NA_QROWS // 2):
                        dr = ks + blk * NA_QROWS + 2 * kp - (rq0 + qr)
                        idx = jnp.clip(dr + NA_KH, 0, 2 * NA_KH - 1)
                        halves.append(bias_ref[head, idx])
                    bands.append(jnp.concatenate(halves, axis=1))
                bias = jnp.concatenate(bands, axis=0)
                scores.append(s + bias + mask_add[blk])
            scores.append(lax.dot_general(qm, kt[NA_KBLOCKS], (((1,), (1,)), ((), ())),
                                          preferred_element_type=F32))
            outs.append(_softmax_pv(scores, vt, None))
        o_ref[0, :, cols] = jnp.where((lane // HEAD_DIM) == 0, outs[0], outs[1]).astype(o_ref.dtype)


def _na_bias_table(rpb):
    col = np.arange(GRID_W)
    dc = np.clip(col[None, :] - col[:, None] + (NA_KW - 1), 0, 2 * NA_KW - 2)
    t = rpb[:, :, dc]
    t_first = jnp.concatenate([t[:, :1], t], axis=1)
    t_next = jnp.concatenate([t, t[:, -1:]], axis=1)
    return jnp.concatenate([t_first, t_next], axis=-1).astype(F32)


def _na_attention(qkv, qkv_c, bias_tab):
    b, n, d3 = qkv.shape
    d = d3 // 3
    l = qkv_c.shape[1]
    rows = n // GRID_W
    tq = NA_QROWS * GRID_W
    nb = n // tq
    heads = d // HEAD_DIM

    def kmap(off, col):
        return lambda i, t: (i, jnp.clip(t - 1, 0, nb - NA_KBLOCKS) + off, col)

    kern = functools.partial(_na_kernel, rows=rows)
    return pl.pallas_call(
        kern,
        grid=(b, nb),
        in_specs=[
            pl.BlockSpec((1, tq, d), lambda i, t: (i, t, 0)),
            pl.BlockSpec((1, tq, d), kmap(0, 1)),
            pl.BlockSpec((1, tq, d), kmap(1, 1)),
            pl.BlockSpec((1, tq, d), kmap(2, 1)),
            pl.BlockSpec((1, tq, d), kmap(0, 2)),
            pl.BlockSpec((1, tq, d), kmap(1, 2)),
            pl.BlockSpec((1, tq, d), kmap(2, 2)),
            pl.BlockSpec((1, l, d), lambda i, t: (i, 0, 1)),
            pl.BlockSpec((1, l, d), lambda i, t: (i, 0, 2)),
            pl.BlockSpec((heads, 2 * NA_KH, GRID_W, 2 * GRID_W), lambda i, t: (0, 0, 0, 0)),
        ],
        out_specs=pl.BlockSpec((1, tq, d), lambda i, t: (i, t, 0)),
        out_shape=jax.ShapeDtypeStruct((b, n, d), BF16),
        compiler_params=_cparams(("parallel", "parallel")),
    )(qkv, qkv, qkv, qkv, qkv, qkv, qkv, qkv_c, qkv_c, bias_tab)


def _ctx_attn_kernel(q_ref, k_ref, v_ref, o_ref):
    lane = lax.broadcasted_iota(I32, (1, LANES), 1)
    for p in range(q_ref.shape[2] // LANES):
        cols = slice(p * LANES, (p + 1) * LANES)
        q2 = q_ref[0, :, cols]
        k2 = k_ref[0, :, cols]
        v2 = v_ref[0, :, cols]
        outs = []
        for hh in range(2):
            qm = jnp.where((lane // HEAD_DIM) == hh, q2, jnp.zeros_like(q2))
            s = lax.dot_general(qm, k2, (((1,), (1,)), ((), ())), preferred_element_type=F32)
            outs.append(_softmax_pv([s], [v2], None))
        o_ref[0, :, cols] = jnp.where((lane // HEAD_DIM) == 0, outs[0], outs[1]).astype(o_ref.dtype)


def _ctx_attention(qkv_c):
    b, l, d3 = qkv_c.shape
    d = d3 // 3
    return pl.pallas_call(
        _ctx_attn_kernel,
        grid=(b,),
        in_specs=[pl.BlockSpec((1, l, d), lambda i: (i, 0, 0)),
                  pl.BlockSpec((1, l, d), lambda i: (i, 0, 1)),
                  pl.BlockSpec((1, l, d), lambda i: (i, 0, 2))],
        out_specs=pl.BlockSpec((1, l, d), lambda i: (i, 0, 0)),
        out_shape=jax.ShapeDtypeStruct((b, l, d), BF16),
        compiler_params=_cparams(("parallel",)),
    )(qkv_c, qkv_c, qkv_c)


def _gqa_kernel(q_ref, k_ref, v_ref, o_ref, qm_sc, m_sc, l_sc, acc_sc, *, tk):
    nk = k_ref.shape[1] // tk
    tq = q_ref.shape[1]
    n_sl = q_ref.shape[2] // LANES
    lane = lax.broadcasted_iota(I32, (1, LANES), 1)
    for sl in range(n_sl):
        q2 = q_ref[0, :, sl * LANES:(sl + 1) * LANES]
        for hh in range(2):
            h = 2 * sl + hh
            qm_sc[h * tq:(h + 1) * tq, :] = jnp.where(((lane // (HEAD_DIM // 2)) % 2) == hh, q2,
                                                      jnp.zeros_like(q2))
    l_sc[...] = jnp.zeros_like(l_sc)
    acc_sc[...] = jnp.zeros_like(acc_sc)

    def scores(j):
        k0 = pl.multiple_of(j * tk, tk)
        return lax.dot_general(qm_sc[...], k_ref[0, pl.ds(k0, tk), :], (((1,), (1,)), ((), ())),
                               preferred_element_type=F32)

    def lane_partial_sum(p):
        return functools.reduce(lambda u, w: u + w,
                                [p[:, i * LANES:(i + 1) * LANES] for i in range(tk // LANES)])

    def values(j):
        return v_ref[0, pl.ds(pl.multiple_of(j * tk, tk), tk), :]

    m0 = jnp.max(scores(0), axis=1, keepdims=True)
    qf = qm_sc[...].astype(F32)
    q_norm2 = jnp.sum(qf * qf, axis=1, keepdims=True)
    k_lane_max2 = jnp.zeros((8, LANES), F32)
    for i in range(k_ref.shape[1] // tk):
        kf = k_ref[0, i * tk:(i + 1) * tk, :].astype(F32)
        k_lane_max2 = jnp.maximum(k_lane_max2, jnp.max((kf * kf).reshape(tk // 8, 8, LANES), axis=0))
    k_norm2 = jnp.sum(jnp.max(k_lane_max2, axis=0, keepdims=True), axis=1, keepdims=True)
    margin = jnp.max(jnp.sqrt(q_norm2 * k_norm2) - m0)
    safe = margin <= EXP2_HEADROOM

    @pl.when(safe)
    def _():
        def body(j, c):
            p = jnp.exp2(scores(j) - m0)
            l_sc[...] += lane_partial_sum(p)
            acc_sc[...] += jnp.dot(p.astype(BF16), values(j), preferred_element_type=F32)
            return c

        lax.fori_loop(0, nk, body, 0)

    @pl.when(jnp.logical_not(safe))
    def _():
        m_sc[...] = jnp.full_like(m_sc, NEG_INF)

        def body(j, c):
            s = scores(j)
            m_old = m_sc[...]
            m_new = jnp.maximum(m_old, jnp.max(s, axis=1, keepdims=True))
            a = jnp.exp2(m_old - m_new)
            p = jnp.exp2(s - m_new)
            l_sc[...] = a * l_sc[...] + lane_partial_sum(p)
            acc_sc[...] = a * acc_sc[...] + jnp.dot(p.astype(BF16), values(j),
                                                    preferred_element_type=F32)
            m_sc[...] = m_new
            return c

        lax.fori_loop(0, nk, body, 0)

    out = acc_sc[...] / jnp.sum(l_sc[...], axis=1, keepdims=True)
    for sl in range(n_sl):
        o0 = out[2 * sl * tq:(2 * sl + 1) * tq]
        o1 = out[(2 * sl + 1) * tq:(2 * sl + 2) * tq]
        o_ref[0, :, sl * LANES:(sl + 1) * LANES] = jnp.where(
            (lane // HEAD_DIM) == 0, o0, o1).astype(o_ref.dtype)


def _gqa_attention(q, kcat, vcat):
    b, n, dq = q.shape
    nk_tot = kcat.shape[1]
    n_slab = kcat.shape[2] // LANES
    q_per = dq // n_slab
    n_heads = 2 * q_per // LANES
    tq = min(256, n)
    tk = 256
    for cand in (1024, 768, 512):
        if nk_tot % cand == 0:
            tk = cand
            break
    kern = functools.partial(_gqa_kernel, tk=tk)
    return pl.pallas_call(
        kern,
        grid=(b, n_slab, n // tq),
        in_specs=[
            pl.BlockSpec((1, tq, q_per), lambda i, p, t: (i, t, p)),
            pl.BlockSpec((1, nk_tot, LANES), lambda i, p, t: (i, 0, p)),
            pl.BlockSpec((1, nk_tot, LANES), lambda i, p, t: (i, 0, p)),
        ],
        out_specs=pl.BlockSpec((1, tq, q_per), lambda i, p, t: (i, t, p)),
        out_shape=jax.ShapeDtypeStruct((b, n, dq), BF16),
        scratch_shapes=[pltpu.VMEM((n_heads * tq, LANES), BF16),
                        pltpu.VMEM((n_heads * tq, 1), F32),
                        pltpu.VMEM((n_heads * tq, LANES), F32),
                        pltpu.VMEM((n_heads * tq, LANES), F32)],
        compiler_params=_cparams(("parallel", "parallel", "parallel")),
    )(q, kcat, vcat)


def _prefix_count(mask_f32, tri, out_cb):
    e, n = mask_f32.shape
    c = tri.shape[0]
    carry = jnp.zeros((e, 1), F32)
    for j in range(n // c):
        chunk = mask_f32[:, j * c:(j + 1) * c]
        inc = jnp.dot(chunk.astype(BF16), tri, preferred_element_type=F32) + carry
        out_cb(j, c, chunk, inc)
        carry = inc[:, c - 1:c]


def _topk_kernel(aff_ref, rank_ref, cum_ref, eqx_ref, *, cap):
    a = aff_ref[0]
    e, n = a.shape
    bits = lax.bitcast_convert_type(a, I32)

    def body(i, thr):
        cand = thr | (jnp.int32(1) << (30 - i))
        cnt = jnp.sum(jnp.where(bits >= cand, 1.0, 0.0), axis=1, keepdims=True)
        return jnp.where(cnt >= cap, cand, thr)

    thr = lax.fori_loop(0, 31, body, jnp.zeros((e, 1), I32))
    gt = bits > thr
    eq = bits == thr
    need = cap - jnp.sum(jnp.where(gt, 1.0, 0.0), axis=1, keepdims=True)
    c = min(256, n)
    tri = jnp.where(lax.broadcasted_iota(I32, (c, c), 0) <= lax.broadcasted_iota(I32, (c, c), 1),
                    1.0, 0.0).astype(BF16)

    def eq_cb(j, c, chunk, inc):
        eqx_ref[:, j * c:(j + 1) * c] = inc - chunk

    _prefix_count(jnp.where(eq, 1.0, 0.0), tri, eq_cb)
    sel = gt | (eq & (eqx_ref[...] < need))

    def sel_cb(j, c, chunk, inc):
        excl = (inc - chunk).astype(I32)
        cum_ref[0, :, j * c:(j + 1) * c] = excl
        rank_ref[0, :, j * c:(j + 1) * c] = jnp.where(chunk > 0.5, excl, -1)

    _prefix_count(jnp.where(sel, 1.0, 0.0), tri, sel_cb)


def _topk(aff, cap):
    b, e, n = aff.shape
    kern = functools.partial(_topk_kernel, cap=cap)
    return pl.pallas_call(
        kern,
        grid=(b,),
        in_specs=[pl.BlockSpec((1, e, n), lambda i: (i, 0, 0))],
        out_specs=(pl.BlockSpec((1, e, n), lambda i: (i, 0, 0)),
                   pl.BlockSpec((1, e, n), lambda i: (i, 0, 0))),
        out_shape=(jax.ShapeDtypeStruct((b, e, n), I32), jax.ShapeDtypeStruct((b, e, n), I32)),
        scratch_shapes=[pltpu.VMEM((e, n), F32)],
        compiler_params=_cparams(("parallel",)),
    )(aff)


def _slot_onehot(rank, starts, lows, k_new, rows_per, n_exp):
    t = rank.shape[1]
    j_iota = lax.broadcasted_iota(I32, (rows_per, t), 0)
    rows = []
    for e in range(n_exp):
        r = rank[e:e + 1, :]
        hit = (r - starts[e] == j_iota) & (r >= lows[e]) & (r < lows[e] + k_new)
        rows.append(jnp.where(hit, 1.0, 0.0))
    return jnp.concatenate(rows, axis=0)


def _align_down(v, a):
    return pl.multiple_of((v // a) * a, a)


def _gather_kernel(base_ref, rank_ref, hf_ref, xs_hbm, stage, carry, sem, pend, *, n_exp, n_tiles,
                   cap):
    b = pl.program_id(0)
    t = pl.program_id(1)
    off = (b * (n_tiles + 1) + t) * n_exp
    base = [base_ref[off + e] for e in range(n_exp)]
    cnt = [base_ref[off + n_exp + e] - base[e] for e in range(n_exp)]
    kmax = functools.reduce(jnp.maximum, cnt)
    rank = rank_ref[0]
    hf = hf_ref[0]

    def window_copy(e, start):
        return pltpu.make_async_copy(stage.at[e * GATHER_ROWS:(e + 1) * GATHER_ROWS, :],
                                     xs_hbm.at[e, b, pl.ds(start, GATHER_ROWS), :], sem.at[e])

    def drain():
        for e in range(n_exp):
            @pl.when(pend[e] == 1)
            def _(e=e):
                window_copy(e, 0).wait()
                pend[e] = 0

    @pl.when((b == 0) & (t == 0))
    def _():
        for e in range(n_exp):
            pend[e] = 0

    @pl.when(t == 0)
    def _():
        drain()
        carry[...] = jnp.zeros_like(carry)
        stage[...] = jnp.zeros_like(stage)
        for e in range(n_exp):
            window_copy(e, cap).start()
            pend[e] = 1

    def round_body(r, c):
        lows = [base[e] + r * ROUTE_K for e in range(n_exp)]
        starts = [_align_down(lows[e], ROW_ALIGN) for e in range(n_exp)]
        onehot = _slot_onehot(rank, starts, lows, ROUTE_K, GATHER_ROWS, n_exp).astype(BF16)
        rows = jnp.dot(onehot, hf, preferred_element_type=F32).astype(BF16)
        drain()
        stage[...] = rows
        for e in range(n_exp):
            r0 = e * GATHER_ROWS
            stage[r0:r0 + ROW_ALIGN, :] += carry[e * ROW_ALIGN:(e + 1) * ROW_ALIGN, :]
        for e in range(n_exp):
            @pl.when(cnt[e] > r * ROUTE_K)
            def _(e=e):
                window_copy(e, starts[e]).start()
                pend[e] = 1
                filled = lows[e] + jnp.minimum(cnt[e] - r * ROUTE_K, ROUTE_K)
                shift = _align_down(filled, ROW_ALIGN) - starts[e]
                src = pl.multiple_of(e * GATHER_ROWS + shift, ROW_ALIGN)
                carry[e * ROW_ALIGN:(e + 1) * ROW_ALIGN, :] = stage[pl.ds(src, ROW_ALIGN), :]
        return c

    lax.fori_loop(0, (kmax + ROUTE_K - 1) // ROUTE_K, round_body, 0)

    @pl.when((b == pl.num_programs(0) - 1) & (t == n_tiles - 1))
    def _():
        drain()


def _gather(base_flat, rank, hf, cap):
    b, n, d = hf.shape
    n_exp = rank.shape[1]
    n_tiles = n // ROUTE_T
    kern = functools.partial(_gather_kernel, n_exp=n_exp, n_tiles=n_tiles, cap=cap)
    return pl.pallas_call(
        kern,
        grid_spec=pltpu.PrefetchScalarGridSpec(
            num_scalar_prefetch=1,
            grid=(b, n_tiles),
            in_specs=[
                pl.BlockSpec((1, n_exp, ROUTE_T), lambda i, t, s: (i, 0, t)),
                pl.BlockSpec((1, ROUTE_T, d), lambda i, t, s: (i, t, 0)),
            ],
            out_specs=pl.BlockSpec(memory_space=pl.ANY),
            scratch_shapes=[pltpu.VMEM((n_exp * GATHER_ROWS, d), BF16),
                            pltpu.VMEM((n_exp * ROW_ALIGN, d), BF16),
                            pltpu.SemaphoreType.DMA((n_exp,)),
                            pltpu.SMEM((n_exp,), I32)],
        ),
        out_shape=jax.ShapeDtypeStruct((n_exp, b, cap + GATHER_ROWS, d), BF16),
        compiler_params=_cparams(("arbitrary", "arbitrary")),
    )(base_flat, rank, hf)


def _split_bf16(v):
    hi = v.astype(BF16)
    return hi, (v - hi.astype(F32)).astype(BF16)


def _ffn_kernel(x_ref, w1_ref, w3_ref, w2_ref, yh_ref, yl_ref, acc, *, n_f):
    f = pl.program_id(2)
    bb, cap, d = x_ref.shape[1:]
    x = x_ref[0].reshape(bb * cap, d)
    h1 = jnp.dot(x, w1_ref[0, 0].astype(BF16), preferred_element_type=F32)
    h3 = jnp.dot(x, w3_ref[0, 0].astype(BF16), preferred_element_type=F32)
    hid = (_silu(h1) * h3).astype(BF16)
    y = jnp.dot(hid, w2_ref[0, 0].astype(BF16), preferred_element_type=F32)

    def emit(total):
        hi, lo = _split_bf16(total)
        yh_ref[0] = hi.reshape(bb, cap, d)
        yl_ref[0] = lo.reshape(bb, cap, d)

    if n_f == 1:
        emit(y)
        return

    @pl.when(f == 0)
    def _():
        acc[...] = y

    @pl.when((f > 0) & (f < n_f - 1))
    def _():
        acc[...] += y

    @pl.when(f == n_f - 1)
    def _():
        emit(acc[...] + y)


def _ffn(xs, w1, w3, w2, layer, cap):
    n_exp, b, _, d = xs.shape
    ff = w1.shape[3]
    bb = b if b * cap <= 1024 else 1
    tf = min(512, ff)
    n_f = ff // tf
    kern = functools.partial(_ffn_kernel, n_f=n_f)
    y_spec = pl.BlockSpec((1, bb, cap, d), lambda e, i, f: (e, i, 0, 0))
    y_shape = jax.ShapeDtypeStruct((n_exp, b, cap, d), BF16)
    return pl.pallas_call(
        kern,
        grid=(n_exp, b // bb, n_f),
        in_specs=[
            pl.BlockSpec((1, bb, cap, d), lambda e, i, f: (e, i, 0, 0)),
            pl.BlockSpec((1, 1, d, tf), lambda e, i, f: (layer, e, 0, f)),
            pl.BlockSpec((1, 1, d, tf), lambda e, i, f: (layer, e, 0, f)),
            pl.BlockSpec((1, 1, tf, d), lambda e, i, f: (layer, e, f, 0)),
        ],
        out_specs=(y_spec, y_spec),
        out_shape=(y_shape, y_shape),
        scratch_shapes=[pltpu.VMEM((bb * cap, d), F32)],
        compiler_params=_cparams(("parallel", "parallel", "arbitrary")),
    )(xs, w1, w3, w2)


def _combine_kernel(base_ref, rank_ref, aff_ref, x1_ref, mod_ref, lng_ref, lnb_ref, yh_hbm, yl_hbm,
                    out_ref, ybuf_h, ybuf_l, sem, *, n_exp, n_tiles, cap, win):
    k_new = win - ROW_ALIGN
    b = pl.program_id(0)
    t = pl.program_id(1)
    g = b * n_tiles + t
    total = pl.num_programs(0) * n_tiles
    slot = g % 2
    tt = rank_ref.shape[2]

    def tile_info(step):
        off = ((step // n_tiles) * (n_tiles + 1) + step % n_tiles) * n_exp
        base = [base_ref[off + e] for e in range(n_exp)]
        cnt = [base_ref[off + n_exp + e] - base[e] for e in range(n_exp)]
        return base, cnt

    def window(base, r, e):
        low = base[e] + r * k_new
        start = pl.multiple_of(jnp.minimum(_align_down(low, ROW_ALIGN), cap - win), ROW_ALIGN)
        return low, start

    def copies(step, start, e, sl):
        bb = step // n_tiles
        rows = pl.ds(e * win, win)
        return (pltpu.make_async_copy(yh_hbm.at[e, bb, pl.ds(start, win), :],
                                      ybuf_h.at[sl, rows, :], sem.at[sl, 0, e]),
                pltpu.make_async_copy(yl_hbm.at[e, bb, pl.ds(start, win), :],
                                      ybuf_l.at[sl, rows, :], sem.at[sl, 1, e]))

    def issue(step, r, sl):
        base, cnt = tile_info(step)
        for e in range(n_exp):
            @pl.when(cnt[e] > r * k_new)
            def _(e=e):
                for cp in copies(step, window(base, r, e)[1], e, sl):
                    cp.start()

    def wait(step, r, sl):
        _, cnt = tile_info(step)
        for e in range(n_exp):
            @pl.when(cnt[e] > r * k_new)
            def _(e=e):
                for cp in copies(step, 0, e, sl):
                    cp.wait()

    @pl.when(g == 0)
    def _():
        ybuf_h[...] = jnp.zeros_like(ybuf_h)
        ybuf_l[...] = jnp.zeros_like(ybuf_l)
        issue(g, 0, slot)

    @pl.when(g + 1 < total)
    def _():
        issue(g + 1, 0, 1 - slot)

    base, cnt = tile_info(g)
    kmax = functools.reduce(jnp.maximum, cnt)
    rank = rank_ref[0]
    aff = aff_ref[0]
    aff_rows = jnp.concatenate(
        [jnp.broadcast_to(aff[e:e + 1, :], (win, tt)) for e in range(n_exp)], axis=0)

    def scatter(r, f):
        lows, starts = zip(*[window(base, r, e) for e in range(n_exp)])
        gates = _slot_onehot(rank, starts, lows, k_new, win, n_exp) * aff_rows
        g_hi, g_lo = _split_bf16(gates.T)
        wait(g, r, slot)
        yh = ybuf_h[slot]
        return (f + jnp.dot(g_hi, yh, preferred_element_type=F32)
                + jnp.dot(g_lo, yh, preferred_element_type=F32)
                + jnp.dot(g_hi, ybuf_l[slot], preferred_element_type=F32))

    def extra_round(r, f):
        issue(g, r, slot)
        return scatter(r, f)

    d = x1_ref.shape[2]
    f = scatter(0, jnp.zeros((tt, d), F32))
    f = lax.fori_loop(1, (kmax + k_new - 1) // k_new, extra_round, f)
    mod = mod_ref[0]
    out_ref[0] = _layer_norm(ALPHA * x1_ref[0] + mod[5:6] * f, lng_ref[...], lnb_ref[...])


def _combine(base_flat, rank, aff, x1, mod, lng, lnb, y_hi, y_lo, cap):
    b, n, d = x1.shape
    n_exp = rank.shape[1]
    n_tiles = n // ROUTE_T
    win = min(GATHER_ROWS, cap)
    assert cap % ROW_ALIGN == 0 and win > ROW_ALIGN
    kern = functools.partial(_combine_kernel, n_exp=n_exp, n_tiles=n_tiles, cap=cap, win=win)
    return pl.pallas_call(
        kern,
        grid_spec=pltpu.PrefetchScalarGridSpec(
            num_scalar_prefetch=1,
            grid=(b, n_tiles),
            in_specs=[
                pl.BlockSpec((1, n_exp, ROUTE_T), lambda i, t, s: (i, 0, t)),
                pl.BlockSpec((1, n_exp, ROUTE_T), lambda i, t, s: (i, 0, t)),
                pl.BlockSpec((1, ROUTE_T, d), lambda i, t, s: (i, t, 0)),
                pl.BlockSpec((1, 8, d), lambda i, t, s: (i, 0, 0)),
                pl.BlockSpec((1, d), lambda i, t, s: (0, 0)),
                pl.BlockSpec((1, d), lambda i, t, s: (0, 0)),
                pl.BlockSpec(memory_space=pl.ANY),
                pl.BlockSpec(memory_space=pl.ANY),
            ],
            out_specs=pl.BlockSpec((1, ROUTE_T, d), lambda i, t, s: (i, t, 0)),
            scratch_shapes=[pltpu.VMEM((2, n_exp * win, d), BF16),
                            pltpu.VMEM((2, n_exp * win, d), BF16),
                            pltpu.SemaphoreType.DMA((2, 2, n_exp))],
        ),
        out_shape=jax.ShapeDtypeStruct((b, n, d), F32),
        compiler_params=_cparams(("arbitrary", "arbitrary")),
    )(base_flat, rank, aff, x1, mod, lng.reshape(1, d), lnb.reshape(1, d), y_hi, y_lo)


def _moe(x1, hf, aff, mod, lng, lnb, w1, w3, w2, layer):
    b, n, d = x1.shape
    n_exp = aff.shape[1]
    cap = EC_CAPACITY * n // n_exp
    rank, cum = _topk(aff, cap)
    tile_base = jnp.swapaxes(cum[:, :, ::ROUTE_T], 1, 2)
    base = jnp.concatenate([tile_base, jnp.full((b, 1, n_exp), cap, I32)], axis=1).reshape(-1)
    xs = _gather(base, rank, hf, cap)
    y_hi, y_lo = _ffn(xs, w1, w3, w2, layer, cap)
    return _combine(base, rank, aff, x1, mod, lng, lnb, y_hi, y_lo, cap)


def _gqa_layout(n_q_heads):
    group = n_q_heads // GQA_KV_HEADS
    half = HEAD_DIM // 2
    ev = np.arange(half) * 2
    od = ev + 1

    def slab(col_a, col_b):
        return np.concatenate([col_a + ev, col_b + ev, col_a + od, col_b + od])

    q_cols, k_cols, o_rows, gain_idx = [], [], [], []
    dq = n_q_heads * HEAD_DIM
    for p in range(GQA_KV_HEADS // 2):
        for i in range(group):
            a = (2 * p) * group + i
            c = (2 * p + 1) * group + i
            q_cols.append(slab(a * HEAD_DIM, c * HEAD_DIM))
            o_rows.append(np.concatenate([a * HEAD_DIM + np.arange(HEAD_DIM),
                                          c * HEAD_DIM + np.arange(HEAD_DIM)]))
    for p in range(GQA_KV_HEADS // 2):
        k_cols.append(dq + slab(2 * p * HEAD_DIM, (2 * p + 1) * HEAD_DIM))
    lane_dim = np.concatenate([ev, ev, od, od])
    return (np.concatenate(q_cols), np.concatenate(k_cols), np.concatenate(o_rows), lane_dim)


def _rope_tables(n):
    t = jnp.arange(n, dtype=I32)
    row = (t // GRID_W).astype(F32)
    col = (t % GRID_W).astype(F32)
    axis_dims = HEAD_DIM // 2
    inv_freq = jnp.power(ROPE_THETA, -jnp.arange(0, axis_dims, 2, dtype=F32) / axis_dims)
    ang = jnp.concatenate([row[:, None] * inv_freq, col[:, None] * inv_freq], axis=-1)
    cos, sin = jnp.cos(ang), jnp.sin(ang)
    cos_t = jnp.concatenate([cos] * 4, axis=1)
    sin_t = jnp.concatenate([-sin, -sin, sin, sin], axis=1)
    return cos_t, sin_t


def kernel(x, c, ctx, c_ctx, mod_w, mod_b, ln_g, ln_b, pool_w, pool_scale, na_wqkv, na_wo, na_rpb,
           gqa_wqkv, gqa_q_norm, gqa_k_norm, gqa_wo, moe_router, moe_w1, moe_w3, moe_w2):
    bsz, n, d = x.shape
    l = ctx.shape[1]
    n_exp = moe_router.shape[2]
    depth = mod_w.shape[0]
    cc = jnp.zeros((8, d), F32).at[:bsz].set(c).at[bsz].set(c_ctx)
    mod_all = _modulation(cc, mod_w, mod_b)

    for i in range(depth):
        m = i % N_MIXERS
        j = i // N_MIXERS
        update_ctx = any(k % N_MIXERS != 0 for k in range(i + 1, depth))
        ctx_keys = m != 0
        mod6 = mod_all[i].reshape(8, N_MOD, d)
        mod_x = jnp.zeros((bsz, 8, d), F32).at[:, :N_MOD].set(mod6[:bsz])
        mod_c = jnp.zeros((bsz, 8, d), F32).at[:, :N_MOD].set(
            jnp.broadcast_to(mod6[bsz][None], (bsz, N_MOD, d)))
        router_p = _pad_router(moe_router[i])
        lng0, lnb0, lng1, lnb1 = ln_g[i, 0], ln_b[i, 0], ln_g[i, 1], ln_b[i, 1]
        post_c = None
        if m == 0:
            post_x = _pool_layer(x, mod_x, pool_w[j], pool_scale[j], lng0, lnb0, router_p, n_exp)
            if update_ctx:
                post_c = _pool_layer(ctx, mod_c, pool_w[j], pool_scale[j], lng0, lnb0, router_p,
                                     n_exp)
        elif m == 1:
            wqkv = na_wqkv[j].astype(BF16)
            wo = na_wo[j].astype(BF16)
            qkv = _proj_in(x, mod_x, wqkv, d)
            qkv_c = _proj_in(ctx, mod_c, wqkv, d)
            o = _na_attention(qkv, qkv_c, _na_bias_table(na_rpb[j]))
            post_x = _proj_out(o, x, mod_x, wo, lng0, lnb0, router_p, n_exp)
            if update_ctx:
                oc = _ctx_attention(qkv_c)
                post_c = _proj_out(oc, ctx, mod_c, wo, lng0, lnb0, router_p, n_exp)
        else:
            n_q_heads = gqa_wo.shape[1] // HEAD_DIM
            dq = n_q_heads * HEAD_DIM
            dkv = GQA_KV_HEADS * HEAD_DIM
            q_cols, k_cols, o_rows, lane_dim = _gqa_layout(n_q_heads)
            v_cols = dq + dkv + np.arange(dkv)
            w_perm = gqa_wqkv[j][:, np.concatenate([q_cols, k_cols, v_cols])].astype(BF16)
            wo = gqa_wo[j][o_rows].astype(BF16)
            gain_row = jnp.concatenate(
                [jnp.tile(gqa_q_norm[j][lane_dim] * (HEAD_DIM ** -0.5 * LOG2_E), dq // LANES),
                 jnp.tile(gqa_k_norm[j][lane_dim], dkv // LANES),
                 jnp.ones((dkv,), F32)]).reshape(1, -1)
            lane_head = (np.arange(LANES) // (HEAD_DIM // 2)) % 2
            seg = jnp.asarray((lane_head[:, None] == lane_head[None, :]) / HEAD_DIM, BF16)
            cos_t, sin_t = _rope_tables(n)
            n_norm = (dq + dkv) // LANES
            qkv = _gqa_proj(x, mod_x, w_perm, gain_row, seg, cos_t, sin_t, n_norm)
            qkv_c = _gqa_proj(ctx, mod_c, w_perm, gain_row, seg, jnp.ones((l, LANES), F32),
                              jnp.zeros((l, LANES), F32), n_norm)
            kcat = jnp.concatenate([qkv[:, :, dq:dq + dkv], qkv_c[:, :, dq:dq + dkv]], axis=1)
            vcat = jnp.concatenate([qkv[:, :, dq + dkv:], qkv_c[:, :, dq + dkv:]], axis=1)
            o = _gqa_attention(qkv[:, :, :dq], kcat, vcat)
            post_x = _proj_out(o, x, mod_x, wo, lng0, lnb0, router_p, n_exp)
            if update_ctx:
                raise NotImplementedError("context update after a GQA layer is not part of this stack")
        x = _moe(*post_x, mod_x, lng1, lnb1, moe_w1, moe_w3, moe_w2, i)
        if update_ctx:
            ctx = _moe(*post_c, mod_c, lng1, lnb1, moe_w1, moe_w3, moe_w2, i)
    return x
```

```python
import functools
import math

import jax
import jax.numpy as jnp
import numpy as np
from jax import lax
from jax.experimental import pallas as pl
from jax.experimental.pallas import tpu as pltpu

F32 = jnp.float32
BF16 = jnp.bfloat16
I32 = jnp.int32
HIGHEST = lax.Precision.HIGHEST

DEPTH = 4
N_MIXERS = 3
GRID_W = 64
HEAD_DIM = 64
POOL_WINDOWS = (2, 4, 8, 16)
POOL_HALO = 8
NA_KH = 8
NA_KW = 16
NA_QROWS = 4
NA_KBLOCKS = 3
GQA_KV_HEADS = 4
ROPE_THETA = 10000.0
EC_CAPACITY = 2
N_MOD = 6
LN_EPS = 1e-5
RMS_EPS = 1e-6
ALPHA = (2.0 * DEPTH) ** 0.25
LANES = 128
ROUTE_T = 256
ROUTE_K = 64
COMBINE_T = 128
COMBINE_WIN = 48
ROW_ALIGN = 16
GATHER_ROWS = ROUTE_K + ROW_ALIGN
VMEM_LIMIT = 56 * 1024 * 1024
NEG_INF = -1e30
LOG2_E = math.log2(math.e)
EXP2_HEADROOM = 64.0


def _cparams(sem, vmem=VMEM_LIMIT):
    return pltpu.CompilerParams(dimension_semantics=sem, vmem_limit_bytes=vmem)


def _silu(v):
    return v / (1.0 + jnp.exp(-v))


def _mod_kernel(c_ref, w_ref, b_ref, o_ref):
    s = _silu(c_ref[...])
    o_ref[0] = jnp.dot(s, w_ref[0], precision=HIGHEST, preferred_element_type=F32) + b_ref[0]


def _modulation(cc, mod_w, mod_b):
    depth, d, nd = mod_w.shape
    tn = nd // 4
    return pl.pallas_call(
        _mod_kernel,
        grid=(depth, nd // tn),
        in_specs=[
            pl.BlockSpec((8, d), lambda i, j: (0, 0)),
            pl.BlockSpec((1, d, tn), lambda i, j: (i, 0, j)),
            pl.BlockSpec((1, 1, tn), lambda i, j: (i, 0, j)),
        ],
        out_specs=pl.BlockSpec((1, 8, tn), lambda i, j: (i, 0, j)),
        out_shape=jax.ShapeDtypeStruct((depth, 8, nd), F32),
        compiler_params=_cparams(("parallel", "parallel")),
    )(cc, mod_w, mod_b.reshape(depth, 1, nd))


def _layer_norm(z, g, b):
    mu = jnp.mean(z, axis=-1, keepdims=True)
    zc = z - mu
    var = jnp.mean(zc * zc, axis=-1, keepdims=True)
    return zc * lax.rsqrt(var + LN_EPS) * g + b


def _post_mixer(x, y, mod, lng, lnb, router, n_exp):
    x1 = _layer_norm(ALPHA * x + mod[2:3] * y, lng, lnb)
    hf = x1 * (1.0 + mod[4:5]) + mod[3:4]
    logits = jnp.dot(hf, router, precision=HIGHEST, preferred_element_type=F32)
    lt = logits.T[:n_exp]
    m = jnp.max(lt, axis=0, keepdims=True)
    p = jnp.exp(lt - m)
    aff = p / jnp.sum(p, axis=0, keepdims=True)
    return x1, hf.astype(BF16), aff


def _post_outs(b, n, d, n_exp):
    return (jax.ShapeDtypeStruct((b, n, d), F32),
            jax.ShapeDtypeStruct((b, n, d), BF16),
            jax.ShapeDtypeStruct((b, n_exp, n), F32))


def _post_out_specs(tm, d, n_exp):
    return (pl.BlockSpec((1, tm, d), lambda b, t: (b, t, 0)),
            pl.BlockSpec((1, tm, d), lambda b, t: (b, t, 0)),
            pl.BlockSpec((1, n_exp, tm), lambda b, t: (b, 0, t)))


def _pad_router(router):
    d, e = router.shape
    return jnp.zeros((d, LANES), F32).at[:, :e].set(router)


def _pool_kernel(x_ref, xp_ref, xn_ref, mod_ref, pw_ref, ps_ref, lng_ref, lnb_ref, r_ref,
                 x1_ref, hf_ref, aff_ref, buf, *, n, tm, n_exp):
    t = pl.program_id(1)
    nt = pl.num_programs(1)
    mod = mod_ref[0]
    x = x_ref[0]
    sc = 1.0 + mod[1:2]
    sh = mod[0:1]
    h = x * sc + sh
    hp = xp_ref[0] * sc + sh
    hn = xn_ref[0] * sc + sh
    buf[0:POOL_HALO, :] = jnp.where(t > 0, hp, 0.0)
    buf[POOL_HALO:POOL_HALO + tm, :] = h
    buf[POOL_HALO + tm:, :] = jnp.where(t < nt - 1, hn, 0.0)
    pos = t * tm + lax.broadcasted_iota(I32, (tm, 1), 0)
    ch = x.shape[1] // len(POOL_WINDOWS)
    parts = []
    for g, w in enumerate(POOL_WINDOWS):
        cols = slice(g * ch, (g + 1) * ch)
        acc = None
        for o in range(-(w // 2), w - w // 2):
            v = buf[POOL_HALO + o:POOL_HALO + o + tm, cols]
            acc = v if acc is None else acc + v
        lo = jnp.maximum(pos - w // 2, 0)
        hi = jnp.minimum(pos + (w - w // 2 - 1), n - 1)
        cnt = (hi - lo + 1).astype(F32)
        dlt = acc / cnt - h[:, cols]
        parts.append(jnp.dot(dlt.astype(BF16), pw_ref[g], preferred_element_type=F32))
    y = jnp.concatenate(parts, axis=1) * ps_ref[...]
    x1, hf, aff = _post_mixer(x, y, mod, lng_ref[...], lnb_ref[...], r_ref[...], n_exp)
    x1_ref[0] = x1
    hf_ref[0] = hf
    aff_ref[0] = aff


def _pool_layer(x, mod, pool_w, pool_scale, lng, lnb, router_p, n_exp):
    b, n, d = x.shape
    tm = min(256, n)
    hb = tm // POOL_HALO
    nhb = n // POOL_HALO
    g, ch, _ = pool_w.shape
    kern = functools.partial(_pool_kernel, n=n, tm=tm, n_exp=n_exp)
    return pl.pallas_call(
        kern,
        grid=(b, n // tm),
        in_specs=[
            pl.BlockSpec((1, tm, d), lambda i, t: (i, t, 0)),
            pl.BlockSpec((1, POOL_HALO, d), lambda i, t: (i, jnp.maximum(t * hb - 1, 0), 0)),
            pl.BlockSpec((1, POOL_HALO, d), lambda i, t: (i, jnp.minimum((t + 1) * hb, nhb - 1), 0)),
            pl.BlockSpec((1, 8, d), lambda i, t: (i, 0, 0)),
            pl.BlockSpec((g, ch, ch), lambda i, t: (0, 0, 0)),
            pl.BlockSpec((1, d), lambda i, t: (0, 0)),
            pl.BlockSpec((1, d), lambda i, t: (0, 0)),
            pl.BlockSpec((1, d), lambda i, t: (0, 0)),
            pl.BlockSpec((d, LANES), lambda i, t: (0, 0)),
        ],
        out_specs=_post_out_specs(tm, d, n_exp),
        out_shape=_post_outs(b, n, d, n_exp),
        scratch_shapes=[pltpu.VMEM((tm + 2 * POOL_HALO, d), F32)],
        compiler_params=_cparams(("parallel", "parallel")),
    )(x, x, x, mod, pool_w.astype(BF16), pool_scale.reshape(1, d), lng.reshape(1, d),
      lnb.reshape(1, d), router_p)


def _proj_in_kernel(x_ref, mod_ref, w_ref, o_ref, *, q_cols, tn):
    mod = mod_ref[0]
    h = (x_ref[0] * (1.0 + mod[1:2]) + mod[0:1]).astype(BF16)
    ncol = w_ref.shape[1]
    for j in range(ncol // tn):
        y = jnp.dot(h, w_ref[:, j * tn:(j + 1) * tn], preferred_element_type=F32)
        if (j + 1) * tn <= q_cols:
            y = y * (HEAD_DIM ** -0.5)
        o_ref[0, :, j * tn:(j + 1) * tn] = y.astype(o_ref.dtype)


def _proj_in(x, mod, w_bf16, q_cols):
    b, n, d = x.shape
    ncol = w_bf16.shape[1]
    tm = min(512, n)
    tn = 512
    kern = functools.partial(_proj_in_kernel, q_cols=q_cols, tn=tn)
    return pl.pallas_call(
        kern,
        grid=(b, n // tm),
        in_specs=[
            pl.BlockSpec((1, tm, d), lambda i, t: (i, t, 0)),
            pl.BlockSpec((1, 8, d), lambda i, t: (i, 0, 0)),
            pl.BlockSpec((d, ncol), lambda i, t: (0, 0)),
        ],
        out_specs=pl.BlockSpec((1, tm, ncol), lambda i, t: (i, t, 0)),
        out_shape=jax.ShapeDtypeStruct((b, n, ncol), BF16),
        compiler_params=_cparams(("parallel", "parallel")),
    )(x, mod, w_bf16)


def _gqa_proj_kernel(x_ref, mod_ref, w_ref, gain_ref, seg_ref, cos_ref, sin_ref, o_ref, *, n_norm):
    mod = mod_ref[0]
    h = (x_ref[0] * (1.0 + mod[1:2]) + mod[0:1]).astype(BF16)
    seg = seg_ref[...]
    cos = cos_ref[...]
    sin = sin_ref[...]
    ncol = w_ref.shape[1]
    wide = 2 * LANES
    for j in range(ncol // wide):
        cols = slice(j * wide, (j + 1) * wide)
        y = jnp.dot(h, w_ref[:, cols], preferred_element_type=F32)
        if 2 * j < n_norm:
            sq_hi, sq_lo = _split_bf16(y * y)
            ms = (jnp.dot(sq_hi, seg, preferred_element_type=F32)
                  + jnp.dot(sq_lo, seg, preferred_element_type=F32))
            yn = y * lax.rsqrt(ms + RMS_EPS) * gain_ref[:, cols]
            halves = []
            for k in range(2):
                part = yn[:, k * LANES:(k + 1) * LANES]
                halves.append(part * cos + pltpu.roll(part, LANES // 2, axis=1) * sin)
            y = jnp.concatenate(halves, axis=1)
        o_ref[0, :, cols] = y.astype(o_ref.dtype)


def _gqa_proj(x, mod, w_bf16, gain_row, seg, cos_t, sin_t, n_norm):
    b, n, d = x.shape
    ncol = w_bf16.shape[1]
    tm = min(512, n)
    kern = functools.partial(_gqa_proj_kernel, n_norm=n_norm)
    return pl.pallas_call(
        kern,
        grid=(b, n // tm),
        in_specs=[
            pl.BlockSpec((1, tm, d), lambda i, t: (i, t, 0)),
            pl.BlockSpec((1, 8, d), lambda i, t: (i, 0, 0)),
            pl.BlockSpec((d, ncol), lambda i, t: (0, 0)),
            pl.BlockSpec((1, ncol), lambda i, t: (0, 0)),
            pl.BlockSpec((2 * LANES, 2 * LANES), lambda i, t: (0, 0)),
            pl.BlockSpec((tm, LANES), lambda i, t: (t, 0)),
            pl.BlockSpec((tm, LANES), lambda i, t: (t, 0)),
        ],
        out_specs=pl.BlockSpec((1, tm, ncol), lambda i, t: (i, t, 0)),
        out_shape=jax.ShapeDtypeStruct((b, n, ncol), BF16),
        compiler_params=_cparams(("parallel", "parallel")),
    )(x, mod, w_bf16, gain_row, seg, cos_t, sin_t)


def _proj_out_kernel(o_ref, x_ref, mod_ref, w_ref, lng_ref, lnb_ref, r_ref,
                     x1_ref, hf_ref, aff_ref, *, n_exp):
    y = jnp.dot(o_ref[0], w_ref[...], preferred_element_type=F32)
    x1, hf, aff = _post_mixer(x_ref[0], y, mod_ref[0], lng_ref[...], lnb_ref[...], r_ref[...], n_exp)
    x1_ref[0] = x1
    hf_ref[0] = hf
    aff_ref[0] = aff


def _proj_out(o, x, mod, w_bf16, lng, lnb, router_p, n_exp):
    b, n, d = x.shape
    tm = min(256, n)
    kern = functools.partial(_proj_out_kernel, n_exp=n_exp)
    return pl.pallas_call(
        kern,
        grid=(b, n // tm),
        in_specs=[
            pl.BlockSpec((1, tm, o.shape[2]), lambda i, t: (i, t, 0)),
            pl.BlockSpec((1, tm, d), lambda i, t: (i, t, 0)),
            pl.BlockSpec((1, 8, d), lambda i, t: (i, 0, 0)),
            pl.BlockSpec(w_bf16.shape, lambda i, t: (0, 0)),
            pl.BlockSpec((1, d), lambda i, t: (0, 0)),
            pl.BlockSpec((1, d), lambda i, t: (0, 0)),
            pl.BlockSpec((d, LANES), lambda i, t: (0, 0)),
        ],
        out_specs=_post_out_specs(tm, d, n_exp),
        out_shape=_post_outs(b, n, d, n_exp),
        compiler_params=_cparams(("parallel", "parallel")),
    )(o, x, mod, w_bf16, lng.reshape(1, d), lnb.reshape(1, d), router_p)


def _softmax_pv(scores, values, lane_lo):
    m = functools.reduce(jnp.maximum, [jnp.max(s, axis=1, keepdims=True) for s in scores])
    ps = [jnp.exp(s - m) for s in scores]
    l = functools.reduce(lambda a, c: a + c, [jnp.sum(p, axis=1, keepdims=True) for p in ps])
    o = None
    for p, v in zip(ps, values):
        c = jnp.dot(p.astype(BF16), v, preferred_element_type=F32)
        o = c if o is None else o + c
    del lane_lo
    return o / l


def _na_kernel(q_ref, ka_ref, kb_ref, kc_ref, va_ref, vb_ref, vc_ref, kx_ref, vx_ref, bias_ref,
               o_ref, *, rows):
    i = pl.program_id(1)
    nb = pl.num_programs(1)
    tq = q_ref.shape[1]
    rq0 = i * NA_QROWS
    ks = jnp.clip(i - 1, 0, nb - NA_KBLOCKS) * NA_QROWS
    qi = lax.broadcasted_iota(I32, (tq, tq), 0)
    ki = lax.broadcasted_iota(I32, (tq, tq), 1)
    q_row = rq0 + qi // GRID_W
    q_col = qi % GRID_W
    k_col = ki % GRID_W
    r0 = jnp.clip(q_row - NA_KH // 2, 0, rows - NA_KH)
    c0 = jnp.clip(q_col - NA_KW // 2, 0, GRID_W - NA_KW)
    col_ok = (k_col >= c0) & (k_col < c0 + NA_KW)
    mask_add = []
    for blk in range(NA_KBLOCKS):
        k_row = ks + blk * NA_QROWS + ki // GRID_W
        ok = col_ok & (k_row >= r0) & (k_row < r0 + NA_KH)
        mask_add.append(jnp.where(ok, 0.0, NEG_INF))
    lane = lax.broadcasted_iota(I32, (1, LANES), 1)
    k_refs = (ka_ref, kb_ref, kc_ref)
    v_refs = (va_ref, vb_ref, vc_ref)
    n_pairs = q_ref.shape[2] // LANES
    for p in range(n_pairs):
        cols = slice(p * LANES, (p + 1) * LANES)
        q2 = q_ref[0, :, cols]
        kt = [r[0, :, cols] for r in k_refs] + [kx_ref[0, :, cols]]
        vt = [r[0, :, cols] for r in v_refs] + [vx_ref[0, :, cols]]
        outs = []
        for hh in range(2):
            head = 2 * p + hh
            hmask = (lane // HEAD_DIM) == hh
            qm = jnp.where(hmask, q2, jnp.zeros_like(q2))
            scores = []
            for blk in range(NA_KBLOCKS):
                s = lax.dot_general(qm, kt[blk], (((1,), (1,)), ((), ())),
                                    preferred_element_type=F32)
                bands = []
                for qr in range(NA_QROWS):
                    halves = []
                    for kp in range(NA_QROWS // 2):
                        dr = ks + blk * NA_QROWS + 2 * kp - (rq0 + qr)
                        idx = jnp.clip(dr + NA_KH, 0, 2 * NA_KH - 1)
                        halves.append(bias_ref[head, idx])
                    bands.append(jnp.concatenate(halves, axis=1))
                bias = jnp.concatenate(bands, axis=0)
                scores.append(s + bias + mask_add[blk])
            scores.append(lax.dot_general(qm, kt[NA_KBLOCKS], (((1,), (1,)), ((), ())),
                                          preferred_element_type=F32))
            outs.append(_softmax_pv(scores, vt, None))
        o_ref[0, :, cols] = jnp.where((lane // HEAD_DIM) == 0, outs[0], outs[1]).astype(o_ref.dtype)


def _na_bias_table(rpb):
    col = np.arange(GRID_W)
    dc = np.clip(col[None, :] - col[:, None] + (NA_KW - 1), 0, 2 * NA_KW - 2)
    t = rpb[:, :, dc]
    t_first = jnp.concatenate([t[:, :1], t], axis=1)
    t_next = jnp.concatenate([t, t[:, -1:]], axis=1)
    return jnp.concatenate([t_first, t_next], axis=-1).astype(F32)


def _na_attention(qkv, qkv_c, bias_tab):
    b, n, d3 = qkv.shape
    d = d3 // 3
    l = qkv_c.shape[1]
    rows = n // GRID_W
    tq = NA_QROWS * GRID_W
    nb = n // tq
    heads = d // HEAD_DIM

    def kmap(off, col):
        return lambda i, t: (i, jnp.clip(t - 1, 0, nb - NA_KBLOCKS) + off, col)

    kern = functools.partial(_na_kernel, rows=rows)
    return pl.pallas_call(
        kern,
        grid=(b, nb),
        in_specs=[
            pl.BlockSpec((1, tq, d), lambda i, t: (i, t, 0)),
            pl.BlockSpec((1, tq, d), kmap(0, 1)),
            pl.BlockSpec((1, tq, d), kmap(1, 1)),
            pl.BlockSpec((1, tq, d), kmap(2, 1)),
            pl.BlockSpec((1, tq, d), kmap(0, 2)),
            pl.BlockSpec((1, tq, d), kmap(1, 2)),
            pl.BlockSpec((1, tq, d), kmap(2, 2)),
            pl.BlockSpec((1, l, d), lambda i, t: (i, 0, 1)),
            pl.BlockSpec((1, l, d), lambda i, t: (i, 0, 2)),
            pl.BlockSpec((heads, 2 * NA_KH, GRID_W, 2 * GRID_W), lambda i, t: (0, 0, 0, 0)),
        ],
        out_specs=pl.BlockSpec((1, tq, d), lambda i, t: (i, t, 0)),
        out_shape=jax.ShapeDtypeStruct((b, n, d), BF16),
        compiler_params=_cparams(("parallel", "parallel")),
    )(qkv, qkv, qkv, qkv, qkv, qkv, qkv, qkv_c, qkv_c, bias_tab)


def _ctx_attn_kernel(q_ref, k_ref, v_ref, o_ref):
    lane = lax.broadcasted_iota(I32, (1, LANES), 1)
    for p in range(q_ref.shape[2] // LANES):
        cols = slice(p * LANES, (p + 1) * LANES)
        q2 = q_ref[0, :, cols]
        k2 = k_ref[0, :, cols]
        v2 = v_ref[0, :, cols]
        outs = []
        for hh in range(2):
            qm = jnp.where((lane // HEAD_DIM) == hh, q2, jnp.zeros_like(q2))
            s = lax.dot_general(qm, k2, (((1,), (1,)), ((), ())), preferred_element_type=F32)
            outs.append(_softmax_pv([s], [v2], None))
        o_ref[0, :, cols] = jnp.where((lane // HEAD_DIM) == 0, outs[0], outs[1]).astype(o_ref.dtype)


def _ctx_attention(qkv_c):
    b, l, d3 = qkv_c.shape
    d = d3 // 3
    return pl.pallas_call(
        _ctx_attn_kernel,
        grid=(b,),
        in_specs=[pl.BlockSpec((1, l, d), lambda i: (i, 0, 0)),
                  pl.BlockSpec((1, l, d), lambda i: (i, 0, 1)),
                  pl.BlockSpec((1, l, d), lambda i: (i, 0, 2))],
        out_specs=pl.BlockSpec((1, l, d), lambda i: (i, 0, 0)),
        out_shape=jax.ShapeDtypeStruct((b, l, d), BF16),
        compiler_params=_cparams(("parallel",)),
    )(qkv_c, qkv_c, qkv_c)


def _gqa_kernel(q_ref, k_ref, v_ref, o_ref, qm_sc, m_sc, l_sc, acc_sc, *, tk):
    nk = k_ref.shape[1] // tk
    tq = q_ref.shape[1]
    n_sl = q_ref.shape[2] // LANES
    lane = lax.broadcasted_iota(I32, (1, LANES), 1)
    for sl in range(n_sl):
        q2 = q_ref[0, :, sl * LANES:(sl + 1) * LANES]
        for hh in range(2):
            h = 2 * sl + hh
            qm_sc[h * tq:(h + 1) * tq, :] = jnp.where(((lane // (HEAD_DIM // 2)) % 2) == hh, q2,
                                                      jnp.zeros_like(q2))
    l_sc[...] = jnp.zeros_like(l_sc)
    acc_sc[...] = jnp.zeros_like(acc_sc)

    def scores(j):
        k0 = pl.multiple_of(j * tk, tk)
        return lax.dot_general(qm_sc[...], k_ref[0, pl.ds(k0, tk), :], (((1,), (1,)), ((), ())),
                               preferred_element_type=F32)

    def lane_partial_sum(p):
        return functools.reduce(lambda u, w: u + w,
                                [p[:, i * LANES:(i + 1) * LANES] for i in range(tk // LANES)])

    def values(j):
        return v_ref[0, pl.ds(pl.multiple_of(j * tk, tk), tk), :]

    m0 = jnp.max(scores(0), axis=1, keepdims=True)
    qf = qm_sc[...].astype(F32)
    q_norm2 = jnp.sum(qf * qf, axis=1, keepdims=True)
    k_lane_max2 = jnp.zeros((8, LANES), F32)
    for i in range(k_ref.shape[1] // tk):
        kf = k_ref[0, i * tk:(i + 1) * tk, :].astype(F32)
        k_lane_max2 = jnp.maximum(k_lane_max2, jnp.max((kf * kf).reshape(tk // 8, 8, LANES), axis=0))
    k_norm2 = jnp.sum(jnp.max(k_lane_max2, axis=0, keepdims=True), axis=1, keepdims=True)
    margin = jnp.max(jnp.sqrt(q_norm2 * k_norm2) - m0)
    safe = margin <= EXP2_HEADROOM

    @pl.when(safe)
    def _():
        def body(j, c):
            p = jnp.exp2(scores(j) - m0)
            l_sc[...] += lane_partial_sum(p)
            acc_sc[...] += jnp.dot(p.astype(BF16), values(j), preferred_element_type=F32)
            return c

        lax.fori_loop(0, nk, body, 0)

    @pl.when(jnp.logical_not(safe))
    def _():
        m_sc[...] = jnp.full_like(m_sc, NEG_INF)

        def body(j, c):
            s = scores(j)
            m_old = m_sc[...]
            m_new = jnp.maximum(m_old, jnp.max(s, axis=1, keepdims=True))
            a = jnp.exp2(m_old - m_new)
            p = jnp.exp2(s - m_new)
            l_sc[...] = a * l_sc[...] + lane_partial_sum(p)
            acc_sc[...] = a * acc_sc[...] + jnp.dot(p.astype(BF16), values(j),
                                                    preferred_element_type=F32)
            m_sc[...] = m_new
            return c

        lax.fori_loop(0, nk, body, 0)

    out = acc_sc[...] / jnp.sum(l_sc[...], axis=1, keepdims=True)
    for sl in range(n_sl):
        o0 = out[2 * sl * tq:(2 * sl + 1) * tq]
        o1 = out[(2 * sl + 1) * tq:(2 * sl + 2) * tq]
        o_ref[0, :, sl * LANES:(sl + 1) * LANES] = jnp.where(
            (lane // HEAD_DIM) == 0, o0, o1).astype(o_ref.dtype)


def _gqa_attention(q, kcat, vcat):
    b, n, dq = q.shape
    nk_tot = kcat.shape[1]
    n_slab = kcat.shape[2] // LANES
    q_per = dq // n_slab
    n_heads = 2 * q_per // LANES
    tq = min(256, n)
    tk = 256
    for cand in (1024, 768, 512):
        if nk_tot % cand == 0:
            tk = cand
            break
    kern = functools.partial(_gqa_kernel, tk=tk)
    return pl.pallas_call(
        kern,
        grid=(b, n_slab, n // tq),
        in_specs=[
            pl.BlockSpec((1, tq, q_per), lambda i, p, t: (i, t, p)),
            pl.BlockSpec((1, nk_tot, LANES), lambda i, p, t: (i, 0, p)),
            pl.BlockSpec((1, nk_tot, LANES), lambda i, p, t: (i, 0, p)),
        ],
        out_specs=pl.BlockSpec((1, tq, q_per), lambda i, p, t: (i, t, p)),
        out_shape=jax.ShapeDtypeStruct((b, n, dq), BF16),
        scratch_shapes=[pltpu.VMEM((n_heads * tq, LANES), BF16),
                        pltpu.VMEM((n_heads * tq, 1), F32),
                        pltpu.VMEM((n_heads * tq, LANES), F32),
                        pltpu.VMEM((n_heads * tq, LANES), F32)],
        compiler_params=_cparams(("parallel", "parallel", "parallel")),
    )(q, kcat, vcat)


def _prefix_count(mask_f32, tri, out_cb):
    e, n = mask_f32.shape
    c = tri.shape[0]
    carry = jnp.zeros((e, 1), F32)
    for j in range(n // c):
        chunk = mask_f32[:, j * c:(j + 1) * c]
        inc = jnp.dot(chunk.astype(BF16), tri, preferred_element_type=F32) + carry
        out_cb(j, c, chunk, inc)
        carry = inc[:, c - 1:c]


def _topk_kernel(aff_ref, rank_ref, cum_ref, eqx_ref, *, cap):
    a = aff_ref[0]
    e, n = a.shape
    bits = lax.bitcast_convert_type(a, I32)

    def body(i, thr):
        cand = thr | (jnp.int32(1) << (30 - i))
        cnt = jnp.sum(jnp.where(bits >= cand, 1.0, 0.0), axis=1, keepdims=True)
        return jnp.where(cnt >= cap, cand, thr)

    thr = lax.fori_loop(0, 31, body, jnp.zeros((e, 1), I32))
    gt = bits > thr
    eq = bits == thr
    need = cap - jnp.sum(jnp.where(gt, 1.0, 0.0), axis=1, keepdims=True)
    c = min(256, n)
    tri = jnp.where(lax.broadcasted_iota(I32, (c, c), 0) <= lax.broadcasted_iota(I32, (c, c), 1),
                    1.0, 0.0).astype(BF16)

    def eq_cb(j, c, chunk, inc):
        eqx_ref[:, j * c:(j + 1) * c] = inc - chunk

    _prefix_count(jnp.where(eq, 1.0, 0.0), tri, eq_cb)
    sel = gt | (eq & (eqx_ref[...] < need))

    def sel_cb(j, c, chunk, inc):
        excl = (inc - chunk).astype(I32)
        cum_ref[0, :, j * c:(j + 1) * c] = excl
        rank_ref[0, :, j * c:(j + 1) * c] = jnp.where(chunk > 0.5, excl, -1)

    _prefix_count(jnp.where(sel, 1.0, 0.0), tri, sel_cb)


def _topk(aff, cap):
    b, e, n = aff.shape
    kern = functools.partial(_topk_kernel, cap=cap)
    return pl.pallas_call(
        kern,
        grid=(b,),
        in_specs=[pl.BlockSpec((1, e, n), lambda i: (i, 0, 0))],
        out_specs=(pl.BlockSpec((1, e, n), lambda i: (i, 0, 0)),
                   pl.BlockSpec((1, e, n), lambda i: (i, 0, 0))),
        out_shape=(jax.ShapeDtypeStruct((b, e, n), I32), jax.ShapeDtypeStruct((b, e, n), I32)),
        scratch_shapes=[pltpu.VMEM((e, n), F32)],
        compiler_params=_cparams(("parallel",)),
    )(aff)


def _slot_onehot(rank, starts, lows, k_new, rows_per, n_exp):
    t = rank.shape[1]
    j_iota = lax.broadcasted_iota(I32, (rows_per, t), 0)
    rows = []
    for e in range(n_exp):
        r = rank[e:e + 1, :]
        hit = (r - starts[e] == j_iota) & (r >= lows[e]) & (r < lows[e] + k_new)
        rows.append(jnp.where(hit, 1.0, 0.0))
    return jnp.concatenate(rows, axis=0)


def _align_down(v, a):
    return pl.multiple_of((v // a) * a, a)


def _gather_kernel(base_ref, rank_ref, hf_ref, xs_hbm, stage, carry, sem, pend, *, n_exp, n_tiles,
                   cap):
    b = pl.program_id(0)
    t = pl.program_id(1)
    off = (b * (n_tiles + 1) + t) * n_exp
    base = [base_ref[off + e] for e in range(n_exp)]
    cnt = [base_ref[off + n_exp + e] - base[e] for e in range(n_exp)]
    kmax = functools.reduce(jnp.maximum, cnt)
    rank = rank_ref[0]
    hf = hf_ref[0]

    def window_copy(e, start):
        return pltpu.make_async_copy(stage.at[e * GATHER_ROWS:(e + 1) * GATHER_ROWS, :],
                                     xs_hbm.at[e, b, pl.ds(start, GATHER_ROWS), :], sem.at[e])

    def drain():
        @pl.when(pend[0] == 1)
        def _():
            for e in range(n_exp):
                window_copy(e, 0).wait()
            pend[0] = 0

    @pl.when((b == 0) & (t == 0))
    def _():
        pend[0] = 0

    @pl.when(t == 0)
    def _():
        drain()
        carry[...] = jnp.zeros_like(carry)
        stage[...] = jnp.zeros_like(stage)
        for e in range(n_exp):
            window_copy(e, cap).start()
        pend[0] = 1

    def write_round(r, guarded):
        lows = [base[e] + r * ROUTE_K for e in range(n_exp)]
        starts = [_align_down(lows[e], ROW_ALIGN) for e in range(n_exp)]
        onehot = _slot_onehot(rank, starts, lows, ROUTE_K, GATHER_ROWS, n_exp).astype(BF16)
        rows = jnp.dot(onehot, hf, preferred_element_type=F32).astype(BF16)
        drain()
        stage[...] = rows
        for e in range(n_exp):
            r0 = e * GATHER_ROWS
            stage[r0:r0 + ROW_ALIGN, :] += carry[e * ROW_ALIGN:(e + 1) * ROW_ALIGN, :]

        def move(e):
            window_copy(e, starts[e]).start()
            filled = lows[e] + jnp.clip(cnt[e] - r * ROUTE_K, 0, ROUTE_K)
            shift = _align_down(filled, ROW_ALIGN) - starts[e]
            src = pl.multiple_of(e * GATHER_ROWS + shift, ROW_ALIGN)
            carry[e * ROW_ALIGN:(e + 1) * ROW_ALIGN, :] = stage[pl.ds(src, ROW_ALIGN), :]

        if not guarded:
            for e in range(n_exp):
                move(e)
            pend[0] = 1
            return
        for e in range(n_exp):
            @pl.when(cnt[e] > r * ROUTE_K)
            def _(e=e):
                move(e)
        for e in range(n_exp):
            @pl.when(cnt[e] > r * ROUTE_K)
            def _(e=e):
                window_copy(e, 0).wait()

    write_round(0, False)

    def extra_round(r, c):
        write_round(r, True)
        return c

    lax.fori_loop(1, (kmax + ROUTE_K - 1) // ROUTE_K, extra_round, 0)

    @pl.when((b == pl.num_programs(0) - 1) & (t == n_tiles - 1))
    def _():
        drain()


def _gather(base_flat, rank, hf, cap):
    b, n, d = hf.shape
    n_exp = rank.shape[1]
    n_tiles = n // ROUTE_T
    kern = functools.partial(_gather_kernel, n_exp=n_exp, n_tiles=n_tiles, cap=cap)
    return pl.pallas_call(
        kern,
        grid_spec=pltpu.PrefetchScalarGridSpec(
            num_scalar_prefetch=1,
            grid=(b, n_tiles),
            in_specs=[
                pl.BlockSpec((1, n_exp, ROUTE_T), lambda i, t, s: (i, 0, t)),
                pl.BlockSpec((1, ROUTE_T, d), lambda i, t, s: (i, t, 0)),
            ],
            out_specs=pl.BlockSpec(memory_space=pl.ANY),
            scratch_shapes=[pltpu.VMEM((n_exp * GATHER_ROWS, d), BF16),
                            pltpu.VMEM((n_exp * ROW_ALIGN, d), BF16),
                            pltpu.SemaphoreType.DMA((n_exp,)),
                            pltpu.SMEM((1,), I32)],
        ),
        out_shape=jax.ShapeDtypeStruct((n_exp, b, cap + GATHER_ROWS, d), BF16),
        compiler_params=_cparams(("arbitrary", "arbitrary")),
    )(base_flat, rank, hf)


def _split_bf16(v):
    hi = v.astype(BF16)
    return hi, (v - hi.astype(F32)).astype(BF16)


def _ffn_kernel(*refs, n_f, n_b, with_ctx):
    if with_ctx:
        x_ref, xc_ref, w1_ref, w3_ref, w2_ref, y_ref, yc_ref, acc = refs
    else:
        x_ref, w1_ref, w3_ref, w2_ref, y_ref, acc = refs
    i = pl.program_id(1)
    f = pl.program_id(2)
    w1 = w1_ref[0, 0].astype(BF16)
    w3 = w3_ref[0, 0].astype(BF16)
    w2 = w2_ref[0, 0].astype(BF16)

    def run(x, emit):
        rows = x.shape[0]
        h1 = jnp.dot(x, w1, preferred_element_type=F32)
        h3 = jnp.dot(x, w3, preferred_element_type=F32)
        hid = (_silu(h1) * h3).astype(BF16)
        y = jnp.dot(hid, w2, preferred_element_type=F32)
        if n_f == 1:
            emit(*_split_bf16(y))
            return

        @pl.when(f == 0)
        def _():
            acc[0:rows, :] = y

        @pl.when((f > 0) & (f < n_f - 1))
        def _():
            acc[0:rows, :] += y

        @pl.when(f == n_f - 1)
        def _():
            emit(*_split_bf16(acc[0:rows, :] + y))

    def emit_main(hi, lo):
        y_ref[0, 0, 0] = hi
        y_ref[0, 1, 0] = lo

    def run_main():
        run(x_ref[0, 0], emit_main)

    if not with_ctx:
        run_main()
        return

    bc, cap_c, d = xc_ref.shape[1:]

    def emit_ctx(hi, lo):
        yc_ref[0, 0] = hi.reshape(bc, cap_c, d)
        yc_ref[0, 1] = lo.reshape(bc, cap_c, d)

    pl.when(i < n_b)(run_main)

    @pl.when(i == n_b)
    def _():
        run(xc_ref[0].reshape(bc * cap_c, d), emit_ctx)


def _ffn(xs, xs_c, w1, w3, w2, layer, cap, cap_c):
    n_exp, b, _, d = xs.shape
    ff = w1.shape[3]
    tf = min(512, ff)
    n_f = ff // tf
    with_ctx = xs_c is not None
    kern = functools.partial(_ffn_kernel, n_f=n_f, n_b=b, with_ctx=with_ctx)

    def main_map(e, i, f):
        return (e, jnp.minimum(i, b - 1), 0, 0)

    x_specs = [pl.BlockSpec((1, 1, cap, d), main_map)]
    out_specs = [pl.BlockSpec((1, 2, 1, cap, d),
                              lambda e, i, f: (e, 0, jnp.minimum(i, b - 1), 0, 0))]
    out_shape = [jax.ShapeDtypeStruct((n_exp, 2, b, cap, d), BF16)]
    operands = [xs]
    if with_ctx:
        assert b * cap_c <= cap
        x_specs.append(pl.BlockSpec((1, b, cap_c, d), lambda e, i, f: (e, 0, 0, 0)))
        out_specs.append(pl.BlockSpec((1, 2, b, cap_c, d), lambda e, i, f: (e, 0, 0, 0, 0)))
        out_shape.append(jax.ShapeDtypeStruct((n_exp, 2, b, cap_c, d), BF16))
        operands.append(xs_c)
    return pl.pallas_call(
        kern,
        grid=(n_exp, b + (1 if with_ctx else 0), n_f),
        in_specs=x_specs + [
            pl.BlockSpec((1, 1, d, tf), lambda e, i, f: (layer, e, 0, f)),
            pl.BlockSpec((1, 1, d, tf), lambda e, i, f: (layer, e, 0, f)),
            pl.BlockSpec((1, 1, tf, d), lambda e, i, f: (layer, e, f, 0)),
        ],
        out_specs=tuple(out_specs),
        out_shape=tuple(out_shape),
        scratch_shapes=[pltpu.VMEM((cap, d), F32)],
        compiler_params=_cparams(("parallel", "arbitrary", "arbitrary")),
    )(*operands, w1, w3, w2)


def _combine_kernel(base_ref, rank_ref, aff_ref, x1_ref, mod_ref, lng_ref, lnb_ref, y_hbm,
                    out_ref, ybuf, sem, *, n_exp, n_tiles, cap, win):
    k_new = win - ROW_ALIGN
    b = pl.program_id(0)
    t = pl.program_id(1)
    g = b * n_tiles + t
    total = pl.num_programs(0) * n_tiles
    slot = g % 2
    tt = rank_ref.shape[2]

    def tile_info(step):
        off = ((step // n_tiles) * (n_tiles + 1) + step % n_tiles) * n_exp
        base = [base_ref[off + e] for e in range(n_exp)]
        cnt = [base_ref[off + n_exp + e] - base[e] for e in range(n_exp)]
        return base, cnt

    def window(base, r, e):
        low = base[e] + r * k_new
        start = pl.multiple_of(jnp.minimum(_align_down(low, ROW_ALIGN), cap - win), ROW_ALIGN)
        return low, start

    def copy(step, start, e, sl):
        return pltpu.make_async_copy(
            y_hbm.at[e, :, step // n_tiles, pl.ds(start, win), :],
            ybuf.at[sl, :, pl.ds(e * win, win), :], sem.at[sl, e])

    def issue(step, r, sl):
        base, cnt = tile_info(step)
        for e in range(n_exp):
            def go(e=e):
                copy(step, window(base, r, e)[1], e, sl).start()
            if isinstance(r, int) and r == 0:
                go()
            else:
                pl.when(cnt[e] > r * k_new)(go)

    def wait(step, r, sl):
        _, cnt = tile_info(step)
        for e in range(n_exp):
            def go(e=e):
                copy(step, 0, e, sl).wait()
            if isinstance(r, int) and r == 0:
                go()
            else:
                pl.when(cnt[e] > r * k_new)(go)

    @pl.when(g == 0)
    def _():
        ybuf[...] = jnp.zeros_like(ybuf)
        issue(g, 0, slot)

    @pl.when(g + 1 < total)
    def _():
        issue(g + 1, 0, 1 - slot)

    base, cnt = tile_info(g)
    kmax = functools.reduce(jnp.maximum, cnt)
    rank = rank_ref[0]
    aff = aff_ref[0]
    aff_rows = jnp.concatenate(
        [jnp.broadcast_to(aff[e:e + 1, :], (win, tt)) for e in range(n_exp)], axis=0)

    def scatter(r, f):
        lows, starts = zip(*[window(base, r, e) for e in range(n_exp)])
        gates = _slot_onehot(rank, starts, lows, k_new, win, n_exp) * aff_rows
        g_hi, g_lo = _split_bf16(gates.T)
        wait(g, r, slot)
        yh = ybuf[slot, 0]
        return (f + jnp.dot(g_hi, yh, preferred_element_type=F32)
                + jnp.dot(g_lo, yh, preferred_element_type=F32)
                + jnp.dot(g_hi, ybuf[slot, 1], preferred_element_type=F32))

    def extra_round(r, f):
        issue(g, r, slot)
        return scatter(r, f)

    d = x1_ref.shape[2]
    f = scatter(0, jnp.zeros((tt, d), F32))
    f = lax.fori_loop(1, (kmax + k_new - 1) // k_new, extra_round, f)
    mod = mod_ref[0]
    out_ref[0] = _layer_norm(ALPHA * x1_ref[0] + mod[5:6] * f, lng_ref[...], lnb_ref[...])


def _combine(cum, rank, aff, x1, mod, lng, lnb, y, cap):
    b, n, d = x1.shape
    n_exp = rank.shape[1]
    n_tiles = n // COMBINE_T
    win = min(COMBINE_WIN, cap)
    assert cap % ROW_ALIGN == 0 and win > ROW_ALIGN
    kern = functools.partial(_combine_kernel, n_exp=n_exp, n_tiles=n_tiles, cap=cap, win=win)
    return pl.pallas_call(
        kern,
        grid_spec=pltpu.PrefetchScalarGridSpec(
            num_scalar_prefetch=1,
            grid=(b, n_tiles),
            in_specs=[
                pl.BlockSpec((1, n_exp, COMBINE_T), lambda i, t, s: (i, 0, t)),
                pl.BlockSpec((1, n_exp, COMBINE_T), lambda i, t, s: (i, 0, t)),
                pl.BlockSpec((1, COMBINE_T, d), lambda i, t, s: (i, t, 0)),
                pl.BlockSpec((1, 8, d), lambda i, t, s: (i, 0, 0)),
                pl.BlockSpec((1, d), lambda i, t, s: (0, 0)),
                pl.BlockSpec((1, d), lambda i, t, s: (0, 0)),
                pl.BlockSpec(memory_space=pl.ANY),
            ],
            out_specs=pl.BlockSpec((1, COMBINE_T, d), lambda i, t, s: (i, t, 0)),
            scratch_shapes=[pltpu.VMEM((2, 2, n_exp * win, d), BF16),
                            pltpu.SemaphoreType.DMA((2, n_exp))],
        ),
        out_shape=jax.ShapeDtypeStruct((b, n, d), F32),
        compiler_params=_cparams(("arbitrary", "arbitrary")),
    )(_tile_bases(cum, COMBINE_T, cap), rank, aff, x1, mod, lng.reshape(1, d), lnb.reshape(1, d), y)


def _tile_bases(cum, tile, cap):
    b, n_exp, _ = cum.shape
    tile_base = jnp.swapaxes(cum[:, :, ::tile], 1, 2)
    return jnp.concatenate([tile_base, jnp.full((b, 1, n_exp), cap, I32)], axis=1).reshape(-1)


def _route(hf, aff):
    n = hf.shape[1]
    cap = EC_CAPACITY * n // aff.shape[1]
    rank, cum = _topk(aff, cap)
    return cum, rank, _gather(_tile_bases(cum, ROUTE_T, cap), rank, hf, cap), cap


def _moe(post_x, post_c, mod_x, mod_c, lng, lnb, w1, w3, w2, layer):
    x1, hf, aff = post_x
    cum, rank, xs, cap = _route(hf, aff)
    if post_c is None:
        (y,) = _ffn(xs, None, w1, w3, w2, layer, cap, None)
        return _combine(cum, rank, aff, x1, mod_x, lng, lnb, y, cap), None
    c1, hf_c, aff_c = post_c
    cum_c, rank_c, xs_c, cap_c = _route(hf_c, aff_c)
    y, y_c = _ffn(xs, xs_c, w1, w3, w2, layer, cap, cap_c)
    return (_combine(cum, rank, aff, x1, mod_x, lng, lnb, y, cap),
            _combine(cum_c, rank_c, aff_c, c1, mod_c, lng, lnb, y_c, cap_c))


def _gqa_layout(n_q_heads):
    group = n_q_heads // GQA_KV_HEADS
    half = HEAD_DIM // 2
    ev = np.arange(half) * 2
    od = ev + 1

    def slab(col_a, col_b):
        return np.concatenate([col_a + ev, col_b + ev, col_a + od, col_b + od])

    q_cols, k_cols, o_rows, gain_idx = [], [], [], []
    dq = n_q_heads * HEAD_DIM
    for p in range(GQA_KV_HEADS // 2):
        for i in range(group):
            a = (2 * p) * group + i
            c = (2 * p + 1) * group + i
            q_cols.append(slab(a * HEAD_DIM, c * HEAD_DIM))
            o_rows.append(np.concatenate([a * HEAD_DIM + np.arange(HEAD_DIM),
                                          c * HEAD_DIM + np.arange(HEAD_DIM)]))
    for p in range(GQA_KV_HEADS // 2):
        k_cols.append(dq + slab(2 * p * HEAD_DIM, (2 * p + 1) * HEAD_DIM))
    lane_dim = np.concatenate([ev, ev, od, od])
    return (np.concatenate(q_cols), np.concatenate(k_cols), np.concatenate(o_rows), lane_dim)


def _rope_tables(n):
    t = jnp.arange(n, dtype=I32)
    row = (t // GRID_W).astype(F32)
    col = (t % GRID_W).astype(F32)
    axis_dims = HEAD_DIM // 2
    inv_freq = jnp.power(ROPE_THETA, -jnp.arange(0, axis_dims, 2, dtype=F32) / axis_dims)
    ang = jnp.concatenate([row[:, None] * inv_freq, col[:, None] * inv_freq], axis=-1)
    cos, sin = jnp.cos(ang), jnp.sin(ang)
    cos_t = jnp.concatenate([cos] * 4, axis=1)
    sin_t = jnp.concatenate([-sin, -sin, sin, sin], axis=1)
    return cos_t, sin_t


def kernel(x, c, ctx, c_ctx, mod_w, mod_b, ln_g, ln_b, pool_w, pool_scale, na_wqkv, na_wo, na_rpb,
           gqa_wqkv, gqa_q_norm, gqa_k_norm, gqa_wo, moe_router, moe_w1, moe_w3, moe_w2):
    bsz, n, d = x.shape
    l = ctx.shape[1]
    n_exp = moe_router.shape[2]
    depth = mod_w.shape[0]
    cc = jnp.zeros((8, d), F32).at[:bsz].set(c).at[bsz].set(c_ctx)
    mod_all = _modulation(cc, mod_w, mod_b)

    for i in range(depth):
        m = i % N_MIXERS
        j = i // N_MIXERS
        update_ctx = any(k % N_MIXERS != 0 for k in range(i + 1, depth))
        ctx_keys = m != 0
        mod6 = mod_all[i].reshape(8, N_MOD, d)
        mod_x = jnp.zeros((bsz, 8, d), F32).at[:, :N_MOD].set(mod6[:bsz])
        mod_c = jnp.zeros((bsz, 8, d), F32).at[:, :N_MOD].set(
            jnp.broadcast_to(mod6[bsz][None], (bsz, N_MOD, d)))
        router_p = _pad_router(moe_router[i])
        lng0, lnb0, lng1, lnb1 = ln_g[i, 0], ln_b[i, 0], ln_g[i, 1], ln_b[i, 1]
        post_c = None
        if m == 0:
            post_x = _pool_layer(x, mod_x, pool_w[j], pool_scale[j], lng0, lnb0, router_p, n_exp)
            if update_ctx:
                post_c = _pool_layer(ctx, mod_c, pool_w[j], pool_scale[j], lng0, lnb0, router_p,
                                     n_exp)
        elif m == 1:
            wqkv = na_wqkv[j].astype(BF16)
            wo = na_wo[j].astype(BF16)
            qkv = _proj_in(x, mod_x, wqkv, d)
            qkv_c = _proj_in(ctx, mod_c, wqkv, d)
            o = _na_attention(qkv, qkv_c, _na_bias_table(na_rpb[j]))
            post_x = _proj_out(o, x, mod_x, wo, lng0, lnb0, router_p, n_exp)
            if update_ctx:
                oc = _ctx_attention(qkv_c)
                post_c = _proj_out(oc, ctx, mod_c, wo, lng0, lnb0, router_p, n_exp)
        else:
            n_q_heads = gqa_wo.shape[1] // HEAD_DIM
            dq = n_q_heads * HEAD_DIM
            dkv = GQA_KV_HEADS * HEAD_DIM
            q_cols, k_cols, o_rows, lane_dim = _gqa_layout(n_q_heads)
            v_cols = dq + dkv + np.arange(dkv)
            w_perm = gqa_wqkv[j][:, np.concatenate([q_cols, k_cols, v_cols])].astype(BF16)
            wo = gqa_wo[j][o_rows].astype(BF16)
            gain_row = jnp.concatenate(
                [jnp.tile(gqa_q_norm[j][lane_dim] * (HEAD_DIM ** -0.5 * LOG2_E), dq // LANES),
                 jnp.tile(gqa_k_norm[j][lane_dim], dkv // LANES),
                 jnp.ones((dkv,), F32)]).reshape(1, -1)
            lanes2 = np.arange(2 * LANES)
            lane_head = 2 * (lanes2 // LANES) + (lanes2 // (HEAD_DIM // 2)) % 2
            seg = jnp.asarray((lane_head[:, None] == lane_head[None, :]) / HEAD_DIM, BF16)
            cos_t, sin_t = _rope_tables(n)
            n_norm = (dq + dkv) // LANES
            qkv = _gqa_proj(x, mod_x, w_perm, gain_row, seg, cos_t, sin_t, n_norm)
            qkv_c = _gqa_proj(ctx, mod_c, w_perm, gain_row, seg, jnp.ones((l, LANES), F32),
                              jnp.zeros((l, LANES), F32), n_norm)
            kcat = jnp.concatenate([qkv[:, :, dq:dq + dkv], qkv_c[:, :, dq:dq + dkv]], axis=1)
            vcat = jnp.concatenate([qkv[:, :, dq + dkv:], qkv_c[:, :, dq + dkv:]], axis=1)
            o = _gqa_attention(qkv[:, :, :dq], kcat, vcat)
            post_x = _proj_out(o, x, mod_x, wo, lng0, lnb0, router_p, n_exp)
            if update_ctx:
                raise NotImplementedError("context update after a GQA layer is not part of this stack")
        x, ctx_new = _moe(post_x, post_c if update_ctx else None, mod_x, mod_c, lng1, lnb1,
                          moe_w1, moe_w3, moe_w2, i)
        if update_ctx:
            ctx = ctx_new
    return x
```

```python
import functools
import math

import jax
import jax.numpy as jnp
import numpy as np
from jax import lax
from jax.experimental import pallas as pl
from jax.experimental.pallas import tpu as pltpu

F32 = jnp.float32
BF16 = jnp.bfloat16
I32 = jnp.int32
HIGHEST = lax.Precision.HIGHEST

DEPTH = 4
N_MIXERS = 3
GRID_W = 64
HEAD_DIM = 64
POOL_WINDOWS = (2, 4, 8, 16)
POOL_HALO = 8
NA_KH = 8
NA_KW = 16
NA_QROWS = 4
NA_KBLOCKS = 3
GQA_KV_HEADS = 4
ROPE_THETA = 10000.0
EC_CAPACITY = 2
N_MOD = 6
LN_EPS = 1e-5
RMS_EPS = 1e-6
ALPHA = (2.0 * DEPTH) ** 0.25
LANES = 128
ROUTE_T = 256
ROUTE_K = 64
COMBINE_T = 128
COMBINE_WIN = 48
ROW_ALIGN = 16
GATHER_ROWS = ROUTE_K + ROW_ALIGN
VMEM_LIMIT = 56 * 1024 * 1024
NEG_INF = -1e30
LOG2_E = math.log2(math.e)
EXP2_HEADROOM = 64.0


def _cparams(sem, vmem=VMEM_LIMIT):
    return pltpu.CompilerParams(dimension_semantics=sem, vmem_limit_bytes=vmem)


def _silu(v):
    return v / (1.0 + jnp.exp(-v))


def _mod_kernel(c_ref, w_ref, b_ref, o_ref):
    s = _silu(c_ref[...])
    o_ref[0] = jnp.dot(s, w_ref[0], precision=HIGHEST, preferred_element_type=F32) + b_ref[0]


def _modulation(cc, mod_w, mod_b):
    depth, d, nd = mod_w.shape
    tn = nd // 4
    return pl.pallas_call(
        _mod_kernel,
        grid=(depth, nd // tn),
        in_specs=[
            pl.BlockSpec((8, d), lambda i, j: (0, 0)),
            pl.BlockSpec((1, d, tn), lambda i, j: (i, 0, j)),
            pl.BlockSpec((1, 1, tn), lambda i, j: (i, 0, j)),
        ],
        out_specs=pl.BlockSpec((1, 8, tn), lambda i, j: (i, 0, j)),
        out_shape=jax.ShapeDtypeStruct((depth, 8, nd), F32),
        compiler_params=_cparams(("parallel", "parallel")),
    )(cc, mod_w, mod_b.reshape(depth, 1, nd))


def _layer_norm(z, g, b):
    mu = jnp.mean(z, axis=-1, keepdims=True)
    zc = z - mu
    var = jnp.mean(zc * zc, axis=-1, keepdims=True)
    return zc * lax.rsqrt(var + LN_EPS) * g + b


def _post_mixer(x, y, mod, lng, lnb, router, n_exp):
    x1 = _layer_norm(ALPHA * x + mod[2:3] * y, lng, lnb)
    hf = x1 * (1.0 + mod[4:5]) + mod[3:4]
    hf_hi, hf_lo = _split_bf16(hf)
    logits = (jnp.dot(hf_hi, router[0], preferred_element_type=F32)
              + jnp.dot(hf_lo, router[0], preferred_element_type=F32)
              + jnp.dot(hf_hi, router[1], preferred_element_type=F32))
    lt = logits.T[:n_exp]
    m = jnp.max(lt, axis=0, keepdims=True)
    p = jnp.exp(lt - m)
    aff = p / jnp.sum(p, axis=0, keepdims=True)
    return x1, hf_hi, aff


def _post_outs(b, n, d, n_exp):
    return (jax.ShapeDtypeStruct((b, n, d), F32),
            jax.ShapeDtypeStruct((b, n, d), BF16),
            jax.ShapeDtypeStruct((b, n_exp, n), F32))


def _post_out_specs(tm, d, n_exp):
    return (pl.BlockSpec((1, tm, d), lambda b, t: (b, t, 0)),
            pl.BlockSpec((1, tm, d), lambda b, t: (b, t, 0)),
            pl.BlockSpec((1, n_exp, tm), lambda b, t: (b, 0, t)))


def _pad_router(router):
    d, e = router.shape
    r = jnp.zeros((d, LANES), F32).at[:, :e].set(router)
    hi = lax.bitcast_convert_type(
        lax.bitcast_convert_type(r, jnp.uint32) & jnp.uint32(0xFFFF0000), F32)
    return jnp.stack([hi.astype(BF16), (r - hi).astype(BF16)])


def _pool_kernel(x_ref, xp_ref, xn_ref, mod_ref, pw_ref, ps_ref, lng_ref, lnb_ref, r_ref,
                 x1_ref, hf_ref, aff_ref, buf, *, n, tm, n_exp):
    t = pl.program_id(1)
    nt = pl.num_programs(1)
    mod = mod_ref[0]
    x = x_ref[0]
    sc = 1.0 + mod[1:2]
    sh = mod[0:1]
    h = x * sc + sh
    hp = xp_ref[0] * sc + sh
    hn = xn_ref[0] * sc + sh
    buf[0:POOL_HALO, :] = jnp.where(t > 0, hp, 0.0)
    buf[POOL_HALO:POOL_HALO + tm, :] = h
    buf[POOL_HALO + tm:, :] = jnp.where(t < nt - 1, hn, 0.0)
    pos = t * tm + lax.broadcasted_iota(I32, (tm, 1), 0)
    ch = x.shape[1] // len(POOL_WINDOWS)
    parts = []
    for g, w in enumerate(POOL_WINDOWS):
        cols = slice(g * ch, (g + 1) * ch)
        acc = None
        for o in range(-(w // 2), w - w // 2):
            v = buf[POOL_HALO + o:POOL_HALO + o + tm, cols]
            acc = v if acc is None else acc + v
        lo = jnp.maximum(pos - w // 2, 0)
        hi = jnp.minimum(pos + (w - w // 2 - 1), n - 1)
        cnt = (hi - lo + 1).astype(F32)
        dlt = acc / cnt - h[:, cols]
        parts.append(jnp.dot(dlt.astype(BF16), pw_ref[g], preferred_element_type=F32))
    y = jnp.concatenate(parts, axis=1) * ps_ref[...]
    x1, hf, aff = _post_mixer(x, y, mod, lng_ref[...], lnb_ref[...], r_ref[...], n_exp)
    x1_ref[0] = x1
    hf_ref[0] = hf
    aff_ref[0] = aff


def _pool_layer(x, mod, pool_w, pool_scale, lng, lnb, router_p, n_exp):
    b, n, d = x.shape
    tm = min(256, n)
    hb = tm // POOL_HALO
    nhb = n // POOL_HALO
    g, ch, _ = pool_w.shape
    kern = functools.partial(_pool_kernel, n=n, tm=tm, n_exp=n_exp)
    return pl.pallas_call(
        kern,
        grid=(b, n // tm),
        in_specs=[
            pl.BlockSpec((1, tm, d), lambda i, t: (i, t, 0)),
            pl.BlockSpec((1, POOL_HALO, d), lambda i, t: (i, jnp.maximum(t * hb - 1, 0), 0)),
            pl.BlockSpec((1, POOL_HALO, d), lambda i, t: (i, jnp.minimum((t + 1) * hb, nhb - 1), 0)),
            pl.BlockSpec((1, 8, d), lambda i, t: (i, 0, 0)),
            pl.BlockSpec((g, ch, ch), lambda i, t: (0, 0, 0)),
            pl.BlockSpec((1, d), lambda i, t: (0, 0)),
            pl.BlockSpec((1, d), lambda i, t: (0, 0)),
            pl.BlockSpec((1, d), lambda i, t: (0, 0)),
            pl.BlockSpec((2, d, LANES), lambda i, t: (0, 0, 0)),
        ],
        out_specs=_post_out_specs(tm, d, n_exp),
        out_shape=_post_outs(b, n, d, n_exp),
        scratch_shapes=[pltpu.VMEM((tm + 2 * POOL_HALO, d), F32)],
        compiler_params=_cparams(("parallel", "parallel")),
    )(x, x, x, mod, pool_w.astype(BF16), pool_scale.reshape(1, d), lng.reshape(1, d),
      lnb.reshape(1, d), router_p)


def _proj_in_kernel(x_ref, mod_ref, w_ref, o_ref, *, q_cols, tn):
    mod = mod_ref[0]
    h = (x_ref[0] * (1.0 + mod[1:2]) + mod[0:1]).astype(BF16)
    ncol = w_ref.shape[1]
    for j in range(ncol // tn):
        y = jnp.dot(h, w_ref[:, j * tn:(j + 1) * tn], preferred_element_type=F32)
        if (j + 1) * tn <= q_cols:
            y = y * (HEAD_DIM ** -0.5)
        o_ref[0, :, j * tn:(j + 1) * tn] = y.astype(o_ref.dtype)


def _proj_in(x, mod, w_bf16, q_cols):
    b, n, d = x.shape
    ncol = w_bf16.shape[1]
    tm = min(512, n)
    tn = 512
    kern = functools.partial(_proj_in_kernel, q_cols=q_cols, tn=tn)
    return pl.pallas_call(
        kern,
        grid=(b, n // tm),
        in_specs=[
            pl.BlockSpec((1, tm, d), lambda i, t: (i, t, 0)),
            pl.BlockSpec((1, 8, d), lambda i, t: (i, 0, 0)),
            pl.BlockSpec((d, ncol), lambda i, t: (0, 0)),
        ],
        out_specs=pl.BlockSpec((1, tm, ncol), lambda i, t: (i, t, 0)),
        out_shape=jax.ShapeDtypeStruct((b, n, ncol), BF16),
        compiler_params=_cparams(("parallel", "parallel")),
    )(x, mod, w_bf16)


def _gqa_proj_kernel(x_ref, mod_ref, w_ref, gain_ref, seg_ref, cos_ref, sin_ref, o_ref, *, n_norm):
    mod = mod_ref[0]
    h = (x_ref[0] * (1.0 + mod[1:2]) + mod[0:1]).astype(BF16)
    seg = seg_ref[...]
    cos = cos_ref[...]
    sin = sin_ref[...]
    ncol = w_ref.shape[1]
    wide = 2 * LANES
    for j in range(ncol // wide):
        cols = slice(j * wide, (j + 1) * wide)
        y = jnp.dot(h, w_ref[:, cols], preferred_element_type=F32)
        if 2 * j < n_norm:
            sq_hi, sq_lo = _split_bf16(y * y)
            ms = (jnp.dot(sq_hi, seg, preferred_element_type=F32)
                  + jnp.dot(sq_lo, seg, preferred_element_type=F32))
            yn = y * lax.rsqrt(ms + RMS_EPS) * gain_ref[:, cols]
            halves = []
            for k in range(2):
                part = yn[:, k * LANES:(k + 1) * LANES]
                halves.append(part * cos + pltpu.roll(part, LANES // 2, axis=1) * sin)
            y = jnp.concatenate(halves, axis=1)
        o_ref[0, :, cols] = y.astype(o_ref.dtype)


def _gqa_proj(x, mod, w_bf16, gain_row, seg, cos_t, sin_t, n_norm):
    b, n, d = x.shape
    ncol = w_bf16.shape[1]
    tm = min(512, n)
    kern = functools.partial(_gqa_proj_kernel, n_norm=n_norm)
    return pl.pallas_call(
        kern,
        grid=(b, n // tm),
        in_specs=[
            pl.BlockSpec((1, tm, d), lambda i, t: (i, t, 0)),
            pl.BlockSpec((1, 8, d), lambda i, t: (i, 0, 0)),
            pl.BlockSpec((d, ncol), lambda i, t: (0, 0)),
            pl.BlockSpec((1, ncol), lambda i, t: (0, 0)),
            pl.BlockSpec((2 * LANES, 2 * LANES), lambda i, t: (0, 0)),
            pl.BlockSpec((tm, LANES), lambda i, t: (t, 0)),
            pl.BlockSpec((tm, LANES), lambda i, t: (t, 0)),
        ],
        out_specs=pl.BlockSpec((1, tm, ncol), lambda i, t: (i, t, 0)),
        out_shape=jax.ShapeDtypeStruct((b, n, ncol), BF16),
        compiler_params=_cparams(("parallel", "parallel")),
    )(x, mod, w_bf16, gain_row, seg, cos_t, sin_t)


def _proj_out_kernel(o_ref, x_ref, mod_ref, w_ref, lng_ref, lnb_ref, r_ref,
                     x1_ref, hf_ref, aff_ref, *, n_exp):
    y = jnp.dot(o_ref[0], w_ref[...], preferred_element_type=F32)
    x1, hf, aff = _post_mixer(x_ref[0], y, mod_ref[0], lng_ref[...], lnb_ref[...], r_ref[...], n_exp)
    x1_ref[0] = x1
    hf_ref[0] = hf
    aff_ref[0] = aff


def _proj_out(o, x, mod, w_bf16, lng, lnb, router_p, n_exp):
    b, n, d = x.shape
    tm = min(256, n)
    kern = functools.partial(_proj_out_kernel, n_exp=n_exp)
    return pl.pallas_call(
        kern,
        grid=(b, n // tm),
        in_specs=[
            pl.BlockSpec((1, tm, o.shape[2]), lambda i, t: (i, t, 0)),
            pl.BlockSpec((1, tm, d), lambda i, t: (i, t, 0)),
            pl.BlockSpec((1, 8, d), lambda i, t: (i, 0, 0)),
            pl.BlockSpec(w_bf16.shape, lambda i, t: (0, 0)),
            pl.BlockSpec((1, d), lambda i, t: (0, 0)),
            pl.BlockSpec((1, d), lambda i, t: (0, 0)),
            pl.BlockSpec((2, d, LANES), lambda i, t: (0, 0, 0)),
        ],
        out_specs=_post_out_specs(tm, d, n_exp),
        out_shape=_post_outs(b, n, d, n_exp),
        compiler_params=_cparams(("parallel", "parallel")),
    )(o, x, mod, w_bf16, lng.reshape(1, d), lnb.reshape(1, d), router_p)


def _softmax_pv(scores, values, lane_lo):
    m = functools.reduce(jnp.maximum, [jnp.max(s, axis=1, keepdims=True) for s in scores])
    ps = [jnp.exp(s - m) for s in scores]
    l = functools.reduce(lambda a, c: a + c, [jnp.sum(p, axis=1, keepdims=True) for p in ps])
    o = None
    for p, v in zip(ps, values):
        c = jnp.dot(p.astype(BF16), v, preferred_element_type=F32)
        o = c if o is None else o + c
    del lane_lo
    return o / l


def _na_kernel(q_ref, ka_ref, kb_ref, kc_ref, va_ref, vb_ref, vc_ref, kx_ref, vx_ref, bias_ref,
               o_ref, *, rows):
    i = pl.program_id(1)
    nb = pl.num_programs(1)
    tq = q_ref.shape[1]
    rq0 = i * NA_QROWS
    ks = jnp.clip(i - 1, 0, nb - NA_KBLOCKS) * NA_QROWS
    qi = lax.broadcasted_iota(I32, (tq, tq), 0)
    ki = lax.broadcasted_iota(I32, (tq, tq), 1)
    q_row = rq0 + qi // GRID_W
    q_col = qi % GRID_W
    k_col = ki % GRID_W
    r0 = jnp.clip(q_row - NA_KH // 2, 0, rows - NA_KH)
    c0 = jnp.clip(q_col - NA_KW // 2, 0, GRID_W - NA_KW)
    col_ok = (k_col >= c0) & (k_col < c0 + NA_KW)
    mask_add = []
    for blk in range(NA_KBLOCKS):
        k_row = ks + blk * NA_QROWS + ki // GRID_W
        ok = col_ok & (k_row >= r0) & (k_row < r0 + NA_KH)
        mask_add.append(jnp.where(ok, 0.0, NEG_INF))
    lane = lax.broadcasted_iota(I32, (1, LANES), 1)
    k_refs = (ka_ref, kb_ref, kc_ref)
    v_refs = (va_ref, vb_ref, vc_ref)
    n_pairs = q_ref.shape[2] // LANES
    for p in range(n_pairs):
        cols = slice(p * LANES, (p + 1) * LANES)
        q2 = q_ref[0, :, cols]
        kt = [r[0, :, cols] for r in k_refs] + [kx_ref[0, :, cols]]
        vt = [r[0, :, cols] for r in v_refs] + [vx_ref[0, :, cols]]
        outs = []
        for hh in range(2):
            head = 2 * p + hh
            hmask = (lane // HEAD_DIM) == hh
            qm = jnp.where(hmask, q2, jnp.zeros_like(q2))
            scores = []
            for blk in range(NA_KBLOCKS):
                s = lax.dot_general(qm, kt[blk], (((1,), (1,)), ((), ())),
                                    preferred_element_type=F32)
                bands = []
                for qr in range(NA_QROWS):
                    halves = []
                    for kp in range(NA_QROWS // 2):
                        dr = ks + blk * NA_QROWS + 2 * kp - (rq0 + qr)
                        idx = jnp.clip(dr + NA_KH, 0, 2 * NA_KH - 1)
                        halves.append(bias_ref[head, idx])
                    bands.append(jnp.concatenate(halves, axis=1))
                bias = jnp.concatenate(bands, axis=0)
                scores.append(s + bias + mask_add[blk])
            scores.append(lax.dot_general(qm, kt[NA_KBLOCKS], (((1,), (1,)), ((), ())),
                                          preferred_element_type=F32))
            outs.append(_softmax_pv(scores, vt, None))
        o_ref[0, :, cols] = jnp.where((lane // HEAD_DIM) == 0, outs[0], outs[1]).astype(o_ref.dtype)


def _na_bias_table(rpb):
    col = np.arange(GRID_W)
    dc = np.clip(col[None, :] - col[:, None] + (NA_KW - 1), 0, 2 * NA_KW - 2)
    t = rpb[:, :, dc]
    t_first = jnp.concatenate([t[:, :1], t], axis=1)
    t_next = jnp.concatenate([t, t[:, -1:]], axis=1)
    return jnp.concatenate([t_first, t_next], axis=-1).astype(F32)


def _na_attention(qkv, qkv_c, bias_tab):
    b, n, d3 = qkv.shape
    d = d3 // 3
    l = qkv_c.shape[1]
    rows = n // GRID_W
    tq = NA_QROWS * GRID_W
    nb = n // tq
    heads = d // HEAD_DIM

    def kmap(off, col):
        return lambda i, t: (i, jnp.clip(t - 1, 0, nb - NA_KBLOCKS) + off, col)

    kern = functools.partial(_na_kernel, rows=rows)
    return pl.pallas_call(
        kern,
        grid=(b, nb),
        in_specs=[
            pl.BlockSpec((1, tq, d), lambda i, t: (i, t, 0)),
            pl.BlockSpec((1, tq, d), kmap(0, 1)),
            pl.BlockSpec((1, tq, d), kmap(1, 1)),
            pl.BlockSpec((1, tq, d), kmap(2, 1)),
            pl.BlockSpec((1, tq, d), kmap(0, 2)),
            pl.BlockSpec((1, tq, d), kmap(1, 2)),
            pl.BlockSpec((1, tq, d), kmap(2, 2)),
            pl.BlockSpec((1, l, d), lambda i, t: (i, 0, 1)),
            pl.BlockSpec((1, l, d), lambda i, t: (i, 0, 2)),
            pl.BlockSpec((heads, 2 * NA_KH, GRID_W, 2 * GRID_W), lambda i, t: (0, 0, 0, 0)),
        ],
        out_specs=pl.BlockSpec((1, tq, d), lambda i, t: (i, t, 0)),
        out_shape=jax.ShapeDtypeStruct((b, n, d), BF16),
        compiler_params=_cparams(("parallel", "parallel")),
    )(qkv, qkv, qkv, qkv, qkv, qkv, qkv, qkv_c, qkv_c, bias_tab)


def _ctx_attn_kernel(q_ref, k_ref, v_ref, o_ref):
    lane = lax.broadcasted_iota(I32, (1, LANES), 1)
    for p in range(q_ref.shape[2] // LANES):
        cols = slice(p * LANES, (p + 1) * LANES)
        q2 = q_ref[0, :, cols]
        k2 = k_ref[0, :, cols]
        v2 = v_ref[0, :, cols]
        outs = []
        for hh in range(2):
            qm = jnp.where((lane // HEAD_DIM) == hh, q2, jnp.zeros_like(q2))
            s = lax.dot_general(qm, k2, (((1,), (1,)), ((), ())), preferred_element_type=F32)
            outs.append(_softmax_pv([s], [v2], None))
        o_ref[0, :, cols] = jnp.where((lane // HEAD_DIM) == 0, outs[0], outs[1]).astype(o_ref.dtype)


def _ctx_attention(qkv_c):
    b, l, d3 = qkv_c.shape
    d = d3 // 3
    return pl.pallas_call(
        _ctx_attn_kernel,
        grid=(b,),
        in_specs=[pl.BlockSpec((1, l, d), lambda i: (i, 0, 0)),
                  pl.BlockSpec((1, l, d), lambda i: (i, 0, 1)),
                  pl.BlockSpec((1, l, d), lambda i: (i, 0, 2))],
        out_specs=pl.BlockSpec((1, l, d), lambda i: (i, 0, 0)),
        out_shape=jax.ShapeDtypeStruct((b, l, d), BF16),
        compiler_params=_cparams(("parallel",)),
    )(qkv_c, qkv_c, qkv_c)


def _gqa_kernel(q_ref, k_ref, v_ref, bound_ref, o_ref, qm_sc, m_sc, l_sc, acc_sc, *, tk):
    nk = k_ref.shape[1] // tk
    tq = q_ref.shape[1]
    n_sl = q_ref.shape[2] // LANES
    lane = lax.broadcasted_iota(I32, (1, LANES), 1)
    for sl in range(n_sl):
        q2 = q_ref[0, :, sl * LANES:(sl + 1) * LANES]
        for hh in range(2):
            h = 2 * sl + hh
            qm_sc[h * tq:(h + 1) * tq, :] = jnp.where(((lane // (HEAD_DIM // 2)) % 2) == hh, q2,
                                                      jnp.zeros_like(q2))
    l_sc[...] = jnp.zeros_like(l_sc)
    acc_sc[...] = jnp.zeros_like(acc_sc)

    def scores(j):
        k0 = pl.multiple_of(j * tk, tk)
        return lax.dot_general(qm_sc[...], k_ref[0, pl.ds(k0, tk), :], (((1,), (1,)), ((), ())),
                               preferred_element_type=F32)

    def lane_partial_sum(p):
        return functools.reduce(lambda u, w: u + w,
                                [p[:, i * LANES:(i + 1) * LANES] for i in range(tk // LANES)])

    def values(j):
        return v_ref[0, pl.ds(pl.multiple_of(j * tk, tk), tk), :]

    m0 = jnp.max(scores(0), axis=1, keepdims=True)
    safe = jnp.max(bound_ref[0:1, 0:1] - m0) <= EXP2_HEADROOM

    @pl.when(safe)
    def _():
        def body(j, c):
            p = jnp.exp2(scores(j) - m0)
            l_sc[...] += lane_partial_sum(p)
            acc_sc[...] += jnp.dot(p.astype(BF16), values(j), preferred_element_type=F32)
            return c

        lax.fori_loop(0, nk, body, 0)

    @pl.when(jnp.logical_not(safe))
    def _():
        m_sc[...] = jnp.full_like(m_sc, NEG_INF)

        def body(j, c):
            s = scores(j)
            m_old = m_sc[...]
            m_new = jnp.maximum(m_old, jnp.max(s, axis=1, keepdims=True))
            a = jnp.exp2(m_old - m_new)
            p = jnp.exp2(s - m_new)
            l_sc[...] = a * l_sc[...] + lane_partial_sum(p)
            acc_sc[...] = a * acc_sc[...] + jnp.dot(p.astype(BF16), values(j),
                                                    preferred_element_type=F32)
            m_sc[...] = m_new
            return c

        lax.fori_loop(0, nk, body, 0)

    out = acc_sc[...] / jnp.sum(l_sc[...], axis=1, keepdims=True)
    for sl in range(n_sl):
        o0 = out[2 * sl * tq:(2 * sl + 1) * tq]
        o1 = out[(2 * sl + 1) * tq:(2 * sl + 2) * tq]
        o_ref[0, :, sl * LANES:(sl + 1) * LANES] = jnp.where(
            (lane // HEAD_DIM) == 0, o0, o1).astype(o_ref.dtype)


def _gqa_attention(q, kcat, vcat, score_bound):
    b, n, dq = q.shape
    nk_tot = kcat.shape[1]
    n_slab = kcat.shape[2] // LANES
    q_per = dq // n_slab
    n_heads = 2 * q_per // LANES
    tq = min(256, n)
    tk = 256
    for cand in (1024, 768, 512):
        if nk_tot % cand == 0:
            tk = cand
            break
    kern = functools.partial(_gqa_kernel, tk=tk)
    return pl.pallas_call(
        kern,
        grid=(b, n_slab, n // tq),
        in_specs=[
            pl.BlockSpec((1, tq, q_per), lambda i, p, t: (i, t, p)),
            pl.BlockSpec((1, nk_tot, LANES), lambda i, p, t: (i, 0, p)),
            pl.BlockSpec((1, nk_tot, LANES), lambda i, p, t: (i, 0, p)),
            pl.BlockSpec((1, LANES), lambda i, p, t: (0, 0)),
        ],
        out_specs=pl.BlockSpec((1, tq, q_per), lambda i, p, t: (i, t, p)),
        out_shape=jax.ShapeDtypeStruct((b, n, dq), BF16),
        scratch_shapes=[pltpu.VMEM((n_heads * tq, LANES), BF16),
                        pltpu.VMEM((n_heads * tq, 1), F32),
                        pltpu.VMEM((n_heads * tq, LANES), F32),
                        pltpu.VMEM((n_heads * tq, LANES), F32)],
        compiler_params=_cparams(("parallel", "parallel", "parallel")),
    )(q, kcat, vcat, score_bound)


def _prefix_count(mask_f32, tri, out_cb):
    e, n = mask_f32.shape
    c = tri.shape[0]
    carry = jnp.zeros((e, 1), F32)
    for j in range(n // c):
        chunk = mask_f32[:, j * c:(j + 1) * c]
        inc = jnp.dot(chunk.astype(BF16), tri, preferred_element_type=F32) + carry
        out_cb(j, c, chunk, inc)
        carry = inc[:, c - 1:c]


def _topk_kernel(aff_ref, rank_ref, cum_ref, eqx_ref, *, cap):
    a = aff_ref[0]
    e, n = a.shape
    bits = lax.bitcast_convert_type(a, I32)

    def body(i, thr):
        cand = thr | (jnp.int32(1) << (30 - i))
        cnt = jnp.sum(jnp.where(bits >= cand, 1.0, 0.0), axis=1, keepdims=True)
        return jnp.where(cnt >= cap, cand, thr)

    thr = lax.fori_loop(0, 31, body, jnp.zeros((e, 1), I32))
    gt = bits > thr
    eq = bits == thr
    need = cap - jnp.sum(jnp.where(gt, 1.0, 0.0), axis=1, keepdims=True)
    c = min(256, n)
    tri = jnp.where(lax.broadcasted_iota(I32, (c, c), 0) <= lax.broadcasted_iota(I32, (c, c), 1),
                    1.0, 0.0).astype(BF16)

    def eq_cb(j, c, chunk, inc):
        eqx_ref[:, j * c:(j + 1) * c] = inc - chunk

    _prefix_count(jnp.where(eq, 1.0, 0.0), tri, eq_cb)
    sel = gt | (eq & (eqx_ref[...] < need))

    def sel_cb(j, c, chunk, inc):
        excl = (inc - chunk).astype(I32)
        cum_ref[0, :, j * c:(j + 1) * c] = excl
        rank_ref[0, :, j * c:(j + 1) * c] = jnp.where(chunk > 0.5, excl, -1)

    _prefix_count(jnp.where(sel, 1.0, 0.0), tri, sel_cb)


def _topk(aff, cap):
    b, e, n = aff.shape
    kern = functools.partial(_topk_kernel, cap=cap)
    return pl.pallas_call(
        kern,
        grid=(b,),
        in_specs=[pl.BlockSpec((1, e, n), lambda i: (i, 0, 0))],
        out_specs=(pl.BlockSpec((1, e, n), lambda i: (i, 0, 0)),
                   pl.BlockSpec((1, e, n), lambda i: (i, 0, 0))),
        out_shape=(jax.ShapeDtypeStruct((b, e, n), I32), jax.ShapeDtypeStruct((b, e, n), I32)),
        scratch_shapes=[pltpu.VMEM((e, n), F32)],
        compiler_params=_cparams(("parallel",)),
    )(aff)


def _slot_onehot(rank, starts, lows, k_new, rows_per, n_exp):
    t = rank.shape[1]
    j_iota = lax.broadcasted_iota(I32, (rows_per, t), 0)
    rows = []
    for e in range(n_exp):
        r = rank[e:e + 1, :]
        hit = (r - starts[e] == j_iota) & (r >= lows[e]) & (r < lows[e] + k_new)
        rows.append(jnp.where(hit, 1.0, 0.0))
    return jnp.concatenate(rows, axis=0)


def _align_down(v, a):
    return pl.multiple_of((v // a) * a, a)


def _gather_kernel(base_ref, rank_ref, hf_ref, xs_hbm, stage, carry, sem, pend, *, n_exp, n_tiles,
                   cap):
    b = pl.program_id(0)
    t = pl.program_id(1)
    off = (b * (n_tiles + 1) + t) * n_exp
    base = [base_ref[off + e] for e in range(n_exp)]
    cnt = [base_ref[off + n_exp + e] - base[e] for e in range(n_exp)]
    kmax = functools.reduce(jnp.maximum, cnt)
    rank = rank_ref[0]
    hf = hf_ref[0]

    def window_copy(e, start):
        return pltpu.make_async_copy(stage.at[e * GATHER_ROWS:(e + 1) * GATHER_ROWS, :],
                                     xs_hbm.at[e, b, pl.ds(start, GATHER_ROWS), :], sem.at[e])

    def drain():
        @pl.when(pend[0] == 1)
        def _():
            for e in range(n_exp):
                window_copy(e, 0).wait()
            pend[0] = 0

    @pl.when((b == 0) & (t == 0))
    def _():
        pend[0] = 0

    @pl.when(t == 0)
    def _():
        drain()
        carry[...] = jnp.zeros_like(carry)
        stage[...] = jnp.zeros_like(stage)
        for e in range(n_exp):
            window_copy(e, cap).start()
        pend[0] = 1

    def write_round(r, guarded):
        lows = [base[e] + r * ROUTE_K for e in range(n_exp)]
        starts = [_align_down(lows[e], ROW_ALIGN) for e in range(n_exp)]
        onehot = _slot_onehot(rank, starts, lows, ROUTE_K, GATHER_ROWS, n_exp).astype(BF16)
        rows = jnp.dot(onehot, hf, preferred_element_type=F32).astype(BF16)
        drain()
        stage[...] = rows
        for e in range(n_exp):
            r0 = e * GATHER_ROWS
            stage[r0:r0 + ROW_ALIGN, :] += carry[e * ROW_ALIGN:(e + 1) * ROW_ALIGN, :]

        def move(e):
            window_copy(e, starts[e]).start()
            filled = lows[e] + jnp.clip(cnt[e] - r * ROUTE_K, 0, ROUTE_K)
            shift = _align_down(filled, ROW_ALIGN) - starts[e]
            src = pl.multiple_of(e * GATHER_ROWS + shift, ROW_ALIGN)
            carry[e * ROW_ALIGN:(e + 1) * ROW_ALIGN, :] = stage[pl.ds(src, ROW_ALIGN), :]

        if not guarded:
            for e in range(n_exp):
                move(e)
            pend[0] = 1
            return
        for e in range(n_exp):
            @pl.when(cnt[e] > r * ROUTE_K)
            def _(e=e):
                move(e)
        for e in range(n_exp):
            @pl.when(cnt[e] > r * ROUTE_K)
            def _(e=e):
                window_copy(e, 0).wait()

    write_round(0, False)

    def extra_round(r, c):
        write_round(r, True)
        return c

    lax.fori_loop(1, (kmax + ROUTE_K - 1) // ROUTE_K, extra_round, 0)

    @pl.when((b == pl.num_programs(0) - 1) & (t == n_tiles - 1))
    def _():
        drain()


def _gather(base_flat, rank, hf, cap):
    b, n, d = hf.shape
    n_exp = rank.shape[1]
    n_tiles = n // ROUTE_T
    kern = functools.partial(_gather_kernel, n_exp=n_exp, n_tiles=n_tiles, cap=cap)
    return pl.pallas_call(
        kern,
        grid_spec=pltpu.PrefetchScalarGridSpec(
            num_scalar_prefetch=1,
            grid=(b, n_tiles),
            in_specs=[
                pl.BlockSpec((1, n_exp, ROUTE_T), lambda i, t, s: (i, 0, t)),
                pl.BlockSpec((1, ROUTE_T, d), lambda i, t, s: (i, t, 0)),
            ],
            out_specs=pl.BlockSpec(memory_space=pl.ANY),
            scratch_shapes=[pltpu.VMEM((n_exp * GATHER_ROWS, d), BF16),
                            pltpu.VMEM((n_exp * ROW_ALIGN, d), BF16),
                            pltpu.SemaphoreType.DMA((n_exp,)),
                            pltpu.SMEM((1,), I32)],
        ),
        out_shape=jax.ShapeDtypeStruct((n_exp, b, cap + GATHER_ROWS, d), BF16),
        compiler_params=_cparams(("arbitrary", "arbitrary")),
    )(base_flat, rank, hf)


def _split_bf16(v):
    hi = v.astype(BF16)
    return hi, (v - hi.astype(F32)).astype(BF16)


def _ffn_kernel(*refs, n_f, n_b, with_ctx):
    if with_ctx:
        x_ref, xc_ref, w1_ref, w3_ref, w2_ref, y_ref, yc_ref, acc = refs
    else:
        x_ref, w1_ref, w3_ref, w2_ref, y_ref, acc = refs
    i = pl.program_id(1)
    f = pl.program_id(2)
    w1 = w1_ref[0, 0].astype(BF16)
    w3 = w3_ref[0, 0].astype(BF16)
    w2 = w2_ref[0, 0].astype(BF16)

    def run(x, emit):
        rows = x.shape[0]
        h1 = jnp.dot(x, w1, preferred_element_type=F32)
        h3 = jnp.dot(x, w3, preferred_element_type=F32)
        hid = (_silu(h1) * h3).astype(BF16)
        y = jnp.dot(hid, w2, preferred_element_type=F32)
        if n_f == 1:
            emit(y.astype(BF16))
            return

        @pl.when(f == 0)
        def _():
            acc[0:rows, :] = y

        @pl.when((f > 0) & (f < n_f - 1))
        def _():
            acc[0:rows, :] += y

        @pl.when(f == n_f - 1)
        def _():
            emit((acc[0:rows, :] + y).astype(BF16))

    def emit_main(out):
        y_ref[0, 0] = out

    def run_main():
        run(x_ref[0, 0], emit_main)

    if not with_ctx:
        run_main()
        return

    bc, cap_c, d = xc_ref.shape[1:]

    def emit_ctx(out):
        yc_ref[0] = out.reshape(bc, cap_c, d)

    pl.when(i < n_b)(run_main)

    @pl.when(i == n_b)
    def _():
        run(xc_ref[0].reshape(bc * cap_c, d), emit_ctx)


def _ffn(xs, xs_c, w1, w3, w2, layer, cap, cap_c):
    n_exp, b, _, d = xs.shape
    ff = w1.shape[3]
    tf = min(512, ff)
    n_f = ff // tf
    with_ctx = xs_c is not None
    kern = functools.partial(_ffn_kernel, n_f=n_f, n_b=b, with_ctx=with_ctx)

    def main_map(e, i, f):
        return (e, jnp.minimum(i, b - 1), 0, 0)

    x_specs = [pl.BlockSpec((1, 1, cap, d), main_map)]
    out_specs = [pl.BlockSpec((1, 1, cap, d), main_map)]
    out_shape = [jax.ShapeDtypeStruct((n_exp, b, cap, d), BF16)]
    operands = [xs]
    if with_ctx:
        assert b * cap_c <= cap
        c_spec = pl.BlockSpec((1, b, cap_c, d), lambda e, i, f: (e, 0, 0, 0))
        x_specs.append(c_spec)
        out_specs.append(c_spec)
        out_shape.append(jax.ShapeDtypeStruct((n_exp, b, cap_c, d), BF16))
        operands.append(xs_c)
    return pl.pallas_call(
        kern,
        grid=(n_exp, b + (1 if with_ctx else 0), n_f),
        in_specs=x_specs + [
            pl.BlockSpec((1, 1, d, tf), lambda e, i, f: (layer, e, 0, f)),
            pl.BlockSpec((1, 1, d, tf), lambda e, i, f: (layer, e, 0, f)),
            pl.BlockSpec((1, 1, tf, d), lambda e, i, f: (layer, e, f, 0)),
        ],
        out_specs=tuple(out_specs),
        out_shape=tuple(out_shape),
        scratch_shapes=[pltpu.VMEM((cap, d), F32)],
        compiler_params=_cparams(("parallel", "arbitrary", "arbitrary")),
    )(*operands, w1, w3, w2)


def _combine_kernel(base_ref, rank_ref, aff_ref, x1_ref, mod_ref, lng_ref, lnb_ref, y_hbm,
                    out_ref, ybuf, sem, *, n_exp, n_tiles, cap, win):
    k_new = win - ROW_ALIGN
    b = pl.program_id(0)
    t = pl.program_id(1)
    g = b * n_tiles + t
    total = pl.num_programs(0) * n_tiles
    slot = g % 2
    tt = rank_ref.shape[2]

    def tile_info(step):
        off = ((step // n_tiles) * (n_tiles + 1) + step % n_tiles) * n_exp
        base = [base_ref[off + e] for e in range(n_exp)]
        cnt = [base_ref[off + n_exp + e] - base[e] for e in range(n_exp)]
        return base, cnt

    def window(base, r, e):
        low = base[e] + r * k_new
        start = pl.multiple_of(jnp.minimum(_align_down(low, ROW_ALIGN), cap - win), ROW_ALIGN)
        return low, start

    def copy(step, start, e, sl):
        return pltpu.make_async_copy(
            y_hbm.at[e, step // n_tiles, pl.ds(start, win), :],
            ybuf.at[sl, pl.ds(e * win, win), :], sem.at[sl, e])

    def issue(step, r, sl):
        base, cnt = tile_info(step)
        for e in range(n_exp):
            def go(e=e):
                copy(step, window(base, r, e)[1], e, sl).start()
            if isinstance(r, int) and r == 0:
                go()
            else:
                pl.when(cnt[e] > r * k_new)(go)

    def wait(step, r, sl):
        _, cnt = tile_info(step)
        for e in range(n_exp):
            def go(e=e):
                copy(step, 0, e, sl).wait()
            if isinstance(r, int) and r == 0:
                go()
            else:
                pl.when(cnt[e] > r * k_new)(go)

    @pl.when(g == 0)
    def _():
        ybuf[...] = jnp.zeros_like(ybuf)
        issue(g, 0, slot)

    @pl.when(g + 1 < total)
    def _():
        issue(g + 1, 0, 1 - slot)

    base, cnt = tile_info(g)
    kmax = functools.reduce(jnp.maximum, cnt)
    rank = rank_ref[0]
    aff = aff_ref[0]
    aff_rows = jnp.concatenate(
        [jnp.broadcast_to(aff[e:e + 1, :], (win, tt)) for e in range(n_exp)], axis=0)

    def scatter(r, f):
        lows, starts = zip(*[window(base, r, e) for e in range(n_exp)])
        gates = _slot_onehot(rank, starts, lows, k_new, win, n_exp) * aff_rows
        wait(g, r, slot)
        return f + jnp.dot(gates.T.astype(BF16), ybuf[slot], preferred_element_type=F32)

    def extra_round(r, f):
        issue(g, r, slot)
        return scatter(r, f)

    d = x1_ref.shape[2]
    f = scatter(0, jnp.zeros((tt, d), F32))
    f = lax.fori_loop(1, (kmax + k_new - 1) // k_new, extra_round, f)
    mod = mod_ref[0]
    out_ref[0] = _layer_norm(ALPHA * x1_ref[0] + mod[5:6] * f, lng_ref[...], lnb_ref[...])


def _combine(cum, rank, aff, x1, mod, lng, lnb, y, cap):
    b, n, d = x1.shape
    n_exp = rank.shape[1]
    n_tiles = n // COMBINE_T
    win = min(COMBINE_WIN, cap)
    assert cap % ROW_ALIGN == 0 and win > ROW_ALIGN
    kern = functools.partial(_combine_kernel, n_exp=n_exp, n_tiles=n_tiles, cap=cap, win=win)
    return pl.pallas_call(
        kern,
        grid_spec=pltpu.PrefetchScalarGridSpec(
            num_scalar_prefetch=1,
            grid=(b, n_tiles),
            in_specs=[
                pl.BlockSpec((1, n_exp, COMBINE_T), lambda i, t, s: (i, 0, t)),
                pl.BlockSpec((1, n_exp, COMBINE_T), lambda i, t, s: (i, 0, t)),
                pl.BlockSpec((1, COMBINE_T, d), lambda i, t, s: (i, t, 0)),
                pl.BlockSpec((1, 8, d), lambda i, t, s: (i, 0, 0)),
                pl.BlockSpec((1, d), lambda i, t, s: (0, 0)),
                pl.BlockSpec((1, d), lambda i, t, s: (0, 0)),
                pl.BlockSpec(memory_space=pl.ANY),
            ],
            out_specs=pl.BlockSpec((1, COMBINE_T, d), lambda i, t, s: (i, t, 0)),
            scratch_shapes=[pltpu.VMEM((2, n_exp * win, d), BF16),
                            pltpu.SemaphoreType.DMA((2, n_exp))],
        ),
        out_shape=jax.ShapeDtypeStruct((b, n, d), F32),
        compiler_params=_cparams(("arbitrary", "arbitrary")),
    )(_tile_bases(cum, COMBINE_T, cap), rank, aff, x1, mod, lng.reshape(1, d), lnb.reshape(1, d), y)


def _tile_bases(cum, tile, cap):
    b, n_exp, _ = cum.shape
    tile_base = jnp.swapaxes(cum[:, :, ::tile], 1, 2)
    return jnp.concatenate([tile_base, jnp.full((b, 1, n_exp), cap, I32)], axis=1).reshape(-1)


def _route(hf, aff):
    n = hf.shape[1]
    cap = EC_CAPACITY * n // aff.shape[1]
    rank, cum = _topk(aff, cap)
    return cum, rank, _gather(_tile_bases(cum, ROUTE_T, cap), rank, hf, cap), cap


def _moe(post_x, post_c, mod_x, mod_c, lng, lnb, w1, w3, w2, layer):
    x1, hf, aff = post_x
    cum, rank, xs, cap = _route(hf, aff)
    if post_c is None:
        (y,) = _ffn(xs, None, w1, w3, w2, layer, cap, None)
        return _combine(cum, rank, aff, x1, mod_x, lng, lnb, y, cap), None
    c1, hf_c, aff_c = post_c
    cum_c, rank_c, xs_c, cap_c = _route(hf_c, aff_c)
    y, y_c = _ffn(xs, xs_c, w1, w3, w2, layer, cap, cap_c)
    return (_combine(cum, rank, aff, x1, mod_x, lng, lnb, y, cap),
            _combine(cum_c, rank_c, aff_c, c1, mod_c, lng, lnb, y_c, cap_c))


def _gqa_layout(n_q_heads):
    group = n_q_heads // GQA_KV_HEADS
    half = HEAD_DIM // 2
    ev = np.arange(half) * 2
    od = ev + 1

    def slab(col_a, col_b):
        return np.concatenate([col_a + ev, col_b + ev, col_a + od, col_b + od])

    q_cols, k_cols, o_rows, gain_idx = [], [], [], []
    dq = n_q_heads * HEAD_DIM
    for p in range(GQA_KV_HEADS // 2):
        for i in range(group):
            a = (2 * p) * group + i
            c = (2 * p + 1) * group + i
            q_cols.append(slab(a * HEAD_DIM, c * HEAD_DIM))
            o_rows.append(np.concatenate([a * HEAD_DIM + np.arange(HEAD_DIM),
                                          c * HEAD_DIM + np.arange(HEAD_DIM)]))
    for p in range(GQA_KV_HEADS // 2):
        k_cols.append(dq + slab(2 * p * HEAD_DIM, (2 * p + 1) * HEAD_DIM))
    lane_dim = np.concatenate([ev, ev, od, od])
    return (np.concatenate(q_cols), np.concatenate(k_cols), np.concatenate(o_rows), lane_dim)


def _rope_tables(n):
    t = jnp.arange(n, dtype=I32)
    row = (t // GRID_W).astype(F32)
    col = (t % GRID_W).astype(F32)
    axis_dims = HEAD_DIM // 2
    inv_freq = jnp.power(ROPE_THETA, -jnp.arange(0, axis_dims, 2, dtype=F32) / axis_dims)
    ang = jnp.concatenate([row[:, None] * inv_freq, col[:, None] * inv_freq], axis=-1)
    cos, sin = jnp.cos(ang), jnp.sin(ang)
    cos_t = jnp.concatenate([cos] * 4, axis=1)
    sin_t = jnp.concatenate([-sin, -sin, sin, sin], axis=1)
    return cos_t, sin_t


def kernel(x, c, ctx, c_ctx, mod_w, mod_b, ln_g, ln_b, pool_w, pool_scale, na_wqkv, na_wo, na_rpb,
           gqa_wqkv, gqa_q_norm, gqa_k_norm, gqa_wo, moe_router, moe_w1, moe_w3, moe_w2):
    bsz, n, d = x.shape
    l = ctx.shape[1]
    n_exp = moe_router.shape[2]
    depth = mod_w.shape[0]
    cc = jnp.zeros((8, d), F32).at[:bsz].set(c).at[bsz].set(c_ctx)
    mod_all = _modulation(cc, mod_w, mod_b)

    for i in range(depth):
        m = i % N_MIXERS
        j = i // N_MIXERS
        update_ctx = any(k % N_MIXERS != 0 for k in range(i + 1, depth))
        ctx_keys = m != 0
        mod6 = mod_all[i].reshape(8, N_MOD, d)
        mod_x = jnp.zeros((bsz, 8, d), F32).at[:, :N_MOD].set(mod6[:bsz])
        mod_c = jnp.zeros((bsz, 8, d), F32).at[:, :N_MOD].set(
            jnp.broadcast_to(mod6[bsz][None], (bsz, N_MOD, d)))
        router_p = _pad_router(moe_router[i])
        lng0, lnb0, lng1, lnb1 = ln_g[i, 0], ln_b[i, 0], ln_g[i, 1], ln_b[i, 1]
        post_c = None
        if m == 0:
            post_x = _pool_layer(x, mod_x, pool_w[j], pool_scale[j], lng0, lnb0, router_p, n_exp)
            if update_ctx:
                post_c = _pool_layer(ctx, mod_c, pool_w[j], pool_scale[j], lng0, lnb0, router_p,
                                     n_exp)
        elif m == 1:
            wqkv = na_wqkv[j].astype(BF16)
            wo = na_wo[j].astype(BF16)
            qkv = _proj_in(x, mod_x, wqkv, d)
            qkv_c = _proj_in(ctx, mod_c, wqkv, d)
            o = _na_attention(qkv, qkv_c, _na_bias_table(na_rpb[j]))
            post_x = _proj_out(o, x, mod_x, wo, lng0, lnb0, router_p, n_exp)
            if update_ctx:
                oc = _ctx_attention(qkv_c)
                post_c = _proj_out(oc, ctx, mod_c, wo, lng0, lnb0, router_p, n_exp)
        else:
            n_q_heads = gqa_wo.shape[1] // HEAD_DIM
            dq = n_q_heads * HEAD_DIM
            dkv = GQA_KV_HEADS * HEAD_DIM
            q_cols, k_cols, o_rows, lane_dim = _gqa_layout(n_q_heads)
            v_cols = dq + dkv + np.arange(dkv)
            w_perm = gqa_wqkv[j][:, np.concatenate([q_cols, k_cols, v_cols])].astype(BF16)
            wo = gqa_wo[j][o_rows].astype(BF16)
            gain_row = jnp.concatenate(
                [jnp.tile(gqa_q_norm[j][lane_dim] * (HEAD_DIM ** -0.5 * LOG2_E), dq // LANES),
                 jnp.tile(gqa_k_norm[j][lane_dim], dkv // LANES),
                 jnp.ones((dkv,), F32)]).reshape(1, -1)
            lanes2 = np.arange(2 * LANES)
            lane_head = 2 * (lanes2 // LANES) + (lanes2 // (HEAD_DIM // 2)) % 2
            seg = jnp.asarray((lane_head[:, None] == lane_head[None, :]) / HEAD_DIM, BF16)
            cos_t, sin_t = _rope_tables(n)
            n_norm = (dq + dkv) // LANES
            qkv = _gqa_proj(x, mod_x, w_perm, gain_row, seg, cos_t, sin_t, n_norm)
            qkv_c = _gqa_proj(ctx, mod_c, w_perm, gain_row, seg, jnp.ones((l, LANES), F32),
                              jnp.zeros((l, LANES), F32), n_norm)
            kcat = jnp.concatenate([qkv[:, :, dq:dq + dkv], qkv_c[:, :, dq:dq + dkv]], axis=1)
            vcat = jnp.concatenate([qkv[:, :, dq + dkv:], qkv_c[:, :, dq + dkv:]], axis=1)
            q_gain_max = jnp.max(jnp.abs(gqa_q_norm[j])) * (HEAD_DIM ** -0.5 * LOG2_E)
            score_bound = jnp.full((1, LANES), 1.02 * HEAD_DIM, F32) * (
                q_gain_max * jnp.max(jnp.abs(gqa_k_norm[j])))
            o = _gqa_attention(qkv[:, :, :dq], kcat, vcat, score_bound)
            post_x = _proj_out(o, x, mod_x, wo, lng0, lnb0, router_p, n_exp)
            if update_ctx:
                raise NotImplementedError("context update after a GQA layer is not part of this stack")
        x, ctx_new = _moe(post_x, post_c if update_ctx else None, mod_x, mod_c, lng1, lnb1,
                          moe_w1, moe_w3, moe_w2, i)
        if update_ctx:
            ctx = ctx_new
    return x
```

```python
import functools
import math

import jax
import jax.numpy as jnp
import numpy as np
from jax import lax
from jax.experimental import pallas as pl
from jax.experimental.pallas import tpu as pltpu

F32 = jnp.float32
BF16 = jnp.bfloat16
I32 = jnp.int32
HIGHEST = lax.Precision.HIGHEST

DEPTH = 4
N_MIXERS = 3
GRID_W = 64
HEAD_DIM = 64
POOL_WINDOWS = (2, 4, 8, 16)
POOL_HALO = 8
NA_KH = 8
NA_KW = 16
NA_QROWS = 4
NA_KBLOCKS = 3
GQA_KV_HEADS = 4
ROPE_THETA = 10000.0
EC_CAPACITY = 2
N_MOD = 6
LN_EPS = 1e-5
RMS_EPS = 1e-6
ALPHA = (2.0 * DEPTH) ** 0.25
LANES = 128
ROUTE_T = 256
ROUTE_K = 64
COMBINE_T = 256
COMBINE_WIN = 80
ROW_ALIGN = 16
GATHER_ROWS = ROUTE_K + ROW_ALIGN
VMEM_LIMIT = 56 * 1024 * 1024
NEG_INF = -1e30
LOG2_E = math.log2(math.e)
EXP2_HEADROOM = 64.0


def _cparams(sem, vmem=VMEM_LIMIT):
    return pltpu.CompilerParams(dimension_semantics=sem, vmem_limit_bytes=vmem)


def _silu(v):
    return v / (1.0 + jnp.exp(-v))


def _mod_kernel(c_ref, w_ref, b_ref, o_ref):
    s = _silu(c_ref[...])
    o_ref[0] = jnp.dot(s, w_ref[0], precision=HIGHEST, preferred_element_type=F32) + b_ref[0]


def _modulation(cc, mod_w, mod_b):
    depth, d, nd = mod_w.shape
    tn = nd // 4
    return pl.pallas_call(
        _mod_kernel,
        grid=(depth, nd // tn),
        in_specs=[
            pl.BlockSpec((8, d), lambda i, j: (0, 0)),
            pl.BlockSpec((1, d, tn), lambda i, j: (i, 0, j)),
            pl.BlockSpec((1, 1, tn), lambda i, j: (i, 0, j)),
        ],
        out_specs=pl.BlockSpec((1, 8, tn), lambda i, j: (i, 0, j)),
        out_shape=jax.ShapeDtypeStruct((depth, 8, nd), F32),
        compiler_params=_cparams(("parallel", "parallel")),
    )(cc, mod_w, mod_b.reshape(depth, 1, nd))


def _layer_norm(z, g, b):
    mu = jnp.mean(z, axis=-1, keepdims=True)
    zc = z - mu
    var = jnp.mean(zc * zc, axis=-1, keepdims=True)
    return zc * lax.rsqrt(var + LN_EPS) * g + b


def _post_mixer(x, y, mod, lng, lnb, router, n_exp):
    x1 = _layer_norm(ALPHA * x + mod[2:3] * y, lng, lnb)
    hf = x1 * (1.0 + mod[4:5]) + mod[3:4]
    hf_hi, hf_lo = _split_bf16(hf)
    logits = (jnp.dot(hf_hi, router[0], preferred_element_type=F32)
              + jnp.dot(hf_lo, router[0], preferred_element_type=F32)
              + jnp.dot(hf_hi, router[1], preferred_element_type=F32))
    lt = logits.T[:n_exp]
    m = jnp.max(lt, axis=0, keepdims=True)
    p = jnp.exp(lt - m)
    aff = p / jnp.sum(p, axis=0, keepdims=True)
    return x1, hf_hi, aff


def _post_outs(b, n, d, n_exp):
    return (jax.ShapeDtypeStruct((b, n, d), F32),
            jax.ShapeDtypeStruct((b, n, d), BF16),
            jax.ShapeDtypeStruct((b, n_exp, n), F32))


def _post_out_specs(tm, d, n_exp):
    return (pl.BlockSpec((1, tm, d), lambda b, t: (b, t, 0)),
            pl.BlockSpec((1, tm, d), lambda b, t: (b, t, 0)),
            pl.BlockSpec((1, n_exp, tm), lambda b, t: (b, 0, t)))


def _pad_router(router):
    d, e = router.shape
    r = jnp.zeros((d, LANES), F32).at[:, :e].set(router)
    hi = lax.bitcast_convert_type(
        lax.bitcast_convert_type(r, jnp.uint32) & jnp.uint32(0xFFFF0000), F32)
    return jnp.stack([hi.astype(BF16), (r - hi).astype(BF16)])


def _pool_kernel(x_ref, xp_ref, xn_ref, mod_ref, pw_ref, ps_ref, lng_ref, lnb_ref, r_ref,
                 x1_ref, hf_ref, aff_ref, buf, *, n, tm, n_exp):
    t = pl.program_id(1)
    nt = pl.num_programs(1)
    mod = mod_ref[0]
    x = x_ref[0]
    sc = 1.0 + mod[1:2]
    sh = mod[0:1]
    h = x * sc + sh
    hp = xp_ref[0] * sc + sh
    hn = xn_ref[0] * sc + sh
    buf[0:POOL_HALO, :] = jnp.where(t > 0, hp, 0.0)
    buf[POOL_HALO:POOL_HALO + tm, :] = h
    buf[POOL_HALO + tm:, :] = jnp.where(t < nt - 1, hn, 0.0)
    pos = t * tm + lax.broadcasted_iota(I32, (tm, 1), 0)
    ch = x.shape[1] // len(POOL_WINDOWS)
    parts = []
    for g, w in enumerate(POOL_WINDOWS):
        cols = slice(g * ch, (g + 1) * ch)
        acc = None
        for o in range(-(w // 2), w - w // 2):
            v = buf[POOL_HALO + o:POOL_HALO + o + tm, cols]
            acc = v if acc is None else acc + v
        lo = jnp.maximum(pos - w // 2, 0)
        hi = jnp.minimum(pos + (w - w // 2 - 1), n - 1)
        cnt = (hi - lo + 1).astype(F32)
        dlt = acc / cnt - h[:, cols]
        parts.append(jnp.dot(dlt.astype(BF16), pw_ref[g], preferred_element_type=F32))
    y = jnp.concatenate(parts, axis=1) * ps_ref[...]
    x1, hf, aff = _post_mixer(x, y, mod, lng_ref[...], lnb_ref[...], r_ref[...], n_exp)
    x1_ref[0] = x1
    hf_ref[0] = hf
    aff_ref[0] = aff


def _pool_layer(x, mod, pool_w, pool_scale, lng, lnb, router_p, n_exp):
    b, n, d = x.shape
    tm = min(256, n)
    hb = tm // POOL_HALO
    nhb = n // POOL_HALO
    g, ch, _ = pool_w.shape
    kern = functools.partial(_pool_kernel, n=n, tm=tm, n_exp=n_exp)
    return pl.pallas_call(
        kern,
        grid=(b, n // tm),
        in_specs=[
            pl.BlockSpec((1, tm, d), lambda i, t: (i, t, 0)),
            pl.BlockSpec((1, POOL_HALO, d), lambda i, t: (i, jnp.maximum(t * hb - 1, 0), 0)),
            pl.BlockSpec((1, POOL_HALO, d), lambda i, t: (i, jnp.minimum((t + 1) * hb, nhb - 1), 0)),
            pl.BlockSpec((1, 8, d), lambda i, t: (i, 0, 0)),
            pl.BlockSpec((g, ch, ch), lambda i, t: (0, 0, 0)),
            pl.BlockSpec((1, d), lambda i, t: (0, 0)),
            pl.BlockSpec((1, d), lambda i, t: (0, 0)),
            pl.BlockSpec((1, d), lambda i, t: (0, 0)),
            pl.BlockSpec((2, d, LANES), lambda i, t: (0, 0, 0)),
        ],
        out_specs=_post_out_specs(tm, d, n_exp),
        out_shape=_post_outs(b, n, d, n_exp),
        scratch_shapes=[pltpu.VMEM((tm + 2 * POOL_HALO, d), F32)],
        compiler_params=_cparams(("parallel", "parallel")),
    )(x, x, x, mod, pool_w.astype(BF16), pool_scale.reshape(1, d), lng.reshape(1, d),
      lnb.reshape(1, d), router_p)


def _proj_in_kernel(x_ref, mod_ref, w_ref, o_ref, *, q_cols, tn):
    mod = mod_ref[0]
    h = (x_ref[0] * (1.0 + mod[1:2]) + mod[0:1]).astype(BF16)
    ncol = w_ref.shape[1]
    for j in range(ncol // tn):
        y = jnp.dot(h, w_ref[:, j * tn:(j + 1) * tn], preferred_element_type=F32)
        if (j + 1) * tn <= q_cols:
            y = y * (HEAD_DIM ** -0.5 * LOG2_E)
        o_ref[0, :, j * tn:(j + 1) * tn] = y.astype(o_ref.dtype)


def _proj_in(x, mod, w_bf16, q_cols):
    b, n, d = x.shape
    ncol = w_bf16.shape[1]
    tm = min(512, n)
    tn = 512
    kern = functools.partial(_proj_in_kernel, q_cols=q_cols, tn=tn)
    return pl.pallas_call(
        kern,
        grid=(b, n // tm),
        in_specs=[
            pl.BlockSpec((1, tm, d), lambda i, t: (i, t, 0)),
            pl.BlockSpec((1, 8, d), lambda i, t: (i, 0, 0)),
            pl.BlockSpec((d, ncol), lambda i, t: (0, 0)),
        ],
        out_specs=pl.BlockSpec((1, tm, ncol), lambda i, t: (i, t, 0)),
        out_shape=jax.ShapeDtypeStruct((b, n, ncol), BF16),
        compiler_params=_cparams(("parallel", "parallel")),
    )(x, mod, w_bf16)


def _gqa_proj_kernel(x_ref, mod_ref, w_ref, gain_ref, seg_ref, cos_ref, sin_ref, o_ref, *, n_norm):
    mod = mod_ref[0]
    h = (x_ref[0] * (1.0 + mod[1:2]) + mod[0:1]).astype(BF16)
    seg = seg_ref[...]
    cos = cos_ref[...]
    sin = sin_ref[...]
    ncol = w_ref.shape[1]
    wide = 2 * LANES
    for j in range(ncol // wide):
        cols = slice(j * wide, (j + 1) * wide)
        y = jnp.dot(h, w_ref[:, cols], preferred_element_type=F32)
        if 2 * j < n_norm:
            sq_hi, sq_lo = _split_bf16(y * y)
            ms = (jnp.dot(sq_hi, seg, preferred_element_type=F32)
                  + jnp.dot(sq_lo, seg, preferred_element_type=F32))
            yn = y * lax.rsqrt(ms + RMS_EPS) * gain_ref[:, cols]
            halves = []
            for k in range(2):
                part = yn[:, k * LANES:(k + 1) * LANES]
                halves.append(part * cos + pltpu.roll(part, LANES // 2, axis=1) * sin)
            y = jnp.concatenate(halves, axis=1)
        o_ref[0, :, cols] = y.astype(o_ref.dtype)


def _gqa_proj(x, mod, w_bf16, gain_row, seg, cos_t, sin_t, n_norm):
    b, n, d = x.shape
    ncol = w_bf16.shape[1]
    tm = min(512, n)
    kern = functools.partial(_gqa_proj_kernel, n_norm=n_norm)
    return pl.pallas_call(
        kern,
        grid=(b, n // tm),
        in_specs=[
            pl.BlockSpec((1, tm, d), lambda i, t: (i, t, 0)),
            pl.BlockSpec((1, 8, d), lambda i, t: (i, 0, 0)),
            pl.BlockSpec((d, ncol), lambda i, t: (0, 0)),
            pl.BlockSpec((1, ncol), lambda i, t: (0, 0)),
            pl.BlockSpec((2 * LANES, 2 * LANES), lambda i, t: (0, 0)),
            pl.BlockSpec((tm, LANES), lambda i, t: (t, 0)),
            pl.BlockSpec((tm, LANES), lambda i, t: (t, 0)),
        ],
        out_specs=pl.BlockSpec((1, tm, ncol), lambda i, t: (i, t, 0)),
        out_shape=jax.ShapeDtypeStruct((b, n, ncol), BF16),
        compiler_params=_cparams(("parallel", "parallel")),
    )(x, mod, w_bf16, gain_row, seg, cos_t, sin_t)


def _proj_out_kernel(o_ref, x_ref, mod_ref, w_ref, lng_ref, lnb_ref, r_ref,
                     x1_ref, hf_ref, aff_ref, *, n_exp):
    y = jnp.dot(o_ref[0], w_ref[...], preferred_element_type=F32)
    x1, hf, aff = _post_mixer(x_ref[0], y, mod_ref[0], lng_ref[...], lnb_ref[...], r_ref[...], n_exp)
    x1_ref[0] = x1
    hf_ref[0] = hf
    aff_ref[0] = aff


def _proj_out(o, x, mod, w_bf16, lng, lnb, router_p, n_exp):
    b, n, d = x.shape
    tm = min(256, n)
    kern = functools.partial(_proj_out_kernel, n_exp=n_exp)
    return pl.pallas_call(
        kern,
        grid=(b, n // tm),
        in_specs=[
            pl.BlockSpec((1, tm, o.shape[2]), lambda i, t: (i, t, 0)),
            pl.BlockSpec((1, tm, d), lambda i, t: (i, t, 0)),
            pl.BlockSpec((1, 8, d), lambda i, t: (i, 0, 0)),
            pl.BlockSpec(w_bf16.shape, lambda i, t: (0, 0)),
            pl.BlockSpec((1, d), lambda i, t: (0, 0)),
            pl.BlockSpec((1, d), lambda i, t: (0, 0)),
            pl.BlockSpec((2, d, LANES), lambda i, t: (0, 0, 0)),
        ],
        out_specs=_post_out_specs(tm, d, n_exp),
        out_shape=_post_outs(b, n, d, n_exp),
        compiler_params=_cparams(("parallel", "parallel")),
    )(o, x, mod, w_bf16, lng.reshape(1, d), lnb.reshape(1, d), router_p)


def _lane_tiles(blocks):
    return [s[:, i * LANES:(i + 1) * LANES] for s in blocks for i in range(s.shape[1] // LANES)]


def _softmax_pv(scores, values):
    m = jnp.max(functools.reduce(jnp.maximum, _lane_tiles(scores)), axis=1, keepdims=True)
    ps = [jnp.exp2(s - m) for s in scores]
    l = jnp.sum(functools.reduce(lambda a, c: a + c, _lane_tiles(ps)), axis=1, keepdims=True)
    o = None
    for p, v in zip(ps, values):
        c = jnp.dot(p.astype(BF16), v, preferred_element_type=F32)
        o = c if o is None else o + c
    return o / l


def _na_kernel(q_ref, ka_ref, kb_ref, kc_ref, va_ref, vb_ref, vc_ref, kx_ref, vx_ref, bias_ref,
               o_ref, *, rows):
    i = pl.program_id(1)
    nb = pl.num_programs(1)
    tq = q_ref.shape[1]
    rq0 = i * NA_QROWS
    ks = jnp.clip(i - 1, 0, nb - NA_KBLOCKS) * NA_QROWS
    qi = lax.broadcasted_iota(I32, (tq, tq), 0)
    ki = lax.broadcasted_iota(I32, (tq, tq), 1)
    q_row = rq0 + qi // GRID_W
    q_col = qi % GRID_W
    k_col = ki % GRID_W
    r0 = jnp.clip(q_row - NA_KH // 2, 0, rows - NA_KH)
    c0 = jnp.clip(q_col - NA_KW // 2, 0, GRID_W - NA_KW)
    col_ok = (k_col >= c0) & (k_col < c0 + NA_KW)
    mask_add = []
    for blk in range(NA_KBLOCKS):
        k_row = ks + blk * NA_QROWS + ki // GRID_W
        ok = col_ok & (k_row >= r0) & (k_row < r0 + NA_KH)
        mask_add.append(jnp.where(ok, 0.0, NEG_INF))
    lane = lax.broadcasted_iota(I32, (1, LANES), 1)
    k_refs = (ka_ref, kb_ref, kc_ref)
    v_refs = (va_ref, vb_ref, vc_ref)
    n_pairs = q_ref.shape[2] // LANES
    for p in range(n_pairs):
        cols = slice(p * LANES, (p + 1) * LANES)
        q2 = q_ref[0, :, cols]
        kt = [r[0, :, cols] for r in k_refs] + [kx_ref[0, :, cols]]
        vt = [r[0, :, cols] for r in v_refs] + [vx_ref[0, :, cols]]
        outs = []
        for hh in range(2):
            head = 2 * p + hh
            hmask = (lane // HEAD_DIM) == hh
            qm = jnp.where(hmask, q2, jnp.zeros_like(q2))
            scores = []
            for blk in range(NA_KBLOCKS):
                s = lax.dot_general(qm, kt[blk], (((1,), (1,)), ((), ())),
                                    preferred_element_type=F32)
                bands = []
                for qr in range(NA_QROWS):
                    halves = []
                    for kp in range(NA_QROWS // 2):
                        dr = ks + blk * NA_QROWS + 2 * kp - (rq0 + qr)
                        idx = jnp.clip(dr + NA_KH, 0, 2 * NA_KH - 1)
                        halves.append(bias_ref[head, idx])
                    bands.append(jnp.concatenate(halves, axis=1))
                bias = jnp.concatenate(bands, axis=0)
                scores.append(s + bias + mask_add[blk])
            scores.append(lax.dot_general(qm, kt[NA_KBLOCKS], (((1,), (1,)), ((), ())),
                                          preferred_element_type=F32))
            outs.append(_softmax_pv(scores, vt))
        o_ref[0, :, cols] = jnp.where((lane // HEAD_DIM) == 0, outs[0], outs[1]).astype(o_ref.dtype)


def _na_bias_table(rpb):
    col = np.arange(GRID_W)
    dc = np.clip(col[None, :] - col[:, None] + (NA_KW - 1), 0, 2 * NA_KW - 2)
    t = rpb[:, :, dc]
    t_first = jnp.concatenate([t[:, :1], t], axis=1)
    t_next = jnp.concatenate([t, t[:, -1:]], axis=1)
    return jnp.concatenate([t_first, t_next], axis=-1).astype(F32) * LOG2_E


def _na_attention(qkv, qkv_c, bias_tab):
    b, n, d3 = qkv.shape
    d = d3 // 3
    l = qkv_c.shape[1]
    rows = n // GRID_W
    tq = NA_QROWS * GRID_W
    nb = n // tq
    heads = d // HEAD_DIM

    def kmap(off, col):
        return lambda i, t: (i, jnp.clip(t - 1, 0, nb - NA_KBLOCKS) + off, col)

    kern = functools.partial(_na_kernel, rows=rows)
    return pl.pallas_call(
        kern,
        grid=(b, nb),
        in_specs=[
            pl.BlockSpec((1, tq, d), lambda i, t: (i, t, 0)),
            pl.BlockSpec((1, tq, d), kmap(0, 1)),
            pl.BlockSpec((1, tq, d), kmap(1, 1)),
            pl.BlockSpec((1, tq, d), kmap(2, 1)),
            pl.BlockSpec((1, tq, d), kmap(0, 2)),
            pl.BlockSpec((1, tq, d), kmap(1, 2)),
            pl.BlockSpec((1, tq, d), kmap(2, 2)),
            pl.BlockSpec((1, l, d), lambda i, t: (i, 0, 1)),
            pl.BlockSpec((1, l, d), lambda i, t: (i, 0, 2)),
            pl.BlockSpec((heads, 2 * NA_KH, GRID_W, 2 * GRID_W), lambda i, t: (0, 0, 0, 0)),
        ],
        out_specs=pl.BlockSpec((1, tq, d), lambda i, t: (i, t, 0)),
        out_shape=jax.ShapeDtypeStruct((b, n, d), BF16),
        compiler_params=_cparams(("parallel", "parallel")),
    )(qkv, qkv, qkv, qkv, qkv, qkv, qkv, qkv_c, qkv_c, bias_tab)


def _ctx_attn_kernel(q_ref, k_ref, v_ref, o_ref):
    lane = lax.broadcasted_iota(I32, (1, LANES), 1)
    for p in range(q_ref.shape[2] // LANES):
        cols = slice(p * LANES, (p + 1) * LANES)
        q2 = q_ref[0, :, cols]
        k2 = k_ref[0, :, cols]
        v2 = v_ref[0, :, cols]
        outs = []
        for hh in range(2):
            qm = jnp.where((lane // HEAD_DIM) == hh, q2, jnp.zeros_like(q2))
            s = lax.dot_general(qm, k2, (((1,), (1,)), ((), ())), preferred_element_type=F32)
            outs.append(_softmax_pv([s], [v2]))
        o_ref[0, :, cols] = jnp.where((lane // HEAD_DIM) == 0, outs[0], outs[1]).astype(o_ref.dtype)


def _ctx_attention(qkv_c):
    b, l, d3 = qkv_c.shape
    d = d3 // 3
    return pl.pallas_call(
        _ctx_attn_kernel,
        grid=(b,),
        in_specs=[pl.BlockSpec((1, l, d), lambda i: (i, 0, 0)),
                  pl.BlockSpec((1, l, d), lambda i: (i, 0, 1)),
                  pl.BlockSpec((1, l, d), lambda i: (i, 0, 2))],
        out_specs=pl.BlockSpec((1, l, d), lambda i: (i, 0, 0)),
        out_shape=jax.ShapeDtypeStruct((b, l, d), BF16),
        compiler_params=_cparams(("parallel",)),
    )(qkv_c, qkv_c, qkv_c)


def _gqa_kernel(q_ref, k_ref, v_ref, bound_ref, o_ref, qm_sc, m_sc, l_sc, acc_sc, *, tk):
    nk = k_ref.shape[1] // tk
    tq = q_ref.shape[1]
    n_sl = q_ref.shape[2] // LANES
    lane = lax.broadcasted_iota(I32, (1, LANES), 1)
    for sl in range(n_sl):
        q2 = q_ref[0, :, sl * LANES:(sl + 1) * LANES]
        for hh in range(2):
            h = 2 * sl + hh
            qm_sc[h * tq:(h + 1) * tq, :] = jnp.where(((lane // (HEAD_DIM // 2)) % 2) == hh, q2,
                                                      jnp.zeros_like(q2))
    l_sc[...] = jnp.zeros_like(l_sc)
    acc_sc[...] = jnp.zeros_like(acc_sc)

    def scores(j):
        k0 = pl.multiple_of(j * tk, tk)
        return lax.dot_general(qm_sc[...], k_ref[0, pl.ds(k0, tk), :], (((1,), (1,)), ((), ())),
                               preferred_element_type=F32)

    def lane_partial_sum(p):
        return functools.reduce(lambda u, w: u + w,
                                [p[:, i * LANES:(i + 1) * LANES] for i in range(tk // LANES)])

    def values(j):
        return v_ref[0, pl.ds(pl.multiple_of(j * tk, tk), tk), :]

    bound = bound_ref[0:1, 0:1]
    reference = bound - EXP2_HEADROOM
    safe = jnp.max(bound) <= EXP2_HEADROOM

    @pl.when(safe)
    def _():
        def body(j, c):
            p = jnp.exp2(scores(j) - reference)
            l_sc[...] += lane_partial_sum(p)
            acc_sc[...] += jnp.dot(p.astype(BF16), values(j), preferred_element_type=F32)
            return c

        lax.fori_loop(0, nk, body, 0)

    @pl.when(jnp.logical_not(safe))
    def _():
        m_sc[...] = jnp.full_like(m_sc, NEG_INF)

        def body(j, c):
            s = scores(j)
            m_old = m_sc[...]
            m_new = jnp.maximum(m_old, jnp.max(s, axis=1, keepdims=True))
            a = jnp.exp2(m_old - m_new)
            p = jnp.exp2(s - m_new)
            l_sc[...] = a * l_sc[...] + lane_partial_sum(p)
            acc_sc[...] = a * acc_sc[...] + jnp.dot(p.astype(BF16), values(j),
                                                    preferred_element_type=F32)
            m_sc[...] = m_new
            return c

        lax.fori_loop(0, nk, body, 0)

    out = acc_sc[...] / jnp.sum(l_sc[...], axis=1, keepdims=True)
    for sl in range(n_sl):
        o0 = out[2 * sl * tq:(2 * sl + 1) * tq]
        o1 = out[(2 * sl + 1) * tq:(2 * sl + 2) * tq]
        o_ref[0, :, sl * LANES:(sl + 1) * LANES] = jnp.where(
            (lane // HEAD_DIM) == 0, o0, o1).astype(o_ref.dtype)


def _gqa_attention(q, kcat, vcat, score_bound):
    b, n, dq = q.shape
    nk_tot = kcat.shape[1]
    n_slab = kcat.shape[2] // LANES
    q_per = dq // n_slab
    n_heads = 2 * q_per // LANES
    tq = min(256, n)
    tk = 256
    for cand in (1024, 768, 512):
        if nk_tot % cand == 0:
            tk = cand
            break
    kern = functools.partial(_gqa_kernel, tk=tk)
    return pl.pallas_call(
        kern,
        grid=(b, n_slab, n // tq),
        in_specs=[
            pl.BlockSpec((1, tq, q_per), lambda i, p, t: (i, t, p)),
            pl.BlockSpec((1, nk_tot, LANES), lambda i, p, t: (i, 0, p)),
            pl.BlockSpec((1, nk_tot, LANES), lambda i, p, t: (i, 0, p)),
            pl.BlockSpec((1, LANES), lambda i, p, t: (0, 0)),
        ],
        out_specs=pl.BlockSpec((1, tq, q_per), lambda i, p, t: (i, t, p)),
        out_shape=jax.ShapeDtypeStruct((b, n, dq), BF16),
        scratch_shapes=[pltpu.VMEM((n_heads * tq, LANES), BF16),
                        pltpu.VMEM((n_heads * tq, 1), F32),
                        pltpu.VMEM((n_heads * tq, LANES), F32),
                        pltpu.VMEM((n_heads * tq, LANES), F32)],
        compiler_params=_cparams(("parallel", "parallel", "parallel")),
    )(q, kcat, vcat, score_bound)


def _prefix_count(mask_f32, tri, out_cb):
    e, n = mask_f32.shape
    c = tri.shape[0]
    carry = jnp.zeros((e, 1), F32)
    for j in range(n // c):
        chunk = mask_f32[:, j * c:(j + 1) * c]
        inc = jnp.dot(chunk.astype(BF16), tri, preferred_element_type=F32) + carry
        out_cb(j, c, chunk, inc)
        carry = inc[:, c - 1:c]


def _topk_kernel(aff_ref, rank_ref, cum_ref, eqx_ref, *, cap):
    a = aff_ref[0]
    e, n = a.shape
    bits = lax.bitcast_convert_type(a, I32)

    def body(i, thr):
        cand = thr | (jnp.int32(1) << (30 - i))
        cnt = jnp.sum(jnp.where(bits >= cand, 1.0, 0.0), axis=1, keepdims=True)
        return jnp.where(cnt >= cap, cand, thr)

    thr = lax.fori_loop(0, 31, body, jnp.zeros((e, 1), I32))
    gt = bits > thr
    eq = bits == thr
    need = cap - jnp.sum(jnp.where(gt, 1.0, 0.0), axis=1, keepdims=True)
    c = min(256, n)
    tri = jnp.where(lax.broadcasted_iota(I32, (c, c), 0) <= lax.broadcasted_iota(I32, (c, c), 1),
                    1.0, 0.0).astype(BF16)

    def eq_cb(j, c, chunk, inc):
        eqx_ref[:, j * c:(j + 1) * c] = inc - chunk

    _prefix_count(jnp.where(eq, 1.0, 0.0), tri, eq_cb)
    sel = gt | (eq & (eqx_ref[...] < need))

    def sel_cb(j, c, chunk, inc):
        excl = (inc - chunk).astype(I32)
        cum_ref[0, :, j * c:(j + 1) * c] = excl
        rank_ref[0, :, j * c:(j + 1) * c] = jnp.where(chunk > 0.5, excl, -1)

    _prefix_count(jnp.where(sel, 1.0, 0.0), tri, sel_cb)


def _topk(aff, cap):
    b, e, n = aff.shape
    kern = functools.partial(_topk_kernel, cap=cap)
    return pl.pallas_call(
        kern,
        grid=(b,),
        in_specs=[pl.BlockSpec((1, e, n), lambda i: (i, 0, 0))],
        out_specs=(pl.BlockSpec((1, e, n), lambda i: (i, 0, 0)),
                   pl.BlockSpec((1, e, n), lambda i: (i, 0, 0))),
        out_shape=(jax.ShapeDtypeStruct((b, e, n), I32), jax.ShapeDtypeStruct((b, e, n), I32)),
        scratch_shapes=[pltpu.VMEM((e, n), F32)],
        compiler_params=_cparams(("parallel",)),
    )(aff)


def _slot_onehot(rank, starts, lows, k_new, rows_per, n_exp, weights=None):
    t = rank.shape[1]
    j_iota = lax.broadcasted_iota(I32, (rows_per, t), 0)
    rows = []
    for e in range(n_exp):
        r = rank[e:e + 1, :]
        local = jnp.where((r >= lows[e]) & (r < lows[e] + k_new), r - starts[e], -1)
        value = 1.0 if weights is None else weights[e:e + 1, :]
        rows.append(jnp.where(local == j_iota, value, 0.0))
    return jnp.concatenate(rows, axis=0)


def _align_down(v, a):
    return pl.multiple_of((v // a) * a, a)


def _gather_kernel(base_ref, rank_ref, hf_ref, xs_hbm, stage, carry, sem, pend, *, n_exp, n_tiles,
                   cap):
    b = pl.program_id(0)
    t = pl.program_id(1)
    off = (b * (n_tiles + 1) + t) * n_exp
    base = [base_ref[off + e] for e in range(n_exp)]
    cnt = [base_ref[off + n_exp + e] - base[e] for e in range(n_exp)]
    kmax = functools.reduce(jnp.maximum, cnt)
    rank = rank_ref[0]
    hf = hf_ref[0]

    def window_copy(e, start):
        return pltpu.make_async_copy(stage.at[e * GATHER_ROWS:(e + 1) * GATHER_ROWS, :],
                                     xs_hbm.at[e, b, pl.ds(start, GATHER_ROWS), :], sem.at[e])

    def drain():
        @pl.when(pend[0] == 1)
        def _():
            for e in range(n_exp):
                window_copy(e, 0).wait()
            pend[0] = 0

    @pl.when((b == 0) & (t == 0))
    def _():
        pend[0] = 0

    @pl.when(t == 0)
    def _():
        drain()
        carry[...] = jnp.zeros_like(carry)
        stage[...] = jnp.zeros_like(stage)
        for e in range(n_exp):
            window_copy(e, cap).start()
        pend[0] = 1

    def write_round(r, guarded):
        lows = [base[e] + r * ROUTE_K for e in range(n_exp)]
        starts = [_align_down(lows[e], ROW_ALIGN) for e in range(n_exp)]
        onehot = _slot_onehot(rank, starts, lows, ROUTE_K, GATHER_ROWS, n_exp).astype(BF16)
        rows = jnp.dot(onehot, hf, preferred_element_type=F32).astype(BF16)
        drain()
        stage[...] = rows
        for e in range(n_exp):
            r0 = e * GATHER_ROWS
            stage[r0:r0 + ROW_ALIGN, :] += carry[e * ROW_ALIGN:(e + 1) * ROW_ALIGN, :]

        def move(e):
            window_copy(e, starts[e]).start()
            filled = lows[e] + jnp.clip(cnt[e] - r * ROUTE_K, 0, ROUTE_K)
            shift = _align_down(filled, ROW_ALIGN) - starts[e]
            src = pl.multiple_of(e * GATHER_ROWS + shift, ROW_ALIGN)
            carry[e * ROW_ALIGN:(e + 1) * ROW_ALIGN, :] = stage[pl.ds(src, ROW_ALIGN), :]

        if not guarded:
            for e in range(n_exp):
                move(e)
            pend[0] = 1
            return
        for e in range(n_exp):
            @pl.when(cnt[e] > r * ROUTE_K)
            def _(e=e):
                move(e)
        for e in range(n_exp):
            @pl.when(cnt[e] > r * ROUTE_K)
            def _(e=e):
                window_copy(e, 0).wait()

    write_round(0, False)

    def extra_round(r, c):
        write_round(r, True)
        return c

    lax.fori_loop(1, (kmax + ROUTE_K - 1) // ROUTE_K, extra_round, 0)

    @pl.when((b == pl.num_programs(0) - 1) & (t == n_tiles - 1))
    def _():
        drain()


def _gather(base_flat, rank, hf, cap):
    b, n, d = hf.shape
    n_exp = rank.shape[1]
    n_tiles = n // ROUTE_T
    kern = functools.partial(_gather_kernel, n_exp=n_exp, n_tiles=n_tiles, cap=cap)
    return pl.pallas_call(
        kern,
        grid_spec=pltpu.PrefetchScalarGridSpec(
            num_scalar_prefetch=1,
            grid=(b, n_tiles),
            in_specs=[
                pl.BlockSpec((1, n_exp, ROUTE_T), lambda i, t, s: (i, 0, t)),
                pl.BlockSpec((1, ROUTE_T, d), lambda i, t, s: (i, t, 0)),
            ],
            out_specs=pl.BlockSpec(memory_space=pl.ANY),
            scratch_shapes=[pltpu.VMEM((n_exp * GATHER_ROWS, d), BF16),
                            pltpu.VMEM((n_exp * ROW_ALIGN, d), BF16),
                            pltpu.SemaphoreType.DMA((n_exp,)),
                            pltpu.SMEM((1,), I32)],
        ),
        out_shape=jax.ShapeDtypeStruct((n_exp, b, cap + GATHER_ROWS, d), BF16),
        compiler_params=_cparams(("arbitrary", "arbitrary")),
    )(base_flat, rank, hf)


def _split_bf16(v):
    hi = v.astype(BF16)
    return hi, (v - hi.astype(F32)).astype(BF16)


def _ffn_kernel(*refs, n_f, n_b, with_ctx):
    if with_ctx:
        x_ref, xc_ref, w1_ref, w3_ref, w2_ref, y_ref, yc_ref, acc = refs
    else:
        x_ref, w1_ref, w3_ref, w2_ref, y_ref, acc = refs
    i = pl.program_id(1)
    f = pl.program_id(2)
    w1 = w1_ref[0, 0].astype(BF16)
    w3 = w3_ref[0, 0].astype(BF16)
    w2 = w2_ref[0, 0].astype(BF16)

    def run(x, emit):
        rows = x.shape[0]
        h1 = jnp.dot(x, w1, preferred_element_type=F32)
        h3 = jnp.dot(x, w3, preferred_element_type=F32)
        hid = (_silu(h1) * h3).astype(BF16)
        y = jnp.dot(hid, w2, preferred_element_type=F32)
        if n_f == 1:
            emit(y.astype(BF16))
            return

        @pl.when(f == 0)
        def _():
            acc[0:rows, :] = y

        @pl.when((f > 0) & (f < n_f - 1))
        def _():
            acc[0:rows, :] += y

        @pl.when(f == n_f - 1)
        def _():
            emit((acc[0:rows, :] + y).astype(BF16))

    def emit_main(out):
        y_ref[0, 0] = out

    def run_main():
        run(x_ref[0, 0], emit_main)

    if not with_ctx:
        run_main()
        return

    bc, cap_c, d = xc_ref.shape[1:]

    def emit_ctx(out):
        yc_ref[0] = out.reshape(bc, cap_c, d)

    pl.when(i < n_b)(run_main)

    @pl.when(i == n_b)
    def _():
        run(xc_ref[0].reshape(bc * cap_c, d), emit_ctx)


def _ffn(xs, xs_c, w1, w3, w2, layer, cap, cap_c):
    n_exp, b, _, d = xs.shape
    ff = w1.shape[3]
    tf = min(512, ff)
    n_f = ff // tf
    with_ctx = xs_c is not None
    kern = functools.partial(_ffn_kernel, n_f=n_f, n_b=b, with_ctx=with_ctx)

    def main_map(e, i, f):
        return (e, jnp.minimum(i, b - 1), 0, 0)

    x_specs = [pl.BlockSpec((1, 1, cap, d), main_map)]
    out_specs = [pl.BlockSpec((1, 1, cap, d), main_map)]
    out_shape = [jax.ShapeDtypeStruct((n_exp, b, cap, d), BF16)]
    operands = [xs]
    if with_ctx:
        assert b * cap_c <= cap
        c_spec = pl.BlockSpec((1, b, cap_c, d), lambda e, i, f: (e, 0, 0, 0))
        x_specs.append(c_spec)
        out_specs.append(c_spec)
        out_shape.append(jax.ShapeDtypeStruct((n_exp, b, cap_c, d), BF16))
        operands.append(xs_c)
    return pl.pallas_call(
        kern,
        grid=(n_exp, b + (1 if with_ctx else 0), n_f),
        in_specs=x_specs + [
            pl.BlockSpec((1, 1, d, tf), lambda e, i, f: (layer, e, 0, f)),
            pl.BlockSpec((1, 1, d, tf), lambda e, i, f: (layer, e, 0, f)),
            pl.BlockSpec((1, 1, tf, d), lambda e, i, f: (layer, e, f, 0)),
        ],
        out_specs=tuple(out_specs),
        out_shape=tuple(out_shape),
        scratch_shapes=[pltpu.VMEM((cap, d), F32)],
        compiler_params=_cparams(("parallel", "arbitrary", "arbitrary")),
    )(*operands, w1, w3, w2)


def _combine_kernel(base_ref, rank_ref, aff_ref, x1_ref, mod_ref, lng_ref, lnb_ref, y_hbm,
                    out_ref, ybuf, sem, *, n_exp, n_tiles, cap, win):
    k_new = win - ROW_ALIGN
    b = pl.program_id(0)
    t = pl.program_id(1)
    g = b * n_tiles + t
    total = pl.num_programs(0) * n_tiles
    slot = g % 2
    tt = rank_ref.shape[2]

    def tile_info(step):
        off = ((step // n_tiles) * (n_tiles + 1) + step % n_tiles) * n_exp
        base = [base_ref[off + e] for e in range(n_exp)]
        cnt = [base_ref[off + n_exp + e] - base[e] for e in range(n_exp)]
        return base, cnt

    def window(base, r, e):
        low = base[e] + r * k_new
        start = pl.multiple_of(jnp.minimum(_align_down(low, ROW_ALIGN), cap - win), ROW_ALIGN)
        return low, start

    def copy(step, start, e, sl):
        return pltpu.make_async_copy(
            y_hbm.at[e, step // n_tiles, pl.ds(start, win), :],
            ybuf.at[sl, pl.ds(e * win, win), :], sem.at[sl, e])

    def issue(step, r, sl):
        base, cnt = tile_info(step)
        for e in range(n_exp):
            def go(e=e):
                copy(step, window(base, r, e)[1], e, sl).start()
            if isinstance(r, int) and r == 0:
                go()
            else:
                pl.when(cnt[e] > r * k_new)(go)

    def wait(step, r, sl):
        _, cnt = tile_info(step)
        for e in range(n_exp):
            def go(e=e):
                copy(step, 0, e, sl).wait()
            if isinstance(r, int) and r == 0:
                go()
            else:
                pl.when(cnt[e] > r * k_new)(go)

    @pl.when(g == 0)
    def _():
        ybuf[...] = jnp.zeros_like(ybuf)
        issue(g, 0, slot)

    @pl.when(g + 1 < total)
    def _():
        issue(g + 1, 0, 1 - slot)

    base, cnt = tile_info(g)
    kmax = functools.reduce(jnp.maximum, cnt)
    rank = rank_ref[0]
    aff = aff_ref[0]

    def scatter(r, f):
        lows, starts = zip(*[window(base, r, e) for e in range(n_exp)])
        gates = _slot_onehot(rank, starts, lows, k_new, win, n_exp, weights=aff)
        wait(g, r, slot)
        return f + jnp.dot(gates.T.astype(BF16), ybuf[slot], preferred_element_type=F32)

    def extra_round(r, f):
        issue(g, r, slot)
        return scatter(r, f)

    d = x1_ref.shape[2]
    f = scatter(0, jnp.zeros((tt, d), F32))
    f = lax.fori_loop(1, (kmax + k_new - 1) // k_new, extra_round, f)
    mod = mod_ref[0]
    out_ref[0] = _layer_norm(ALPHA * x1_ref[0] + mod[5:6] * f, lng_ref[...], lnb_ref[...])


def _combine(cum, rank, aff, x1, mod, lng, lnb, y, cap):
    b, n, d = x1.shape
    n_exp = rank.shape[1]
    n_tiles = n // COMBINE_T
    win = min(COMBINE_WIN, cap)
    assert cap % ROW_ALIGN == 0 and win > ROW_ALIGN
    kern = functools.partial(_combine_kernel, n_exp=n_exp, n_tiles=n_tiles, cap=cap, win=win)
    return pl.pallas_call(
        kern,
        grid_spec=pltpu.PrefetchScalarGridSpec(
            num_scalar_prefetch=1,
            grid=(b, n_tiles),
            in_specs=[
                pl.BlockSpec((1, n_exp, COMBINE_T), lambda i, t, s: (i, 0, t)),
                pl.BlockSpec((1, n_exp, COMBINE_T), lambda i, t, s: (i, 0, t)),
                pl.BlockSpec((1, COMBINE_T, d), lambda i, t, s: (i, t, 0)),
                pl.BlockSpec((1, 8, d), lambda i, t, s: (i, 0, 0)),
                pl.BlockSpec((1, d), lambda i, t, s: (0, 0)),
                pl.BlockSpec((1, d), lambda i, t, s: (0, 0)),
                pl.BlockSpec(memory_space=pl.ANY),
            ],
            out_specs=pl.BlockSpec((1, COMBINE_T, d), lambda i, t, s: (i, t, 0)),
            scratch_shapes=[pltpu.VMEM((2, n_exp * win, d), BF16),
                            pltpu.SemaphoreType.DMA((2, n_exp))],
        ),
        out_shape=jax.ShapeDtypeStruct((b, n, d), F32),
        compiler_params=_cparams(("arbitrary", "arbitrary")),
    )(_tile_bases(cum, COMBINE_T, cap), rank, aff, x1, mod, lng.reshape(1, d), lnb.reshape(1, d), y)


def _tile_bases(cum, tile, cap):
    b, n_exp, _ = cum.shape
    tile_base = jnp.swapaxes(cum[:, :, ::tile], 1, 2)
    return jnp.concatenate([tile_base, jnp.full((b, 1, n_exp), cap, I32)], axis=1).reshape(-1)


def _route(hf, aff):
    n = hf.shape[1]
    cap = EC_CAPACITY * n // aff.shape[1]
    rank, cum = _topk(aff, cap)
    return cum, rank, _gather(_tile_bases(cum, ROUTE_T, cap), rank, hf, cap), cap


def _moe(post_x, post_c, mod_x, mod_c, lng, lnb, w1, w3, w2, layer):
    x1, hf, aff = post_x
    cum, rank, xs, cap = _route(hf, aff)
    if post_c is None:
        (y,) = _ffn(xs, None, w1, w3, w2, layer, cap, None)
        return _combine(cum, rank, aff, x1, mod_x, lng, lnb, y, cap), None
    c1, hf_c, aff_c = post_c
    cum_c, rank_c, xs_c, cap_c = _route(hf_c, aff_c)
    y, y_c = _ffn(xs, xs_c, w1, w3, w2, layer, cap, cap_c)
    return (_combine(cum, rank, aff, x1, mod_x, lng, lnb, y, cap),
            _combine(cum_c, rank_c, aff_c, c1, mod_c, lng, lnb, y_c, cap_c))


def _gqa_layout(n_q_heads):
    group = n_q_heads // GQA_KV_HEADS
    half = HEAD_DIM // 2
    ev = np.arange(half) * 2
    od = ev + 1

    def slab(col_a, col_b):
        return np.concatenate([col_a + ev, col_b + ev, col_a + od, col_b + od])

    q_cols, k_cols, o_rows, gain_idx = [], [], [], []
    dq = n_q_heads * HEAD_DIM
    for p in range(GQA_KV_HEADS // 2):
        for i in range(group):
            a = (2 * p) * group + i
            c = (2 * p + 1) * group + i
            q_cols.append(slab(a * HEAD_DIM, c * HEAD_DIM))
            o_rows.append(np.concatenate([a * HEAD_DIM + np.arange(HEAD_DIM),
                                          c * HEAD_DIM + np.arange(HEAD_DIM)]))
    for p in range(GQA_KV_HEADS // 2):
        k_cols.append(dq + slab(2 * p * HEAD_DIM, (2 * p + 1) * HEAD_DIM))
    lane_dim = np.concatenate([ev, ev, od, od])
    return (np.concatenate(q_cols), np.concatenate(k_cols), np.concatenate(o_rows), lane_dim)


def _rope_tables(n):
    t = jnp.arange(n, dtype=I32)
    row = (t // GRID_W).astype(F32)
    col = (t % GRID_W).astype(F32)
    axis_dims = HEAD_DIM // 2
    inv_freq = jnp.power(ROPE_THETA, -jnp.arange(0, axis_dims, 2, dtype=F32) / axis_dims)
    ang = jnp.concatenate([row[:, None] * inv_freq, col[:, None] * inv_freq], axis=-1)
    cos, sin = jnp.cos(ang), jnp.sin(ang)
    cos_t = jnp.concatenate([cos] * 4, axis=1)
    sin_t = jnp.concatenate([-sin, -sin, sin, sin], axis=1)
    return cos_t, sin_t


def kernel(x, c, ctx, c_ctx, mod_w, mod_b, ln_g, ln_b, pool_w, pool_scale, na_wqkv, na_wo, na_rpb,
           gqa_wqkv, gqa_q_norm, gqa_k_norm, gqa_wo, moe_router, moe_w1, moe_w3, moe_w2):
    bsz, n, d = x.shape
    l = ctx.shape[1]
    n_exp = moe_router.shape[2]
    depth = mod_w.shape[0]
    cc = jnp.zeros((8, d), F32).at[:bsz].set(c).at[bsz].set(c_ctx)
    mod_all = _modulation(cc, mod_w, mod_b)

    for i in range(depth):
        m = i % N_MIXERS
        j = i // N_MIXERS
        update_ctx = any(k % N_MIXERS != 0 for k in range(i + 1, depth))
        ctx_keys = m != 0
        mod6 = mod_all[i].reshape(8, N_MOD, d)
        mod_x = jnp.zeros((bsz, 8, d), F32).at[:, :N_MOD].set(mod6[:bsz])
        mod_c = jnp.zeros((bsz, 8, d), F32).at[:, :N_MOD].set(
            jnp.broadcast_to(mod6[bsz][None], (bsz, N_MOD, d)))
        router_p = _pad_router(moe_router[i])
        lng0, lnb0, lng1, lnb1 = ln_g[i, 0], ln_b[i, 0], ln_g[i, 1], ln_b[i, 1]
        post_c = None
        if m == 0:
            post_x = _pool_layer(x, mod_x, pool_w[j], pool_scale[j], lng0, lnb0, router_p, n_exp)
            if update_ctx:
                post_c = _pool_layer(ctx, mod_c, pool_w[j], pool_scale[j], lng0, lnb0, router_p,
                                     n_exp)
        elif m == 1:
            wqkv = na_wqkv[j].astype(BF16)
            wo = na_wo[j].astype(BF16)
            qkv = _proj_in(x, mod_x, wqkv, d)
            qkv_c = _proj_in(ctx, mod_c, wqkv, d)
            o = _na_attention(qkv, qkv_c, _na_bias_table(na_rpb[j]))
            post_x = _proj_out(o, x, mod_x, wo, lng0, lnb0, router_p, n_exp)
            if update_ctx:
                oc = _ctx_attention(qkv_c)
                post_c = _proj_out(oc, ctx, mod_c, wo, lng0, lnb0, router_p, n_exp)
        else:
            n_q_heads = gqa_wo.shape[1] // HEAD_DIM
            dq = n_q_heads * HEAD_DIM
            dkv = GQA_KV_HEADS * HEAD_DIM
            q_cols, k_cols, o_rows, lane_dim = _gqa_layout(n_q_heads)
            v_cols = dq + dkv + np.arange(dkv)
            w_perm = gqa_wqkv[j][:, np.concatenate([q_cols, k_cols, v_cols])].astype(BF16)
            wo = gqa_wo[j][o_rows].astype(BF16)
            gain_row = jnp.concatenate(
                [jnp.tile(gqa_q_norm[j][lane_dim] * (HEAD_DIM ** -0.5 * LOG2_E), dq // LANES),
                 jnp.tile(gqa_k_norm[j][lane_dim], dkv // LANES),
                 jnp.ones((dkv,), F32)]).reshape(1, -1)
            lanes2 = np.arange(2 * LANES)
            lane_head = 2 * (lanes2 // LANES) + (lanes2 // (HEAD_DIM // 2)) % 2
            seg = jnp.asarray((lane_head[:, None] == lane_head[None, :]) / HEAD_DIM, BF16)
            cos_t, sin_t = _rope_tables(n)
            n_norm = (dq + dkv) // LANES
            qkv = _gqa_proj(x, mod_x, w_perm, gain_row, seg, cos_t, sin_t, n_norm)
            qkv_c = _gqa_proj(ctx, mod_c, w_perm, gain_row, seg, jnp.ones((l, LANES), F32),
                              jnp.zeros((l, LANES), F32), n_norm)
            kcat = jnp.concatenate([qkv[:, :, dq:dq + dkv], qkv_c[:, :, dq:dq + dkv]], axis=1)
            vcat = jnp.concatenate([qkv[:, :, dq + dkv:], qkv_c[:, :, dq + dkv:]], axis=1)
            q_gain_max = jnp.max(jnp.abs(gqa_q_norm[j])) * (HEAD_DIM ** -0.5 * LOG2_E)
            score_bound = jnp.full((1, LANES), 1.02 * HEAD_DIM, F32) * (
                q_gain_max * jnp.max(jnp.abs(gqa_k_norm[j])))
            o = _gqa_attention(qkv[:, :, :dq], kcat, vcat, score_bound)
            post_x = _proj_out(o, x, mod_x, wo, lng0, lnb0, router_p, n_exp)
            if update_ctx:
                raise NotImplementedError("context update after a GQA layer is not part of this stack")
        x, ctx_new = _moe(post_x, post_c if update_ctx else None, mod_x, mod_c, lng1, lnb1,
                          moe_w1, moe_w3, moe_w2, i)
        if update_ctx:
            ctx = ctx_new
    return x
```

```python
import functools
import math

import jax
import jax.numpy as jnp
import numpy as np
from jax import lax
from jax.experimental import pallas as pl
from jax.experimental.pallas import tpu as pltpu

F32 = jnp.float32
BF16 = jnp.bfloat16
I32 = jnp.int32
HIGHEST = lax.Precision.HIGHEST

DEPTH = 4
N_MIXERS = 3
GRID_W = 64
HEAD_DIM = 64
POOL_WINDOWS = (2, 4, 8, 16)
POOL_HALO = 8
NA_KH = 8
NA_KW = 16
NA_QROWS = 4
NA_KBLOCKS = 3
GQA_KV_HEADS = 4
ROPE_THETA = 10000.0
EC_CAPACITY = 2
N_MOD = 6
LN_EPS = 1e-5
RMS_EPS = 1e-6
ALPHA = (2.0 * DEPTH) ** 0.25
LANES = 128
ROUTE_T = 256
ROUTE_K = 64
COMBINE_T = 256
COMBINE_WIN = 80
ROW_ALIGN = 16
GATHER_ROWS = ROUTE_K + ROW_ALIGN
GATHER_CHUNK = 2
FFN_BLOCK = 1024
FFN_SUB = 256
VMEM_LIMIT = 56 * 1024 * 1024
NEG_INF = -1e30
LOG2_E = math.log2(math.e)
EXP2_HEADROOM = 64.0


def _cparams(sem, vmem=VMEM_LIMIT):
    return pltpu.CompilerParams(dimension_semantics=sem, vmem_limit_bytes=vmem)


def _silu(v):
    return v / (1.0 + jnp.exp(-v))


def _mod_kernel(c_ref, w_ref, b_ref, o_ref):
    s = _silu(c_ref[...])
    o_ref[0] = jnp.dot(s, w_ref[0], precision=HIGHEST, preferred_element_type=F32) + b_ref[0]


def _modulation(cc, mod_w, mod_b):
    depth, d, nd = mod_w.shape
    tn = nd // 4
    return pl.pallas_call(
        _mod_kernel,
        grid=(depth, nd // tn),
        in_specs=[
            pl.BlockSpec((8, d), lambda i, j: (0, 0)),
            pl.BlockSpec((1, d, tn), lambda i, j: (i, 0, j)),
            pl.BlockSpec((1, 1, tn), lambda i, j: (i, 0, j)),
        ],
        out_specs=pl.BlockSpec((1, 8, tn), lambda i, j: (i, 0, j)),
        out_shape=jax.ShapeDtypeStruct((depth, 8, nd), F32),
        compiler_params=_cparams(("parallel", "parallel")),
    )(cc, mod_w, mod_b.reshape(depth, 1, nd))


def _layer_norm(z, g, b):
    mu = jnp.mean(z, axis=-1, keepdims=True)
    zc = z - mu
    var = jnp.mean(zc * zc, axis=-1, keepdims=True)
    return zc * lax.rsqrt(var + LN_EPS) * g + b


def _post_mixer(x, y, mod, lng, lnb, router, n_exp):
    x1 = _layer_norm(ALPHA * x + mod[2:3] * y, lng, lnb)
    hf = x1 * (1.0 + mod[4:5]) + mod[3:4]
    hf_hi, hf_lo = _split_bf16(hf)
    logits = (jnp.dot(hf_hi, router[0], preferred_element_type=F32)
              + jnp.dot(hf_lo, router[0], preferred_element_type=F32)
              + jnp.dot(hf_hi, router[1], preferred_element_type=F32))
    lt = logits.T[:n_exp]
    m = jnp.max(lt, axis=0, keepdims=True)
    p = jnp.exp(lt - m)
    aff = p / jnp.sum(p, axis=0, keepdims=True)
    return x1, hf_hi, aff


def _post_outs(b, n, d, n_exp):
    return (jax.ShapeDtypeStruct((b, n, d), F32),
            jax.ShapeDtypeStruct((b, n, d), BF16),
            jax.ShapeDtypeStruct((b, n_exp, n), F32))


def _post_out_specs(tm, d, n_exp):
    return (pl.BlockSpec((1, tm, d), lambda b, t: (b, t, 0)),
            pl.BlockSpec((1, tm, d), lambda b, t: (b, t, 0)),
            pl.BlockSpec((1, n_exp, tm), lambda b, t: (b, 0, t)))


def _pad_router(router):
    d, e = router.shape
    r = jnp.zeros((d, LANES), F32).at[:, :e].set(router)
    hi = lax.bitcast_convert_type(
        lax.bitcast_convert_type(r, jnp.uint32) & jnp.uint32(0xFFFF0000), F32)
    return jnp.stack([hi.astype(BF16), (r - hi).astype(BF16)])


def _pool_kernel(x_ref, xp_ref, xn_ref, mod_ref, pw_ref, ps_ref, lng_ref, lnb_ref, r_ref,
                 x1_ref, hf_ref, aff_ref, buf, *, n, tm, n_exp):
    t = pl.program_id(1)
    nt = pl.num_programs(1)
    mod = mod_ref[0]
    x = x_ref[0]
    sc = 1.0 + mod[1:2]
    sh = mod[0:1]
    h = x * sc + sh
    hp = xp_ref[0] * sc + sh
    hn = xn_ref[0] * sc + sh
    buf[0:POOL_HALO, :] = jnp.where(t > 0, hp, 0.0)
    buf[POOL_HALO:POOL_HALO + tm, :] = h
    buf[POOL_HALO + tm:, :] = jnp.where(t < nt - 1, hn, 0.0)
    pos = t * tm + lax.broadcasted_iota(I32, (tm, 1), 0)
    ch = x.shape[1] // len(POOL_WINDOWS)
    parts = []
    for g, w in enumerate(POOL_WINDOWS):
        cols = slice(g * ch, (g + 1) * ch)
        acc = None
        for o in range(-(w // 2), w - w // 2):
            v = buf[POOL_HALO + o:POOL_HALO + o + tm, cols]
            acc = v if acc is None else acc + v
        lo = jnp.maximum(pos - w // 2, 0)
        hi = jnp.minimum(pos + (w - w // 2 - 1), n - 1)
        cnt = (hi - lo + 1).astype(F32)
        dlt = acc / cnt - h[:, cols]
        parts.append(jnp.dot(dlt.astype(BF16), pw_ref[g], preferred_element_type=F32))
    y = jnp.concatenate(parts, axis=1) * ps_ref[...]
    x1, hf, aff = _post_mixer(x, y, mod, lng_ref[...], lnb_ref[...], r_ref[...], n_exp)
    x1_ref[0] = x1
    hf_ref[0] = hf
    aff_ref[0] = aff


def _pool_layer(x, mod, pool_w, pool_scale, lng, lnb, router_p, n_exp):
    b, n, d = x.shape
    tm = min(256, n)
    hb = tm // POOL_HALO
    nhb = n // POOL_HALO
    g, ch, _ = pool_w.shape
    kern = functools.partial(_pool_kernel, n=n, tm=tm, n_exp=n_exp)
    return pl.pallas_call(
        kern,
        grid=(b, n // tm),
        in_specs=[
            pl.BlockSpec((1, tm, d), lambda i, t: (i, t, 0)),
            pl.BlockSpec((1, POOL_HALO, d), lambda i, t: (i, jnp.maximum(t * hb - 1, 0), 0)),
            pl.BlockSpec((1, POOL_HALO, d), lambda i, t: (i, jnp.minimum((t + 1) * hb, nhb - 1), 0)),
            pl.BlockSpec((1, 8, d), lambda i, t: (i, 0, 0)),
            pl.BlockSpec((g, ch, ch), lambda i, t: (0, 0, 0)),
            pl.BlockSpec((1, d), lambda i, t: (0, 0)),
            pl.BlockSpec((1, d), lambda i, t: (0, 0)),
            pl.BlockSpec((1, d), lambda i, t: (0, 0)),
            pl.BlockSpec((2, d, LANES), lambda i, t: (0, 0, 0)),
        ],
        out_specs=_post_out_specs(tm, d, n_exp),
        out_shape=_post_outs(b, n, d, n_exp),
        scratch_shapes=[pltpu.VMEM((tm + 2 * POOL_HALO, d), F32)],
        compiler_params=_cparams(("parallel", "parallel")),
    )(x, x, x, mod, pool_w.astype(BF16), pool_scale.reshape(1, d), lng.reshape(1, d),
      lnb.reshape(1, d), router_p)


def _proj_in_kernel(x_ref, mod_ref, w_ref, o_ref, *, q_cols, tn):
    mod = mod_ref[0]
    h = (x_ref[0] * (1.0 + mod[1:2]) + mod[0:1]).astype(BF16)
    ncol = w_ref.shape[1]
    for j in range(ncol // tn):
        y = jnp.dot(h, w_ref[:, j * tn:(j + 1) * tn], preferred_element_type=F32)
        if (j + 1) * tn <= q_cols:
            y = y * (HEAD_DIM ** -0.5 * LOG2_E)
        o_ref[0, :, j * tn:(j + 1) * tn] = y.astype(o_ref.dtype)


def _proj_in(x, mod, w_bf16, q_cols):
    b, n, d = x.shape
    ncol = w_bf16.shape[1]
    tm = min(512, n)
    tn = 512
    kern = functools.partial(_proj_in_kernel, q_cols=q_cols, tn=tn)
    return pl.pallas_call(
        kern,
        grid=(b, n // tm),
        in_specs=[
            pl.BlockSpec((1, tm, d), lambda i, t: (i, t, 0)),
            pl.BlockSpec((1, 8, d), lambda i, t: (i, 0, 0)),
            pl.BlockSpec((d, ncol), lambda i, t: (0, 0)),
        ],
        out_specs=pl.BlockSpec((1, tm, ncol), lambda i, t: (i, t, 0)),
        out_shape=jax.ShapeDtypeStruct((b, n, ncol), BF16),
        compiler_params=_cparams(("parallel", "parallel")),
    )(x, mod, w_bf16)


def _gqa_proj_kernel(x_ref, mod_ref, w_ref, gain_ref, seg_ref, cos_ref, sin_ref, o_ref, *, n_norm):
    mod = mod_ref[0]
    h = (x_ref[0] * (1.0 + mod[1:2]) + mod[0:1]).astype(BF16)
    seg = seg_ref[...]
    cos = cos_ref[...]
    sin = sin_ref[...]
    ncol = w_ref.shape[1]
    wide = 2 * LANES
    y_all = jnp.dot(h, w_ref[...], preferred_element_type=F32)
    for j in range(ncol // wide):
        cols = slice(j * wide, (j + 1) * wide)
        y = y_all[:, cols]
        if 2 * j < n_norm:
            sq_hi, sq_lo = _split_bf16(y * y)
            ms = (jnp.dot(sq_hi, seg, preferred_element_type=F32)
                  + jnp.dot(sq_lo, seg, preferred_element_type=F32))
            yn = y * lax.rsqrt(ms + RMS_EPS) * gain_ref[:, cols]
            halves = []
            for k in range(2):
                part = yn[:, k * LANES:(k + 1) * LANES]
                halves.append(part * cos + pltpu.roll(part, LANES // 2, axis=1) * sin)
            y = jnp.concatenate(halves, axis=1)
        o_ref[0, :, cols] = y.astype(o_ref.dtype)


def _gqa_proj(x, mod, w_bf16, gain_row, seg, cos_t, sin_t, n_norm):
    b, n, d = x.shape
    ncol = w_bf16.shape[1]
    tm = min(512, n)
    kern = functools.partial(_gqa_proj_kernel, n_norm=n_norm)
    return pl.pallas_call(
        kern,
        grid=(b, n // tm),
        in_specs=[
            pl.BlockSpec((1, tm, d), lambda i, t: (i, t, 0)),
            pl.BlockSpec((1, 8, d), lambda i, t: (i, 0, 0)),
            pl.BlockSpec((d, ncol), lambda i, t: (0, 0)),
            pl.BlockSpec((1, ncol), lambda i, t: (0, 0)),
            pl.BlockSpec((2 * LANES, 2 * LANES), lambda i, t: (0, 0)),
            pl.BlockSpec((tm, LANES), lambda i, t: (t, 0)),
            pl.BlockSpec((tm, LANES), lambda i, t: (t, 0)),
        ],
        out_specs=pl.BlockSpec((1, tm, ncol), lambda i, t: (i, t, 0)),
        out_shape=jax.ShapeDtypeStruct((b, n, ncol), BF16),
        compiler_params=_cparams(("parallel", "parallel")),
    )(x, mod, w_bf16, gain_row, seg, cos_t, sin_t)


def _proj_out_kernel(o_ref, x_ref, mod_ref, w_ref, lng_ref, lnb_ref, r_ref,
                     x1_ref, hf_ref, aff_ref, *, n_exp):
    y = jnp.dot(o_ref[0], w_ref[...], preferred_element_type=F32)
    x1, hf, aff = _post_mixer(x_ref[0], y, mod_ref[0], lng_ref[...], lnb_ref[...], r_ref[...], n_exp)
    x1_ref[0] = x1
    hf_ref[0] = hf
    aff_ref[0] = aff


def _proj_out(o, x, mod, w_bf16, lng, lnb, router_p, n_exp):
    b, n, d = x.shape
    tm = min(256, n)
    kern = functools.partial(_proj_out_kernel, n_exp=n_exp)
    return pl.pallas_call(
        kern,
        grid=(b, n // tm),
        in_specs=[
            pl.BlockSpec((1, tm, o.shape[2]), lambda i, t: (i, t, 0)),
            pl.BlockSpec((1, tm, d), lambda i, t: (i, t, 0)),
            pl.BlockSpec((1, 8, d), lambda i, t: (i, 0, 0)),
            pl.BlockSpec(w_bf16.shape, lambda i, t: (0, 0)),
            pl.BlockSpec((1, d), lambda i, t: (0, 0)),
            pl.BlockSpec((1, d), lambda i, t: (0, 0)),
            pl.BlockSpec((2, d, LANES), lambda i, t: (0, 0, 0)),
        ],
        out_specs=_post_out_specs(tm, d, n_exp),
        out_shape=_post_outs(b, n, d, n_exp),
        compiler_params=_cparams(("parallel", "parallel")),
    )(o, x, mod, w_bf16, lng.reshape(1, d), lnb.reshape(1, d), router_p)


def _lane_tiles(blocks):
    return [s[:, i * LANES:(i + 1) * LANES] for s in blocks for i in range(s.shape[1] // LANES)]


def _softmax_pv(scores, values):
    m = jnp.max(functools.reduce(jnp.maximum, _lane_tiles(scores)), axis=1, keepdims=True)
    ps = [jnp.exp2(s - m) for s in scores]
    l = jnp.sum(functools.reduce(lambda a, c: a + c, _lane_tiles(ps)), axis=1, keepdims=True)
    o = None
    for p, v in zip(ps, values):
        c = jnp.dot(p.astype(BF16), v, preferred_element_type=F32)
        o = c if o is None else o + c
    return o / l


def _na_kernel(q_ref, ka_ref, kb_ref, kc_ref, va_ref, vb_ref, vc_ref, kx_ref, vx_ref, bias_ref,
               o_ref, *, rows):
    i = pl.program_id(1)
    nb = pl.num_programs(1)
    tq = q_ref.shape[1]
    rq0 = i * NA_QROWS
    ks = jnp.clip(i - 1, 0, nb - NA_KBLOCKS) * NA_QROWS
    qi = lax.broadcasted_iota(I32, (tq, tq), 0)
    ki = lax.broadcasted_iota(I32, (tq, tq), 1)
    q_row = rq0 + qi // GRID_W
    q_col = qi % GRID_W
    k_col = ki % GRID_W
    r0 = jnp.clip(q_row - NA_KH // 2, 0, rows - NA_KH)
    c0 = jnp.clip(q_col - NA_KW // 2, 0, GRID_W - NA_KW)
    col_ok = (k_col >= c0) & (k_col < c0 + NA_KW)
    mask_add = []
    for blk in range(NA_KBLOCKS):
        k_row = ks + blk * NA_QROWS + ki // GRID_W
        ok = col_ok & (k_row >= r0) & (k_row < r0 + NA_KH)
        mask_add.append(jnp.where(ok, 0.0, NEG_INF))
    lane = lax.broadcasted_iota(I32, (1, LANES), 1)
    k_refs = (ka_ref, kb_ref, kc_ref)
    v_refs = (va_ref, vb_ref, vc_ref)
    n_pairs = q_ref.shape[2] // LANES
    for p in range(n_pairs):
        cols = slice(p * LANES, (p + 1) * LANES)
        q2 = q_ref[0, :, cols]
        kt = [r[0, :, cols] for r in k_refs] + [kx_ref[0, :, cols]]
        vt = [r[0, :, cols] for r in v_refs] + [vx_ref[0, :, cols]]
        qm = jnp.concatenate(
            [jnp.where((lane // HEAD_DIM) == hh, q2, jnp.zeros_like(q2)) for hh in range(2)], axis=0)
        scores = []
        for blk in range(NA_KBLOCKS):
            s = lax.dot_general(qm, kt[blk], (((1,), (1,)), ((), ())), preferred_element_type=F32)
            bands = []
            for hh in range(2):
                for qr in range(NA_QROWS):
                    halves = []
                    for kp in range(NA_QROWS // 2):
                        dr = ks + blk * NA_QROWS + 2 * kp - (rq0 + qr)
                        idx = jnp.clip(dr + NA_KH, 0, 2 * NA_KH - 1)
                        halves.append(bias_ref[2 * p + hh, idx] + mask_add[blk][
                            qr * GRID_W:(qr + 1) * GRID_W, kp * LANES:(kp + 1) * LANES])
                    bands.append(jnp.concatenate(halves, axis=1))
            scores.append(s + jnp.concatenate(bands, axis=0))
        scores.append(lax.dot_general(qm, kt[NA_KBLOCKS], (((1,), (1,)), ((), ())),
                                      preferred_element_type=F32))
        out = _softmax_pv(scores, vt)
        o_ref[0, :, cols] = jnp.where((lane // HEAD_DIM) == 0, out[:tq], out[tq:]).astype(o_ref.dtype)


def _na_bias_table(rpb):
    col = np.arange(GRID_W)
    dc = np.clip(col[None, :] - col[:, None] + (NA_KW - 1), 0, 2 * NA_KW - 2)
    t = rpb[:, :, dc]
    t_first = jnp.concatenate([t[:, :1], t], axis=1)
    t_next = jnp.concatenate([t, t[:, -1:]], axis=1)
    return jnp.concatenate([t_first, t_next], axis=-1).astype(F32) * LOG2_E


def _na_attention(qkv, qkv_c, bias_tab):
    b, n, d3 = qkv.shape
    d = d3 // 3
    l = qkv_c.shape[1]
    rows = n // GRID_W
    tq = NA_QROWS * GRID_W
    nb = n // tq
    heads = d // HEAD_DIM

    def kmap(off, col):
        return lambda i, t: (i, jnp.clip(t - 1, 0, nb - NA_KBLOCKS) + off, col)

    kern = functools.partial(_na_kernel, rows=rows)
    return pl.pallas_call(
        kern,
        grid=(b, nb),
        in_specs=[
            pl.BlockSpec((1, tq, d), lambda i, t: (i, t, 0)),
            pl.BlockSpec((1, tq, d), kmap(0, 1)),
            pl.BlockSpec((1, tq, d), kmap(1, 1)),
            pl.BlockSpec((1, tq, d), kmap(2, 1)),
            pl.BlockSpec((1, tq, d), kmap(0, 2)),
            pl.BlockSpec((1, tq, d), kmap(1, 2)),
            pl.BlockSpec((1, tq, d), kmap(2, 2)),
            pl.BlockSpec((1, l, d), lambda i, t: (i, 0, 1)),
            pl.BlockSpec((1, l, d), lambda i, t: (i, 0, 2)),
            pl.BlockSpec((heads, 2 * NA_KH, GRID_W, 2 * GRID_W), lambda i, t: (0, 0, 0, 0)),
        ],
        out_specs=pl.BlockSpec((1, tq, d), lambda i, t: (i, t, 0)),
        out_shape=jax.ShapeDtypeStruct((b, n, d), BF16),
        compiler_params=_cparams(("parallel", "parallel")),
    )(qkv, qkv, qkv, qkv, qkv, qkv, qkv, qkv_c, qkv_c, bias_tab)


def _ctx_attn_kernel(q_ref, k_ref, v_ref, o_ref):
    lane = lax.broadcasted_iota(I32, (1, LANES), 1)
    for p in range(q_ref.shape[2] // LANES):
        cols = slice(p * LANES, (p + 1) * LANES)
        q2 = q_ref[0, :, cols]
        k2 = k_ref[0, :, cols]
        v2 = v_ref[0, :, cols]
        outs = []
        for hh in range(2):
            qm = jnp.where((lane // HEAD_DIM) == hh, q2, jnp.zeros_like(q2))
            s = lax.dot_general(qm, k2, (((1,), (1,)), ((), ())), preferred_element_type=F32)
            outs.append(_softmax_pv([s], [v2]))
        o_ref[0, :, cols] = jnp.where((lane // HEAD_DIM) == 0, outs[0], outs[1]).astype(o_ref.dtype)


def _ctx_attention(qkv_c):
    b, l, d3 = qkv_c.shape
    d = d3 // 3
    return pl.pallas_call(
        _ctx_attn_kernel,
        grid=(b,),
        in_specs=[pl.BlockSpec((1, l, d), lambda i: (i, 0, 0)),
                  pl.BlockSpec((1, l, d), lambda i: (i, 0, 1)),
                  pl.BlockSpec((1, l, d), lambda i: (i, 0, 2))],
        out_specs=pl.BlockSpec((1, l, d), lambda i: (i, 0, 0)),
        out_shape=jax.ShapeDtypeStruct((b, l, d), BF16),
        compiler_params=_cparams(("parallel",)),
    )(qkv_c, qkv_c, qkv_c)


def _gqa_kernel(q_ref, k_ref, v_ref, kc_ref, vc_ref, bound_ref, o_ref, qm_sc, m_sc, l_sc, acc_sc, *,
                tk):
    nk = k_ref.shape[1] // tk
    tq = q_ref.shape[1]
    n_sl = q_ref.shape[2] // LANES
    lane = lax.broadcasted_iota(I32, (1, LANES), 1)
    for sl in range(n_sl):
        q2 = q_ref[0, :, sl * LANES:(sl + 1) * LANES]
        for hh in range(2):
            h = 2 * sl + hh
            qm_sc[h * tq:(h + 1) * tq, :] = jnp.where(((lane // (HEAD_DIM // 2)) % 2) == hh, q2,
                                                      jnp.zeros_like(q2))
    l_sc[...] = jnp.zeros_like(l_sc)
    acc_sc[...] = jnp.zeros_like(acc_sc)

    def scores(kt):
        return lax.dot_general(qm_sc[...], kt, (((1,), (1,)), ((), ())),
                               preferred_element_type=F32)

    def lane_partial_sum(p):
        return functools.reduce(lambda u, w: u + w, _lane_tiles([p]))

    def over_keys(update):
        def body(j, c):
            k0 = pl.multiple_of(j * tk, tk)
            update(k_ref[0, pl.ds(k0, tk), :], v_ref[0, pl.ds(k0, tk), :])
            return c

        lax.fori_loop(0, nk, body, 0, unroll=2 if nk % 2 == 0 else 1)
        update(kc_ref[0], vc_ref[0])

    bound = bound_ref[0:1, 0:1]
    reference = bound - EXP2_HEADROOM
    safe = jnp.max(bound) <= EXP2_HEADROOM

    @pl.when(safe)
    def _():
        def update(kt, vt):
            p = jnp.exp2(scores(kt) - reference)
            l_sc[...] += lane_partial_sum(p)
            acc_sc[...] += jnp.dot(p.astype(BF16), vt, preferred_element_type=F32)

        over_keys(update)

    @pl.when(jnp.logical_not(safe))
    def _():
        m_sc[...] = jnp.full_like(m_sc, NEG_INF)

        def update(kt, vt):
            s = scores(kt)
            m_old = m_sc[...]
            m_new = jnp.maximum(m_old, jnp.max(s, axis=1, keepdims=True))
            a = jnp.exp2(m_old - m_new)
            p = jnp.exp2(s - m_new)
            l_sc[...] = a * l_sc[...] + lane_partial_sum(p)
            acc_sc[...] = a * acc_sc[...] + jnp.dot(p.astype(BF16), vt, preferred_element_type=F32)
            m_sc[...] = m_new

        over_keys(update)

    out = acc_sc[...] / jnp.sum(l_sc[...], axis=1, keepdims=True)
    for sl in range(n_sl):
        o0 = out[2 * sl * tq:(2 * sl + 1) * tq]
        o1 = out[(2 * sl + 1) * tq:(2 * sl + 2) * tq]
        o_ref[0, :, sl * LANES:(sl + 1) * LANES] = jnp.where(
            (lane // HEAD_DIM) == 0, o0, o1).astype(o_ref.dtype)


def _gqa_attention(qkv, qkv_c, dq, dkv, score_bound):
    b, n, _ = qkv.shape
    l = qkv_c.shape[1]
    n_slab = dkv // LANES
    q_per = dq // n_slab
    n_heads = 2 * q_per // LANES
    tq = min(256, n)
    tk = min(1024, n)
    k_blk = dq // LANES
    v_blk = (dq + dkv) // LANES
    kern = functools.partial(_gqa_kernel, tk=tk)
    return pl.pallas_call(
        kern,
        grid=(b, n_slab, n // tq),
        in_specs=[
            pl.BlockSpec((1, tq, q_per), lambda i, p, t: (i, t, p)),
            pl.BlockSpec((1, n, LANES), lambda i, p, t: (i, 0, k_blk + p)),
            pl.BlockSpec((1, n, LANES), lambda i, p, t: (i, 0, v_blk + p)),
            pl.BlockSpec((1, l, LANES), lambda i, p, t: (i, 0, k_blk + p)),
            pl.BlockSpec((1, l, LANES), lambda i, p, t: (i, 0, v_blk + p)),
            pl.BlockSpec((1, LANES), lambda i, p, t: (0, 0)),
        ],
        out_specs=pl.BlockSpec((1, tq, q_per), lambda i, p, t: (i, t, p)),
        out_shape=jax.ShapeDtypeStruct((b, n, dq), BF16),
        scratch_shapes=[pltpu.VMEM((n_heads * tq, LANES), BF16),
                        pltpu.VMEM((n_heads * tq, 1), F32),
                        pltpu.VMEM((n_heads * tq, LANES), F32),
                        pltpu.VMEM((n_heads * tq, LANES), F32)],
        compiler_params=_cparams(("parallel", "parallel", "parallel")),
    )(qkv, qkv, qkv, qkv_c, qkv_c, score_bound)


def _prefix_count(mask_f32, tri, out_cb):
    e, n = mask_f32.shape
    c = tri.shape[0]
    carry = jnp.zeros((e, 1), F32)
    for j in range(n // c):
        chunk = mask_f32[:, j * c:(j + 1) * c]
        inc = jnp.dot(chunk.astype(BF16), tri, preferred_element_type=F32) + carry
        out_cb(j, c, chunk, inc)
        carry = inc[:, c - 1:c]


def _topk_kernel(aff_ref, rank_ref, cum_ref, eqx_ref, *, cap):
    a = aff_ref[0]
    e, n = a.shape
    bits = lax.bitcast_convert_type(a, I32)

    def body(i, thr):
        cand = thr | (jnp.int32(1) << (30 - i))
        cnt = jnp.sum(jnp.where(bits >= cand, 1.0, 0.0), axis=1, keepdims=True)
        return jnp.where(cnt >= cap, cand, thr)

    thr = lax.fori_loop(0, 31, body, jnp.zeros((e, 1), I32))
    gt = bits > thr
    eq = bits == thr
    need = cap - jnp.sum(jnp.where(gt, 1.0, 0.0), axis=1, keepdims=True)
    c = min(256, n)
    tri = jnp.where(lax.broadcasted_iota(I32, (c, c), 0) <= lax.broadcasted_iota(I32, (c, c), 1),
                    1.0, 0.0).astype(BF16)

    def eq_cb(j, c, chunk, inc):
        eqx_ref[:, j * c:(j + 1) * c] = inc - chunk

    _prefix_count(jnp.where(eq, 1.0, 0.0), tri, eq_cb)
    sel = gt | (eq & (eqx_ref[...] < need))

    def sel_cb(j, c, chunk, inc):
        excl = (inc - chunk).astype(I32)
        cum_ref[0, :, j * c:(j + 1) * c] = excl
        rank_ref[0, :, j * c:(j + 1) * c] = jnp.where(chunk > 0.5, excl, -1)

    _prefix_count(jnp.where(sel, 1.0, 0.0), tri, sel_cb)


def _topk(aff, cap):
    b, e, n = aff.shape
    kern = functools.partial(_topk_kernel, cap=cap)
    return pl.pallas_call(
        kern,
        grid=(b,),
        in_specs=[pl.BlockSpec((1, e, n), lambda i: (i, 0, 0))],
        out_specs=(pl.BlockSpec((1, e, n), lambda i: (i, 0, 0)),
                   pl.BlockSpec((1, e, n), lambda i: (i, 0, 0))),
        out_shape=(jax.ShapeDtypeStruct((b, e, n), I32), jax.ShapeDtypeStruct((b, e, n), I32)),
        scratch_shapes=[pltpu.VMEM((e, n), F32)],
        compiler_params=_cparams(("parallel",)),
    )(aff)


def _slot_onehot(rank, starts, lows, k_new, rows_per, n_exp, weights=None):
    t = rank.shape[1]
    j_iota = lax.broadcasted_iota(I32, (rows_per, t), 0)
    rows = []
    for e in range(n_exp):
        r = rank[e:e + 1, :]
        local = jnp.where((r >= lows[e]) & (r < lows[e] + k_new), r - starts[e], -1)
        value = 1.0 if weights is None else weights[e:e + 1, :]
        rows.append(jnp.where(local == j_iota, value, 0.0))
    return jnp.concatenate(rows, axis=0)


def _align_down(v, a):
    return pl.multiple_of((v // a) * a, a)


def _gather_kernel(base_ref, rank_ref, hf_ref, xs_hbm, stage, carry, sem, pend, *, n_exp, n_tiles,
                   cap):
    b = pl.program_id(0)
    t = pl.program_id(1)
    off = (b * (n_tiles + 1) + t) * n_exp
    base = [base_ref[off + e] for e in range(n_exp)]
    cnt = [base_ref[off + n_exp + e] - base[e] for e in range(n_exp)]
    kmax = functools.reduce(jnp.maximum, cnt)
    rank = rank_ref[0]
    hf = hf_ref[0]
    slot = (b * n_tiles + t) % 2

    def window_copy(e, start, sl):
        return pltpu.make_async_copy(stage.at[sl, e * GATHER_ROWS:(e + 1) * GATHER_ROWS, :],
                                     xs_hbm.at[e, b, pl.ds(start, GATHER_ROWS), :], sem.at[e])

    def drain():
        @pl.when(pend[0] == 1)
        def _():
            for e in range(n_exp):
                window_copy(e, 0, slot).wait()
            pend[0] = 0

    @pl.when((b == 0) & (t == 0))
    def _():
        pend[0] = 0

    @pl.when(t == 0)
    def _():
        drain()
        carry[...] = jnp.zeros_like(carry)
        stage[1 - slot] = jnp.zeros(stage.shape[1:], stage.dtype)
        for e in range(n_exp):
            window_copy(e, cap, 1 - slot).start()
        pend[0] = 1

    def write_round(r, guarded):
        lows = [base[e] + r * ROUTE_K for e in range(n_exp)]
        starts = [_align_down(lows[e], ROW_ALIGN) for e in range(n_exp)]
        onehot = _slot_onehot(rank, starts, lows, ROUTE_K, GATHER_ROWS, n_exp).astype(BF16)
        if guarded:
            drain()
        step = GATHER_CHUNK * GATHER_ROWS
        for c0 in range(0, n_exp * GATHER_ROWS, step):
            stage[slot, c0:c0 + step, :] = jnp.dot(onehot[c0:c0 + step], hf,
                                                   preferred_element_type=F32).astype(BF16)
        for e in range(n_exp):
            r0 = e * GATHER_ROWS
            stage[slot, r0:r0 + ROW_ALIGN, :] += carry[e * ROW_ALIGN:(e + 1) * ROW_ALIGN, :]
        if not guarded:
            drain()

        def move(e):
            window_copy(e, starts[e], slot).start()
            filled = lows[e] + jnp.clip(cnt[e] - r * ROUTE_K, 0, ROUTE_K)
            shift = _align_down(filled, ROW_ALIGN) - starts[e]
            src = pl.multiple_of(e * GATHER_ROWS + shift, ROW_ALIGN)
            carry[e * ROW_ALIGN:(e + 1) * ROW_ALIGN, :] = stage[slot, pl.ds(src, ROW_ALIGN), :]

        if not guarded:
            for e in range(n_exp):
                move(e)
            pend[0] = 1
            return
        for e in range(n_exp):
            @pl.when(cnt[e] > r * ROUTE_K)
            def _(e=e):
                move(e)
        for e in range(n_exp):
            @pl.when(cnt[e] > r * ROUTE_K)
            def _(e=e):
                window_copy(e, 0, slot).wait()

    write_round(0, False)

    def extra_round(r, c):
        write_round(r, True)
        return c

    lax.fori_loop(1, (kmax + ROUTE_K - 1) // ROUTE_K, extra_round, 0)

    @pl.when((b == pl.num_programs(0) - 1) & (t == n_tiles - 1))
    def _():
        drain()


def _gather(base_flat, rank, hf, cap):
    b, n, d = hf.shape
    n_exp = rank.shape[1]
    n_tiles = n // ROUTE_T
    kern = functools.partial(_gather_kernel, n_exp=n_exp, n_tiles=n_tiles, cap=cap)
    return pl.pallas_call(
        kern,
        grid_spec=pltpu.PrefetchScalarGridSpec(
            num_scalar_prefetch=1,
            grid=(b, n_tiles),
            in_specs=[
                pl.BlockSpec((1, n_exp, ROUTE_T), lambda i, t, s: (i, 0, t)),
                pl.BlockSpec((1, ROUTE_T, d), lambda i, t, s: (i, t, 0)),
            ],
            out_specs=pl.BlockSpec(memory_space=pl.ANY),
            scratch_shapes=[pltpu.VMEM((2, n_exp * GATHER_ROWS, d), BF16),
                            pltpu.VMEM((n_exp * ROW_ALIGN, d), BF16),
                            pltpu.SemaphoreType.DMA((n_exp,)),
                            pltpu.SMEM((1,), I32)],
        ),
        out_shape=jax.ShapeDtypeStruct((n_exp, b, cap + GATHER_ROWS, d), BF16),
        compiler_params=_cparams(("arbitrary", "arbitrary")),
    )(base_flat, rank, hf)


def _split_bf16(v):
    hi = v.astype(BF16)
    return hi, (v - hi.astype(F32)).astype(BF16)


def _ffn_kernel(*refs, n_f, n_b, with_ctx):
    if with_ctx:
        x_ref, xc_ref, w1_ref, w3_ref, w2_ref, y_ref, yc_ref, acc = refs
    else:
        x_ref, w1_ref, w3_ref, w2_ref, y_ref, acc = refs
    i = pl.program_id(1)
    f = pl.program_id(2)
    tf = w1_ref.shape[3]
    sub = min(FFN_SUB, tf)

    def run(x, emit):
        rows = x.shape[0]
        y = None
        for c in range(tf // sub):
            cs = slice(c * sub, (c + 1) * sub)
            h1 = jnp.dot(x, w1_ref[0, 0, :, cs].astype(BF16), preferred_element_type=F32)
            h3 = jnp.dot(x, w3_ref[0, 0, :, cs].astype(BF16), preferred_element_type=F32)
            hid = (_silu(h1) * h3).astype(BF16)
            part = jnp.dot(hid, w2_ref[0, 0, cs, :].astype(BF16), preferred_element_type=F32)
            y = part if y is None else y + part
        if n_f == 1:
            emit(y.astype(BF16))
            return

        @pl.when(f == 0)
        def _():
            acc[0:rows, :] = y

        @pl.when((f > 0) & (f < n_f - 1))
        def _():
            acc[0:rows, :] += y

        @pl.when(f == n_f - 1)
        def _():
            emit((acc[0:rows, :] + y).astype(BF16))

    def emit_main(out):
        y_ref[0, 0] = out

    def run_main():
        run(x_ref[0, 0], emit_main)

    if not with_ctx:
        run_main()
        return

    bc, cap_c, d = xc_ref.shape[1:]
    cap = x_ref.shape[2]

    def emit_both(out):
        y_ref[0, 0] = out[:cap]
        yc_ref[0] = out[cap:].reshape(bc, cap_c, d)

    pl.when(i < n_b - 1)(run_main)

    @pl.when(i == n_b - 1)
    def _():
        run(jnp.concatenate([x_ref[0, 0], xc_ref[0].reshape(bc * cap_c, d)], axis=0), emit_both)


def _ffn(xs, xs_c, w1, w3, w2, layer, cap, cap_c):
    n_exp, b, _, d = xs.shape
    ff = w1.shape[3]
    tf = min(FFN_BLOCK, ff)
    n_f = ff // tf
    with_ctx = xs_c is not None
    kern = functools.partial(_ffn_kernel, n_f=n_f, n_b=b, with_ctx=with_ctx)

    def main_map(e, i, f):
        return (e, i, 0, 0)

    x_specs = [pl.BlockSpec((1, 1, cap, d), main_map)]
    out_specs = [pl.BlockSpec((1, 1, cap, d), main_map)]
    out_shape = [jax.ShapeDtypeStruct((n_exp, b, cap, d), BF16)]
    operands = [xs]
    acc_rows = cap
    if with_ctx:
        c_spec = pl.BlockSpec((1, b, cap_c, d), lambda e, i, f: (e, 0, 0, 0))
        x_specs.append(c_spec)
        out_specs.append(c_spec)
        out_shape.append(jax.ShapeDtypeStruct((n_exp, b, cap_c, d), BF16))
        operands.append(xs_c)
        acc_rows += b * cap_c
    return pl.pallas_call(
        kern,
        grid=(n_exp, b, n_f),
        in_specs=x_specs + [
            pl.BlockSpec((1, 1, d, tf), lambda e, i, f: (layer, e, 0, f)),
            pl.BlockSpec((1, 1, d, tf), lambda e, i, f: (layer, e, 0, f)),
            pl.BlockSpec((1, 1, tf, d), lambda e, i, f: (layer, e, f, 0)),
        ],
        out_specs=tuple(out_specs),
        out_shape=tuple(out_shape),
        scratch_shapes=[pltpu.VMEM((acc_rows, d), F32)],
        compiler_params=_cparams(("parallel", "arbitrary", "arbitrary")),
    )(*operands, w1, w3, w2)


def _combine_kernel(base_ref, rank_ref, aff_ref, x1_ref, mod_ref, lng_ref, lnb_ref, y_hbm,
                    out_ref, ybuf, sem, *, n_exp, n_tiles, cap, win):
    k_new = win - ROW_ALIGN
    b = pl.program_id(0)
    t = pl.program_id(1)
    g = b * n_tiles + t
    total = pl.num_programs(0) * n_tiles
    slot = g % 2
    tt = rank_ref.shape[2]

    def tile_info(step):
        off = ((step // n_tiles) * (n_tiles + 1) + step % n_tiles) * n_exp
        base = [base_ref[off + e] for e in range(n_exp)]
        cnt = [base_ref[off + n_exp + e] - base[e] for e in range(n_exp)]
        return base, cnt

    def window(base, r, e):
        low = base[e] + r * k_new
        start = pl.multiple_of(jnp.minimum(_align_down(low, ROW_ALIGN), cap - win), ROW_ALIGN)
        return low, start

    def copy(step, start, e, sl):
        return pltpu.make_async_copy(
            y_hbm.at[e, step // n_tiles, pl.ds(start, win), :],
            ybuf.at[sl, pl.ds(e * win, win), :], sem.at[sl, e])

    def issue(step, r, sl):
        base, cnt = tile_info(step)
        for e in range(n_exp):
            def go(e=e):
                copy(step, window(base, r, e)[1], e, sl).start()
            if isinstance(r, int) and r == 0:
                go()
            else:
                pl.when(cnt[e] > r * k_new)(go)

    def wait(step, r, sl):
        _, cnt = tile_info(step)
        for e in range(n_exp):
            def go(e=e):
                copy(step, 0, e, sl).wait()
            if isinstance(r, int) and r == 0:
                go()
            else:
                pl.when(cnt[e] > r * k_new)(go)

    @pl.when(g == 0)
    def _():
        ybuf[...] = jnp.zeros_like(ybuf)
        issue(g, 0, slot)

    @pl.when(g + 1 < total)
    def _():
        issue(g + 1, 0, 1 - slot)

    base, cnt = tile_info(g)
    kmax = functools.reduce(jnp.maximum, cnt)
    rank = rank_ref[0]
    aff = aff_ref[0]

    def scatter(r, f):
        lows, starts = zip(*[window(base, r, e) for e in range(n_exp)])
        gates = _slot_onehot(rank, starts, lows, k_new, win, n_exp, weights=aff)
        wait(g, r, slot)
        return f + jnp.dot(gates.T.astype(BF16), ybuf[slot], preferred_element_type=F32)

    def extra_round(r, f):
        issue(g, r, slot)
        return scatter(r, f)

    d = x1_ref.shape[2]
    f = scatter(0, jnp.zeros((tt, d), F32))
    f = lax.fori_loop(1, (kmax + k_new - 1) // k_new, extra_round, f)
    mod = mod_ref[0]
    out_ref[0] = _layer_norm(ALPHA * x1_ref[0] + mod[5:6] * f, lng_ref[...], lnb_ref[...])


def _combine(cum, rank, aff, x1, mod, lng, lnb, y, cap):
    b, n, d = x1.shape
    n_exp = rank.shape[1]
    n_tiles = n // COMBINE_T
    win = min(COMBINE_WIN, cap)
    assert cap % ROW_ALIGN == 0 and win > ROW_ALIGN
    kern = functools.partial(_combine_kernel, n_exp=n_exp, n_tiles=n_tiles, cap=cap, win=win)
    return pl.pallas_call(
        kern,
        grid_spec=pltpu.PrefetchScalarGridSpec(
            num_scalar_prefetch=1,
            grid=(b, n_tiles),
            in_specs=[
                pl.BlockSpec((1, n_exp, COMBINE_T), lambda i, t, s: (i, 0, t)),
                pl.BlockSpec((1, n_exp, COMBINE_T), lambda i, t, s: (i, 0, t)),
                pl.BlockSpec((1, COMBINE_T, d), lambda i, t, s: (i, t, 0)),
                pl.BlockSpec((1, 8, d), lambda i, t, s: (i, 0, 0)),
                pl.BlockSpec((1, d), lambda i, t, s: (0, 0)),
                pl.BlockSpec((1, d), lambda i, t, s: (0, 0)),
                pl.BlockSpec(memory_space=pl.ANY),
            ],
            out_specs=pl.BlockSpec((1, COMBINE_T, d), lambda i, t, s: (i, t, 0)),
            scratch_shapes=[pltpu.VMEM((2, n_exp * win, d), BF16),
                            pltpu.SemaphoreType.DMA((2, n_exp))],
        ),
        out_shape=jax.ShapeDtypeStruct((b, n, d), F32),
        compiler_params=_cparams(("arbitrary", "arbitrary")),
    )(_tile_bases(cum, COMBINE_T, cap), rank, aff, x1, mod, lng.reshape(1, d), lnb.reshape(1, d), y)


def _tile_bases(cum, tile, cap):
    b, n_exp, _ = cum.shape
    tile_base = jnp.swapaxes(cum[:, :, ::tile], 1, 2)
    return jnp.concatenate([tile_base, jnp.full((b, 1, n_exp), cap, I32)], axis=1).reshape(-1)


def _route(hf, aff):
    n = hf.shape[1]
    cap = EC_CAPACITY * n // aff.shape[1]
    rank, cum = _topk(aff, cap)
    return cum, rank, _gather(_tile_bases(cum, ROUTE_T, cap), rank, hf, cap), cap


def _moe(post_x, post_c, mod_x, mod_c, lng, lnb, w1, w3, w2, layer):
    x1, hf, aff = post_x
    cum, rank, xs, cap = _route(hf, aff)
    if post_c is None:
        (y,) = _ffn(xs, None, w1, w3, w2, layer, cap, None)
        return _combine(cum, rank, aff, x1, mod_x, lng, lnb, y, cap), None
    c1, hf_c, aff_c = post_c
    cum_c, rank_c, xs_c, cap_c = _route(hf_c, aff_c)
    y, y_c = _ffn(xs, xs_c, w1, w3, w2, layer, cap, cap_c)
    return (_combine(cum, rank, aff, x1, mod_x, lng, lnb, y, cap),
            _combine(cum_c, rank_c, aff_c, c1, mod_c, lng, lnb, y_c, cap_c))


def _gqa_layout(n_q_heads):
    group = n_q_heads // GQA_KV_HEADS
    half = HEAD_DIM // 2
    ev = np.arange(half) * 2
    od = ev + 1

    def slab(col_a, col_b):
        return np.concatenate([col_a + ev, col_b + ev, col_a + od, col_b + od])

    q_cols, k_cols, o_rows, gain_idx = [], [], [], []
    dq = n_q_heads * HEAD_DIM
    for p in range(GQA_KV_HEADS // 2):
        for i in range(group):
            a = (2 * p) * group + i
            c = (2 * p + 1) * group + i
            q_cols.append(slab(a * HEAD_DIM, c * HEAD_DIM))
            o_rows.append(np.concatenate([a * HEAD_DIM + np.arange(HEAD_DIM),
                                          c * HEAD_DIM + np.arange(HEAD_DIM)]))
    for p in range(GQA_KV_HEADS // 2):
        k_cols.append(dq + slab(2 * p * HEAD_DIM, (2 * p + 1) * HEAD_DIM))
    lane_dim = np.concatenate([ev, ev, od, od])
    return (np.concatenate(q_cols), np.concatenate(k_cols), np.concatenate(o_rows), lane_dim)


def _rope_tables(n):
    t = jnp.arange(n, dtype=I32)
    row = (t // GRID_W).astype(F32)
    col = (t % GRID_W).astype(F32)
    axis_dims = HEAD_DIM // 2
    inv_freq = jnp.power(ROPE_THETA, -jnp.arange(0, axis_dims, 2, dtype=F32) / axis_dims)
    ang = jnp.concatenate([row[:, None] * inv_freq, col[:, None] * inv_freq], axis=-1)
    cos, sin = jnp.cos(ang), jnp.sin(ang)
    cos_t = jnp.concatenate([cos] * 4, axis=1)
    sin_t = jnp.concatenate([-sin, -sin, sin, sin], axis=1)
    return cos_t, sin_t


def kernel(x, c, ctx, c_ctx, mod_w, mod_b, ln_g, ln_b, pool_w, pool_scale, na_wqkv, na_wo, na_rpb,
           gqa_wqkv, gqa_q_norm, gqa_k_norm, gqa_wo, moe_router, moe_w1, moe_w3, moe_w2):
    bsz, n, d = x.shape
    l = ctx.shape[1]
    n_exp = moe_router.shape[2]
    depth = mod_w.shape[0]
    cc = jnp.zeros((8, d), F32).at[:bsz].set(c).at[bsz].set(c_ctx)
    mod_all = _modulation(cc, mod_w, mod_b)

    for i in range(depth):
        m = i % N_MIXERS
        j = i // N_MIXERS
        update_ctx = any(k % N_MIXERS != 0 for k in range(i + 1, depth))
        ctx_keys = m != 0
        mod6 = mod_all[i].reshape(8, N_MOD, d)
        mod_x = jnp.zeros((bsz, 8, d), F32).at[:, :N_MOD].set(mod6[:bsz])
        mod_c = jnp.zeros((bsz, 8, d), F32).at[:, :N_MOD].set(
            jnp.broadcast_to(mod6[bsz][None], (bsz, N_MOD, d)))
        router_p = _pad_router(moe_router[i])
        lng0, lnb0, lng1, lnb1 = ln_g[i, 0], ln_b[i, 0], ln_g[i, 1], ln_b[i, 1]
        post_c = None
        if m == 0:
            post_x = _pool_layer(x, mod_x, pool_w[j], pool_scale[j], lng0, lnb0, router_p, n_exp)
            if update_ctx:
                post_c = _pool_layer(ctx, mod_c, pool_w[j], pool_scale[j], lng0, lnb0, router_p,
                                     n_exp)
        elif m == 1:
            wqkv = na_wqkv[j].astype(BF16)
            wo = na_wo[j].astype(BF16)
            qkv = _proj_in(x, mod_x, wqkv, d)
            qkv_c = _proj_in(ctx, mod_c, wqkv, d)
            o = _na_attention(qkv, qkv_c, _na_bias_table(na_rpb[j]))
            post_x = _proj_out(o, x, mod_x, wo, lng0, lnb0, router_p, n_exp)
            if update_ctx:
                oc = _ctx_attention(qkv_c)
                post_c = _proj_out(oc, ctx, mod_c, wo, lng0, lnb0, router_p, n_exp)
        else:
            n_q_heads = gqa_wo.shape[1] // HEAD_DIM
            dq = n_q_heads * HEAD_DIM
            dkv = GQA_KV_HEADS * HEAD_DIM
            q_cols, k_cols, o_rows, lane_dim = _gqa_layout(n_q_heads)
            v_cols = dq + dkv + np.arange(dkv)
            w_perm = gqa_wqkv[j][:, np.concatenate([q_cols, k_cols, v_cols])].astype(BF16)
            wo = gqa_wo[j][o_rows].astype(BF16)
            gain_row = jnp.concatenate(
                [jnp.tile(gqa_q_norm[j][lane_dim] * (HEAD_DIM ** -0.5 * LOG2_E), dq // LANES),
                 jnp.tile(gqa_k_norm[j][lane_dim], dkv // LANES),
                 jnp.ones((dkv,), F32)]).reshape(1, -1)
            lanes2 = np.arange(2 * LANES)
            lane_head = 2 * (lanes2 // LANES) + (lanes2 // (HEAD_DIM // 2)) % 2
            seg = jnp.asarray((lane_head[:, None] == lane_head[None, :]) / HEAD_DIM, BF16)
            cos_t, sin_t = _rope_tables(n)
            n_norm = (dq + dkv) // LANES
            qkv = _gqa_proj(x, mod_x, w_perm, gain_row, seg, cos_t, sin_t, n_norm)
            qkv_c = _gqa_proj(ctx, mod_c, w_perm, gain_row, seg, jnp.ones((l, LANES), F32),
                              jnp.zeros((l, LANES), F32), n_norm)
            q_gain_max = jnp.max(jnp.abs(gqa_q_norm[j])) * (HEAD_DIM ** -0.5 * LOG2_E)
            score_bound = jnp.full((1, LANES), 1.02 * HEAD_DIM, F32) * (
                q_gain_max * jnp.max(jnp.abs(gqa_k_norm[j])))
            o = _gqa_attention(qkv, qkv_c, dq, dkv, score_bound)
            post_x = _proj_out(o, x, mod_x, wo, lng0, lnb0, router_p, n_exp)
            if update_ctx:
                raise NotImplementedError("context update after a GQA layer is not part of this stack")
        x, ctx_new = _moe(post_x, post_c if update_ctx else None, mod_x, mod_c, lng1, lnb1,
                          moe_w1, moe_w3, moe_w2, i)
        if update_ctx:
            ctx = ctx_new
    return x
```

```python
import functools
import math

import jax
import jax.numpy as jnp
import numpy as np
from jax import lax
from jax.experimental import pallas as pl
from jax.experimental.pallas import tpu as pltpu

F32 = jnp.float32
BF16 = jnp.bfloat16
I32 = jnp.int32
HIGHEST = lax.Precision.HIGHEST

DEPTH = 4
N_MIXERS = 3
GRID_W = 64
HEAD_DIM = 64
POOL_WINDOWS = (2, 4, 8, 16)
POOL_HALO = 8
NA_KH = 8
NA_KW = 16
NA_QROWS = 4
NA_KBLOCKS = 3
GQA_KV_HEADS = 4
ROPE_THETA = 10000.0
EC_CAPACITY = 2
N_MOD = 6
LN_EPS = 1e-5
RMS_EPS = 1e-6
ALPHA = (2.0 * DEPTH) ** 0.25
LANES = 128
ROUTE_T = 256
GQA_UNROLL = 4
ROUTE_K = 48
COMBINE_T = 256
COMBINE_WIN = 64
ROW_ALIGN = 16
GATHER_ROWS = ROUTE_K + ROW_ALIGN
GATHER_CHUNK = 2
FFN_BLOCK = 1024
FFN_SUB = 256
VMEM_LIMIT = 56 * 1024 * 1024
NEG_INF = -1e30
LOG2_E = math.log2(math.e)
EXP2_HEADROOM = 64.0


def _cparams(sem, vmem=VMEM_LIMIT):
    return pltpu.CompilerParams(dimension_semantics=sem, vmem_limit_bytes=vmem)


def _silu(v):
    return v / (1.0 + jnp.exp(-v))


def _mod_kernel(c_ref, w_ref, b_ref, o_ref):
    s = _silu(c_ref[...])
    o_ref[0] = jnp.dot(s, w_ref[0], precision=HIGHEST, preferred_element_type=F32) + b_ref[0]


def _modulation(cc, mod_w, mod_b):
    depth, d, nd = mod_w.shape
    tn = nd // 4
    return pl.pallas_call(
        _mod_kernel,
        grid=(depth, nd // tn),
        in_specs=[
            pl.BlockSpec((8, d), lambda i, j: (0, 0)),
            pl.BlockSpec((1, d, tn), lambda i, j: (i, 0, j)),
            pl.BlockSpec((1, 1, tn), lambda i, j: (i, 0, j)),
        ],
        out_specs=pl.BlockSpec((1, 8, tn), lambda i, j: (i, 0, j)),
        out_shape=jax.ShapeDtypeStruct((depth, 8, nd), F32),
        compiler_params=_cparams(("parallel", "parallel")),
    )(cc, mod_w, mod_b.reshape(depth, 1, nd))


def _layer_norm(z, g, b):
    mu = jnp.mean(z, axis=-1, keepdims=True)
    zc = z - mu
    var = jnp.mean(zc * zc, axis=-1, keepdims=True)
    return zc * lax.rsqrt(var + LN_EPS) * g + b


def _post_mixer(x, y, mod, lng, lnb, router, n_exp):
    x1 = _layer_norm(ALPHA * x + mod[2:3] * y, lng, lnb)
    hf = x1 * (1.0 + mod[4:5]) + mod[3:4]
    hf_hi, hf_lo = _split_bf16(hf)
    both = jnp.dot(hf_hi, router, preferred_element_type=F32)
    logits = (both[:, :LANES] + both[:, LANES:]
              + jnp.dot(hf_lo, router[:, :LANES], preferred_element_type=F32))
    lt = logits.T[:n_exp]
    m = jnp.max(lt, axis=0, keepdims=True)
    p = jnp.exp(lt - m)
    aff = p / jnp.sum(p, axis=0, keepdims=True)
    return x1, hf_hi, aff


def _post_outs(b, n, d, n_exp):
    return (jax.ShapeDtypeStruct((b, n, d), F32),
            jax.ShapeDtypeStruct((b, n, d), BF16),
            jax.ShapeDtypeStruct((b, n_exp, n), F32))


def _post_out_specs(tm, d, n_exp):
    return (pl.BlockSpec((1, tm, d), lambda b, t: (b, t, 0)),
            pl.BlockSpec((1, tm, d), lambda b, t: (b, t, 0)),
            pl.BlockSpec((1, n_exp, tm), lambda b, t: (b, 0, t)))


def _pad_router(router):
    d, e = router.shape
    r = jnp.zeros((d, LANES), F32).at[:, :e].set(router)
    hi = lax.bitcast_convert_type(
        lax.bitcast_convert_type(r, jnp.uint32) & jnp.uint32(0xFFFF0000), F32)
    return jnp.concatenate([hi.astype(BF16), (r - hi).astype(BF16)], axis=1)


def _pool_kernel(x_ref, xp_ref, xn_ref, mod_ref, pw_ref, ps_ref, lng_ref, lnb_ref, r_ref,
                 x1_ref, hf_ref, aff_ref, buf, *, n, tm, n_exp):
    t = pl.program_id(1)
    nt = pl.num_programs(1)
    mod = mod_ref[0]
    x = x_ref[0]
    sc = 1.0 + mod[1:2]
    sh = mod[0:1]
    h = x * sc + sh
    hp = xp_ref[0] * sc + sh
    hn = xn_ref[0] * sc + sh
    buf[0:POOL_HALO, :] = jnp.where(t > 0, hp, 0.0)
    buf[POOL_HALO:POOL_HALO + tm, :] = h
    buf[POOL_HALO + tm:, :] = jnp.where(t < nt - 1, hn, 0.0)
    pos = t * tm + lax.broadcasted_iota(I32, (tm, 1), 0)
    ch = x.shape[1] // len(POOL_WINDOWS)
    parts = []
    for g, w in enumerate(POOL_WINDOWS):
        cols = slice(g * ch, (g + 1) * ch)
        acc = None
        for o in range(-(w // 2), w - w // 2):
            v = buf[POOL_HALO + o:POOL_HALO + o + tm, cols]
            acc = v if acc is None else acc + v
        lo = jnp.maximum(pos - w // 2, 0)
        hi = jnp.minimum(pos + (w - w // 2 - 1), n - 1)
        cnt = (hi - lo + 1).astype(F32)
        dlt = acc / cnt - h[:, cols]
        parts.append(jnp.dot(dlt.astype(BF16), pw_ref[g], preferred_element_type=F32))
    y = jnp.concatenate(parts, axis=1) * ps_ref[...]
    x1, hf, aff = _post_mixer(x, y, mod, lng_ref[...], lnb_ref[...], r_ref[...], n_exp)
    x1_ref[0] = x1
    hf_ref[0] = hf
    aff_ref[0] = aff


def _pool_layer(x, mod, pool_w, pool_scale, lng, lnb, router_p, n_exp):
    b, n, d = x.shape
    tm = min(256, n)
    hb = tm // POOL_HALO
    nhb = n // POOL_HALO
    g, ch, _ = pool_w.shape
    kern = functools.partial(_pool_kernel, n=n, tm=tm, n_exp=n_exp)
    return pl.pallas_call(
        kern,
        grid=(b, n // tm),
        in_specs=[
            pl.BlockSpec((1, tm, d), lambda i, t: (i, t, 0)),
            pl.BlockSpec((1, POOL_HALO, d), lambda i, t: (i, jnp.maximum(t * hb - 1, 0), 0)),
            pl.BlockSpec((1, POOL_HALO, d), lambda i, t: (i, jnp.minimum((t + 1) * hb, nhb - 1), 0)),
            pl.BlockSpec((1, 8, d), lambda i, t: (i, 0, 0)),
            pl.BlockSpec((g, ch, ch), lambda i, t: (0, 0, 0)),
            pl.BlockSpec((1, d), lambda i, t: (0, 0)),
            pl.BlockSpec((1, d), lambda i, t: (0, 0)),
            pl.BlockSpec((1, d), lambda i, t: (0, 0)),
            pl.BlockSpec((d, 2 * LANES), lambda i, t: (0, 0)),
        ],
        out_specs=_post_out_specs(tm, d, n_exp),
        out_shape=_post_outs(b, n, d, n_exp),
        scratch_shapes=[pltpu.VMEM((tm + 2 * POOL_HALO, d), F32)],
        compiler_params=_cparams(("parallel", "parallel")),
    )(x, x, x, mod, pool_w.astype(BF16), pool_scale.reshape(1, d), lng.reshape(1, d),
      lnb.reshape(1, d), router_p)


def _proj_in_kernel(x_ref, mod_ref, w_ref, o_ref, *, q_cols, tn):
    mod = mod_ref[0]
    h = (x_ref[0] * (1.0 + mod[1:2]) + mod[0:1]).astype(BF16)
    ncol = w_ref.shape[1]
    for j in range(ncol // tn):
        y = jnp.dot(h, w_ref[:, j * tn:(j + 1) * tn], preferred_element_type=F32)
        if (j + 1) * tn <= q_cols:
            y = y * (HEAD_DIM ** -0.5 * LOG2_E)
        o_ref[0, :, j * tn:(j + 1) * tn] = y.astype(o_ref.dtype)


def _proj_in(x, mod, w_bf16, q_cols):
    b, n, d = x.shape
    ncol = w_bf16.shape[1]
    tm = min(512, n)
    tn = 512
    kern = functools.partial(_proj_in_kernel, q_cols=q_cols, tn=tn)
    return pl.pallas_call(
        kern,
        grid=(b, n // tm),
        in_specs=[
            pl.BlockSpec((1, tm, d), lambda i, t: (i, t, 0)),
            pl.BlockSpec((1, 8, d), lambda i, t: (i, 0, 0)),
            pl.BlockSpec((d, ncol), lambda i, t: (0, 0)),
        ],
        out_specs=pl.BlockSpec((1, tm, ncol), lambda i, t: (i, t, 0)),
        out_shape=jax.ShapeDtypeStruct((b, n, ncol), BF16),
        compiler_params=_cparams(("parallel", "parallel")),
    )(x, mod, w_bf16)


def _gqa_proj_kernel(x_ref, mod_ref, w_ref, gain_ref, seg_ref, cos_ref, sin_ref, o_ref, *, n_norm):
    mod = mod_ref[0]
    h = (x_ref[0] * (1.0 + mod[1:2]) + mod[0:1]).astype(BF16)
    seg = seg_ref[...]
    cos = cos_ref[...]
    sin = sin_ref[...]
    ncol = w_ref.shape[1]
    wide = 2 * LANES
    y_all = jnp.dot(h, w_ref[...], preferred_element_type=F32)
    for j in range(ncol // wide):
        cols = slice(j * wide, (j + 1) * wide)
        y = y_all[:, cols]
        if 2 * j < n_norm:
            sq_hi, sq_lo = _split_bf16(y * y)
            ms = (jnp.dot(sq_hi, seg, preferred_element_type=F32)
                  + jnp.dot(sq_lo, seg, preferred_element_type=F32))
            yn = y * lax.rsqrt(ms + RMS_EPS) * gain_ref[:, cols]
            halves = []
            for k in range(2):
                part = yn[:, k * LANES:(k + 1) * LANES]
                halves.append(part * cos + pltpu.roll(part, LANES // 2, axis=1) * sin)
            y = jnp.concatenate(halves, axis=1)
        o_ref[0, :, cols] = y.astype(o_ref.dtype)


def _gqa_proj(x, mod, w_bf16, gain_row, seg, cos_t, sin_t, n_norm):
    b, n, d = x.shape
    ncol = w_bf16.shape[1]
    tm = min(512, n)
    kern = functools.partial(_gqa_proj_kernel, n_norm=n_norm)
    return pl.pallas_call(
        kern,
        grid=(b, n // tm),
        in_specs=[
            pl.BlockSpec((1, tm, d), lambda i, t: (i, t, 0)),
            pl.BlockSpec((1, 8, d), lambda i, t: (i, 0, 0)),
            pl.BlockSpec((d, ncol), lambda i, t: (0, 0)),
            pl.BlockSpec((1, ncol), lambda i, t: (0, 0)),
            pl.BlockSpec((2 * LANES, 2 * LANES), lambda i, t: (0, 0)),
            pl.BlockSpec((tm, LANES), lambda i, t: (t, 0)),
            pl.BlockSpec((tm, LANES), lambda i, t: (t, 0)),
        ],
        out_specs=pl.BlockSpec((1, tm, ncol), lambda i, t: (i, t, 0)),
        out_shape=jax.ShapeDtypeStruct((b, n, ncol), BF16),
        compiler_params=_cparams(("parallel", "parallel")),
    )(x, mod, w_bf16, gain_row, seg, cos_t, sin_t)


def _proj_out_kernel(o_ref, x_ref, mod_ref, w_ref, lng_ref, lnb_ref, r_ref,
                     x1_ref, hf_ref, aff_ref, *, n_exp):
    y = jnp.dot(o_ref[0], w_ref[...], preferred_element_type=F32)
    x1, hf, aff = _post_mixer(x_ref[0], y, mod_ref[0], lng_ref[...], lnb_ref[...], r_ref[...], n_exp)
    x1_ref[0] = x1
    hf_ref[0] = hf
    aff_ref[0] = aff


def _proj_out(o, x, mod, w_bf16, lng, lnb, router_p, n_exp):
    b, n, d = x.shape
    tm = min(256, n)
    kern = functools.partial(_proj_out_kernel, n_exp=n_exp)
    return pl.pallas_call(
        kern,
        grid=(b, n // tm),
        in_specs=[
            pl.BlockSpec((1, tm, o.shape[2]), lambda i, t: (i, t, 0)),
            pl.BlockSpec((1, tm, d), lambda i, t: (i, t, 0)),
            pl.BlockSpec((1, 8, d), lambda i, t: (i, 0, 0)),
            pl.BlockSpec(w_bf16.shape, lambda i, t: (0, 0)),
            pl.BlockSpec((1, d), lambda i, t: (0, 0)),
            pl.BlockSpec((1, d), lambda i, t: (0, 0)),
            pl.BlockSpec((d, 2 * LANES), lambda i, t: (0, 0)),
        ],
        out_specs=_post_out_specs(tm, d, n_exp),
        out_shape=_post_outs(b, n, d, n_exp),
        compiler_params=_cparams(("parallel", "parallel")),
    )(o, x, mod, w_bf16, lng.reshape(1, d), lnb.reshape(1, d), router_p)


def _lane_tiles(blocks):
    return [s[:, i * LANES:(i + 1) * LANES] for s in blocks for i in range(s.shape[1] // LANES)]


def _softmax_pv(scores, values):
    m = jnp.max(functools.reduce(jnp.maximum, _lane_tiles(scores)), axis=1, keepdims=True)
    ps = [jnp.exp2(s - m) for s in scores]
    l = jnp.sum(functools.reduce(lambda a, c: a + c, _lane_tiles(ps)), axis=1, keepdims=True)
    o = None
    for p, v in zip(ps, values):
        c = jnp.dot(p.astype(BF16), v, preferred_element_type=F32)
        o = c if o is None else o + c
    return o / l


def _na_kernel(q_ref, ka_ref, kb_ref, kc_ref, va_ref, vb_ref, vc_ref, kx_ref, vx_ref, bias_ref,
               o_ref, *, rows):
    i = pl.program_id(1)
    nb = pl.num_programs(1)
    tq = q_ref.shape[1]
    rq0 = i * NA_QROWS
    ks = jnp.clip(i - 1, 0, nb - NA_KBLOCKS) * NA_QROWS
    qi = lax.broadcasted_iota(I32, (tq, tq), 0)
    ki = lax.broadcasted_iota(I32, (tq, tq), 1)
    q_row = rq0 + qi // GRID_W
    q_col = qi % GRID_W
    k_col = ki % GRID_W
    r0 = jnp.clip(q_row - NA_KH // 2, 0, rows - NA_KH)
    c0 = jnp.clip(q_col - NA_KW // 2, 0, GRID_W - NA_KW)
    col_ok = (k_col >= c0) & (k_col < c0 + NA_KW)
    mask_add = []
    for blk in range(NA_KBLOCKS):
        k_row = ks + blk * NA_QROWS + ki // GRID_W
        ok = col_ok & (k_row >= r0) & (k_row < r0 + NA_KH)
        mask_add.append(jnp.where(ok, 0.0, NEG_INF))
    lane = lax.broadcasted_iota(I32, (1, LANES), 1)
    k_refs = (ka_ref, kb_ref, kc_ref)
    v_refs = (va_ref, vb_ref, vc_ref)
    n_pairs = q_ref.shape[2] // LANES
    for p in range(n_pairs):
        cols = slice(p * LANES, (p + 1) * LANES)
        q2 = q_ref[0, :, cols]
        kt = [r[0, :, cols] for r in k_refs] + [kx_ref[0, :, cols]]
        vt = [r[0, :, cols] for r in v_refs] + [vx_ref[0, :, cols]]
        qm = jnp.concatenate(
            [jnp.where((lane // HEAD_DIM) == hh, q2, jnp.zeros_like(q2)) for hh in range(2)], axis=0)
        scores = []
        for blk in range(NA_KBLOCKS):
            s = lax.dot_general(qm, kt[blk], (((1,), (1,)), ((), ())), preferred_element_type=F32)
            bands = []
            for hh in range(2):
                for qr in range(NA_QROWS):
                    halves = []
                    for kp in range(NA_QROWS // 2):
                        dr = ks + blk * NA_QROWS + 2 * kp - (rq0 + qr)
                        idx = jnp.clip(dr + NA_KH, 0, 2 * NA_KH - 1)
                        halves.append(bias_ref[2 * p + hh, idx] + mask_add[blk][
                            qr * GRID_W:(qr + 1) * GRID_W, kp * LANES:(kp + 1) * LANES])
                    bands.append(jnp.concatenate(halves, axis=1))
            scores.append(s + jnp.concatenate(bands, axis=0))
        scores.append(lax.dot_general(qm, kt[NA_KBLOCKS], (((1,), (1,)), ((), ())),
                                      preferred_element_type=F32))
        out = _softmax_pv(scores, vt)
        o_ref[0, :, cols] = jnp.where((lane // HEAD_DIM) == 0, out[:tq], out[tq:]).astype(o_ref.dtype)


def _na_bias_table(rpb):
    col = np.arange(GRID_W)
    dc = np.clip(col[None, :] - col[:, None] + (NA_KW - 1), 0, 2 * NA_KW - 2)
    t = rpb[:, :, dc]
    t_first = jnp.concatenate([t[:, :1], t], axis=1)
    t_next = jnp.concatenate([t, t[:, -1:]], axis=1)
    return jnp.concatenate([t_first, t_next], axis=-1).astype(F32) * LOG2_E


def _na_attention(qkv, qkv_c, bias_tab):
    b, n, d3 = qkv.shape
    d = d3 // 3
    l = qkv_c.shape[1]
    rows = n // GRID_W
    tq = NA_QROWS * GRID_W
    nb = n // tq
    heads = d // HEAD_DIM

    def kmap(off, col):
        return lambda i, t: (i, jnp.clip(t - 1, 0, nb - NA_KBLOCKS) + off, col)

    kern = functools.partial(_na_kernel, rows=rows)
    return pl.pallas_call(
        kern,
        grid=(b, nb),
        in_specs=[
            pl.BlockSpec((1, tq, d), lambda i, t: (i, t, 0)),
            pl.BlockSpec((1, tq, d), kmap(0, 1)),
            pl.BlockSpec((1, tq, d), kmap(1, 1)),
            pl.BlockSpec((1, tq, d), kmap(2, 1)),
            pl.BlockSpec((1, tq, d), kmap(0, 2)),
            pl.BlockSpec((1, tq, d), kmap(1, 2)),
            pl.BlockSpec((1, tq, d), kmap(2, 2)),
            pl.BlockSpec((1, l, d), lambda i, t: (i, 0, 1)),
            pl.BlockSpec((1, l, d), lambda i, t: (i, 0, 2)),
            pl.BlockSpec((heads, 2 * NA_KH, GRID_W, 2 * GRID_W), lambda i, t: (0, 0, 0, 0)),
        ],
        out_specs=pl.BlockSpec((1, tq, d), lambda i, t: (i, t, 0)),
        out_shape=jax.ShapeDtypeStruct((b, n, d), BF16),
        compiler_params=_cparams(("parallel", "parallel")),
    )(qkv, qkv, qkv, qkv, qkv, qkv, qkv, qkv_c, qkv_c, bias_tab)


def _ctx_attn_kernel(q_ref, k_ref, v_ref, o_ref):
    lane = lax.broadcasted_iota(I32, (1, LANES), 1)
    for p in range(q_ref.shape[2] // LANES):
        cols = slice(p * LANES, (p + 1) * LANES)
        q2 = q_ref[0, :, cols]
        k2 = k_ref[0, :, cols]
        v2 = v_ref[0, :, cols]
        outs = []
        for hh in range(2):
            qm = jnp.where((lane // HEAD_DIM) == hh, q2, jnp.zeros_like(q2))
            s = lax.dot_general(qm, k2, (((1,), (1,)), ((), ())), preferred_element_type=F32)
            outs.append(_softmax_pv([s], [v2]))
        o_ref[0, :, cols] = jnp.where((lane // HEAD_DIM) == 0, outs[0], outs[1]).astype(o_ref.dtype)


def _ctx_attention(qkv_c):
    b, l, d3 = qkv_c.shape
    d = d3 // 3
    return pl.pallas_call(
        _ctx_attn_kernel,
        grid=(b,),
        in_specs=[pl.BlockSpec((1, l, d), lambda i: (i, 0, 0)),
                  pl.BlockSpec((1, l, d), lambda i: (i, 0, 1)),
                  pl.BlockSpec((1, l, d), lambda i: (i, 0, 2))],
        out_specs=pl.BlockSpec((1, l, d), lambda i: (i, 0, 0)),
        out_shape=jax.ShapeDtypeStruct((b, l, d), BF16),
        compiler_params=_cparams(("parallel",)),
    )(qkv_c, qkv_c, qkv_c)


def _gqa_kernel(q_ref, k_ref, v_ref, kc_ref, vc_ref, bound_ref, o_ref, qm_sc, m_sc, l_sc, acc_sc, *,
                tk):
    nk = k_ref.shape[1] // tk
    tq = q_ref.shape[1]
    n_sl = q_ref.shape[2] // LANES
    lane = lax.broadcasted_iota(I32, (1, LANES), 1)
    for sl in range(n_sl):
        q2 = q_ref[0, :, sl * LANES:(sl + 1) * LANES]
        for hh in range(2):
            h = 2 * sl + hh
            qm_sc[h * tq:(h + 1) * tq, :] = jnp.where(((lane // (HEAD_DIM // 2)) % 2) == hh, q2,
                                                      jnp.zeros_like(q2))
    l_sc[...] = jnp.zeros_like(l_sc)
    acc_sc[...] = jnp.zeros_like(acc_sc)

    def scores(kt):
        return lax.dot_general(qm_sc[...], kt, (((1,), (1,)), ((), ())),
                               preferred_element_type=F32)

    def lane_partial_sum(p):
        return functools.reduce(lambda u, w: u + w, _lane_tiles([p]))

    def over_keys(update):
        def body(j, c):
            k0 = pl.multiple_of(j * tk, tk)
            update(k_ref[0, pl.ds(k0, tk), :], v_ref[0, pl.ds(k0, tk), :])
            return c

        lax.fori_loop(0, nk, body, 0, unroll=GQA_UNROLL if nk % GQA_UNROLL == 0 else 1)
        update(kc_ref[0], vc_ref[0])

    bound = bound_ref[0:1, 0:1]
    reference = bound - EXP2_HEADROOM
    safe = jnp.max(bound) <= EXP2_HEADROOM

    @pl.when(safe)
    def _():
        def update(kt, vt):
            p = jnp.exp2(scores(kt) - reference)
            l_sc[...] += lane_partial_sum(p)
            acc_sc[...] += jnp.dot(p.astype(BF16), vt, preferred_element_type=F32)

        over_keys(update)

    @pl.when(jnp.logical_not(safe))
    def _():
        m_sc[...] = jnp.full_like(m_sc, NEG_INF)

        def update(kt, vt):
            s = scores(kt)
            m_old = m_sc[...]
            m_new = jnp.maximum(m_old, jnp.max(s, axis=1, keepdims=True))
            a = jnp.exp2(m_old - m_new)
            p = jnp.exp2(s - m_new)
            l_sc[...] = a * l_sc[...] + lane_partial_sum(p)
            acc_sc[...] = a * acc_sc[...] + jnp.dot(p.astype(BF16), vt, preferred_element_type=F32)
            m_sc[...] = m_new

        over_keys(update)

    out = acc_sc[...] / jnp.sum(l_sc[...], axis=1, keepdims=True)
    for sl in range(n_sl):
        o0 = out[2 * sl * tq:(2 * sl + 1) * tq]
        o1 = out[(2 * sl + 1) * tq:(2 * sl + 2) * tq]
        o_ref[0, :, sl * LANES:(sl + 1) * LANES] = jnp.where(
            (lane // HEAD_DIM) == 0, o0, o1).astype(o_ref.dtype)


def _gqa_attention(qkv, qkv_c, dq, dkv, score_bound):
    b, n, _ = qkv.shape
    l = qkv_c.shape[1]
    n_slab = dkv // LANES
    q_per = dq // n_slab
    n_heads = 2 * q_per // LANES
    tq = min(256, n)
    tk = min(1024, n)
    k_blk = dq // LANES
    v_blk = (dq + dkv) // LANES
    kern = functools.partial(_gqa_kernel, tk=tk)
    return pl.pallas_call(
        kern,
        grid=(b, n_slab, n // tq),
        in_specs=[
            pl.BlockSpec((1, tq, q_per), lambda i, p, t: (i, t, p)),
            pl.BlockSpec((1, n, LANES), lambda i, p, t: (i, 0, k_blk + p)),
            pl.BlockSpec((1, n, LANES), lambda i, p, t: (i, 0, v_blk + p)),
            pl.BlockSpec((1, l, LANES), lambda i, p, t: (i, 0, k_blk + p)),
            pl.BlockSpec((1, l, LANES), lambda i, p, t: (i, 0, v_blk + p)),
            pl.BlockSpec((1, LANES), lambda i, p, t: (0, 0)),
        ],
        out_specs=pl.BlockSpec((1, tq, q_per), lambda i, p, t: (i, t, p)),
        out_shape=jax.ShapeDtypeStruct((b, n, dq), BF16),
        scratch_shapes=[pltpu.VMEM((n_heads * tq, LANES), BF16),
                        pltpu.VMEM((n_heads * tq, 1), F32),
                        pltpu.VMEM((n_heads * tq, LANES), F32),
                        pltpu.VMEM((n_heads * tq, LANES), F32)],
        compiler_params=_cparams(("parallel", "parallel", "parallel")),
    )(qkv, qkv, qkv, qkv_c, qkv_c, score_bound)


def _prefix_count(mask_f32, tri, out_cb):
    e, n = mask_f32.shape
    c = tri.shape[0]
    carry = jnp.zeros((e, 1), F32)
    for j in range(n // c):
        chunk = mask_f32[:, j * c:(j + 1) * c]
        inc = jnp.dot(chunk.astype(BF16), tri, preferred_element_type=F32) + carry
        out_cb(j, c, chunk, inc)
        carry = inc[:, c - 1:c]


def _topk_kernel(aff_ref, rank_ref, cum_ref, eqx_ref, *, cap):
    a = aff_ref[0]
    e, n = a.shape
    bits = lax.bitcast_convert_type(a, I32)

    def body(i, thr):
        cand = thr | (jnp.int32(1) << (30 - i))
        cnt = jnp.sum(jnp.where(bits >= cand, 1.0, 0.0), axis=1, keepdims=True)
        return jnp.where(cnt >= cap, cand, thr)

    thr = lax.fori_loop(0, 31, body, jnp.zeros((e, 1), I32))
    gt = bits > thr
    eq = bits == thr
    need = cap - jnp.sum(jnp.where(gt, 1.0, 0.0), axis=1, keepdims=True)
    c = min(256, n)
    tri = jnp.where(lax.broadcasted_iota(I32, (c, c), 0) <= lax.broadcasted_iota(I32, (c, c), 1),
                    1.0, 0.0).astype(BF16)

    def eq_cb(j, c, chunk, inc):
        eqx_ref[:, j * c:(j + 1) * c] = inc - chunk

    _prefix_count(jnp.where(eq, 1.0, 0.0), tri, eq_cb)
    sel = gt | (eq & (eqx_ref[...] < need))

    def sel_cb(j, c, chunk, inc):
        excl = (inc - chunk).astype(I32)
        cum_ref[0, :, j * c:(j + 1) * c] = excl
        rank_ref[0, :, j * c:(j + 1) * c] = jnp.where(chunk > 0.5, excl, -1)

    _prefix_count(jnp.where(sel, 1.0, 0.0), tri, sel_cb)


def _topk(aff, cap):
    b, e, n = aff.shape
    kern = functools.partial(_topk_kernel, cap=cap)
    return pl.pallas_call(
        kern,
        grid=(b,),
        in_specs=[pl.BlockSpec((1, e, n), lambda i: (i, 0, 0))],
        out_specs=(pl.BlockSpec((1, e, n), lambda i: (i, 0, 0)),
                   pl.BlockSpec((1, e, n), lambda i: (i, 0, 0))),
        out_shape=(jax.ShapeDtypeStruct((b, e, n), I32), jax.ShapeDtypeStruct((b, e, n), I32)),
        scratch_shapes=[pltpu.VMEM((e, n), F32)],
        compiler_params=_cparams(("parallel",)),
    )(aff)


def _slot_onehot(rank, starts, lows, k_new, rows_per, n_exp, weights=None):
    t = rank.shape[1]
    j_iota = lax.broadcasted_iota(I32, (rows_per, t), 0)
    rows = []
    for e in range(n_exp):
        r = rank[e:e + 1, :]
        local = jnp.where((r >= lows[e]) & (r < lows[e] + k_new), r - starts[e], -1)
        value = 1.0 if weights is None else weights[e:e + 1, :]
        rows.append(jnp.where(local == j_iota, value, 0.0))
    return jnp.concatenate(rows, axis=0)


def _align_down(v, a):
    return pl.multiple_of((v // a) * a, a)


def _gather_kernel(base_ref, rank_ref, hf_ref, xs_hbm, stage, carry, sem, pend, *, n_exp, n_tiles,
                   cap):
    b = pl.program_id(0)
    t = pl.program_id(1)
    off = (b * (n_tiles + 1) + t) * n_exp
    base = [base_ref[off + e] for e in range(n_exp)]
    cnt = [base_ref[off + n_exp + e] - base[e] for e in range(n_exp)]
    kmax = functools.reduce(jnp.maximum, cnt)
    rank = rank_ref[0]
    hf = hf_ref[0]
    slot = (b * n_tiles + t) % 2

    def window_copy(e, start, sl):
        return pltpu.make_async_copy(stage.at[sl, e * GATHER_ROWS:(e + 1) * GATHER_ROWS, :],
                                     xs_hbm.at[e, b, pl.ds(start, GATHER_ROWS), :], sem.at[e])

    def drain():
        @pl.when(pend[0] == 1)
        def _():
            for e in range(n_exp):
                window_copy(e, 0, slot).wait()
            pend[0] = 0

    @pl.when((b == 0) & (t == 0))
    def _():
        pend[0] = 0

    @pl.when(t == 0)
    def _():
        drain()
        carry[...] = jnp.zeros_like(carry)
        stage[1 - slot] = jnp.zeros(stage.shape[1:], stage.dtype)
        for e in range(n_exp):
            window_copy(e, cap, 1 - slot).start()
        pend[0] = 1

    def write_round(r, guarded):
        lows = [base[e] + r * ROUTE_K for e in range(n_exp)]
        starts = [_align_down(lows[e], ROW_ALIGN) for e in range(n_exp)]
        onehot = _slot_onehot(rank, starts, lows, ROUTE_K, GATHER_ROWS, n_exp).astype(BF16)
        if guarded:
            drain()
        step = GATHER_CHUNK * GATHER_ROWS
        for c0 in range(0, n_exp * GATHER_ROWS, step):
            stage[slot, c0:c0 + step, :] = jnp.dot(onehot[c0:c0 + step], hf,
                                                   preferred_element_type=F32).astype(BF16)
        for e in range(n_exp):
            r0 = e * GATHER_ROWS
            stage[slot, r0:r0 + ROW_ALIGN, :] += carry[e * ROW_ALIGN:(e + 1) * ROW_ALIGN, :]
        if not guarded:
            drain()

        def move(e):
            window_copy(e, starts[e], slot).start()
            filled = lows[e] + jnp.clip(cnt[e] - r * ROUTE_K, 0, ROUTE_K)
            shift = _align_down(filled, ROW_ALIGN) - starts[e]
            src = pl.multiple_of(e * GATHER_ROWS + shift, ROW_ALIGN)
            carry[e * ROW_ALIGN:(e + 1) * ROW_ALIGN, :] = stage[slot, pl.ds(src, ROW_ALIGN), :]

        if not guarded:
            for e in range(n_exp):
                move(e)
            pend[0] = 1
            return
        for e in range(n_exp):
            @pl.when(cnt[e] > r * ROUTE_K)
            def _(e=e):
                move(e)
        for e in range(n_exp):
            @pl.when(cnt[e] > r * ROUTE_K)
            def _(e=e):
                window_copy(e, 0, slot).wait()

    write_round(0, False)

    def extra_round(r, c):
        write_round(r, True)
        return c

    lax.fori_loop(1, (kmax + ROUTE_K - 1) // ROUTE_K, extra_round, 0)

    @pl.when((b == pl.num_programs(0) - 1) & (t == n_tiles - 1))
    def _():
        drain()


def _gather(base_flat, rank, hf, cap):
    b, n, d = hf.shape
    n_exp = rank.shape[1]
    n_tiles = n // ROUTE_T
    kern = functools.partial(_gather_kernel, n_exp=n_exp, n_tiles=n_tiles, cap=cap)
    return pl.pallas_call(
        kern,
        grid_spec=pltpu.PrefetchScalarGridSpec(
            num_scalar_prefetch=1,
            grid=(b, n_tiles),
            in_specs=[
                pl.BlockSpec((1, n_exp, ROUTE_T), lambda i, t, s: (i, 0, t)),
                pl.BlockSpec((1, ROUTE_T, d), lambda i, t, s: (i, t, 0)),
            ],
            out_specs=pl.BlockSpec(memory_space=pl.ANY),
            scratch_shapes=[pltpu.VMEM((2, n_exp * GATHER_ROWS, d), BF16),
                            pltpu.VMEM((n_exp * ROW_ALIGN, d), BF16),
                            pltpu.SemaphoreType.DMA((n_exp,)),
                            pltpu.SMEM((1,), I32)],
        ),
        out_shape=jax.ShapeDtypeStruct((n_exp, b, cap + GATHER_ROWS, d), BF16),
        compiler_params=_cparams(("arbitrary", "arbitrary")),
    )(base_flat, rank, hf)


def _split_bf16(v):
    hi = v.astype(BF16)
    return hi, (v - hi.astype(F32)).astype(BF16)


def _ffn_kernel(*refs, n_f, n_b, with_ctx):
    if with_ctx:
        x_ref, xc_ref, w1_ref, w3_ref, w2_ref, y_ref, yc_ref, acc = refs
    else:
        x_ref, w1_ref, w3_ref, w2_ref, y_ref, acc = refs
    i = pl.program_id(1)
    f = pl.program_id(2)
    tf = w1_ref.shape[3]
    sub = min(FFN_SUB, tf)

    def run(x, emit):
        rows = x.shape[0]
        y = None
        for c in range(tf // sub):
            cs = slice(c * sub, (c + 1) * sub)
            h1 = jnp.dot(x, w1_ref[0, 0, :, cs].astype(BF16), preferred_element_type=F32)
            h3 = jnp.dot(x, w3_ref[0, 0, :, cs].astype(BF16), preferred_element_type=F32)
            hid = (_silu(h1) * h3).astype(BF16)
            part = jnp.dot(hid, w2_ref[0, 0, cs, :].astype(BF16), preferred_element_type=F32)
            y = part if y is None else y + part
        if n_f == 1:
            emit(y.astype(BF16))
            return

        @pl.when(f == 0)
        def _():
            acc[0:rows, :] = y

        @pl.when((f > 0) & (f < n_f - 1))
        def _():
            acc[0:rows, :] += y

        @pl.when(f == n_f - 1)
        def _():
            emit((acc[0:rows, :] + y).astype(BF16))

    def emit_main(out):
        y_ref[0, 0] = out

    def run_main():
        run(x_ref[0, 0], emit_main)

    if not with_ctx:
        run_main()
        return

    bc, cap_c, d = xc_ref.shape[1:]
    cap = x_ref.shape[2]

    def emit_both(out):
        y_ref[0, 0] = out[:cap]
        yc_ref[0] = out[cap:].reshape(bc, cap_c, d)

    pl.when(i < n_b - 1)(run_main)

    @pl.when(i == n_b - 1)
    def _():
        run(jnp.concatenate([x_ref[0, 0], xc_ref[0].reshape(bc * cap_c, d)], axis=0), emit_both)


def _ffn(xs, xs_c, w1, w3, w2, layer, cap, cap_c):
    n_exp, b, _, d = xs.shape
    ff = w1.shape[3]
    tf = min(FFN_BLOCK, ff)
    n_f = ff // tf
    with_ctx = xs_c is not None
    kern = functools.partial(_ffn_kernel, n_f=n_f, n_b=b, with_ctx=with_ctx)

    def main_map(e, i, f):
        return (e, i, 0, 0)

    x_specs = [pl.BlockSpec((1, 1, cap, d), main_map)]
    out_specs = [pl.BlockSpec((1, 1, cap, d), main_map)]
    out_shape = [jax.ShapeDtypeStruct((n_exp, b, cap, d), BF16)]
    operands = [xs]
    acc_rows = cap
    if with_ctx:
        c_spec = pl.BlockSpec((1, b, cap_c, d), lambda e, i, f: (e, 0, 0, 0))
        x_specs.append(c_spec)
        out_specs.append(c_spec)
        out_shape.append(jax.ShapeDtypeStruct((n_exp, b, cap_c, d), BF16))
        operands.append(xs_c)
        acc_rows += b * cap_c
    return pl.pallas_call(
        kern,
        grid=(n_exp, b, n_f),
        in_specs=x_specs + [
            pl.BlockSpec((1, 1, d, tf), lambda e, i, f: (layer, e, 0, f)),
            pl.BlockSpec((1, 1, d, tf), lambda e, i, f: (layer, e, 0, f)),
            pl.BlockSpec((1, 1, tf, d), lambda e, i, f: (layer, e, f, 0)),
        ],
        out_specs=tuple(out_specs),
        out_shape=tuple(out_shape),
        scratch_shapes=[pltpu.VMEM((acc_rows, d), F32)],
        compiler_params=_cparams(("parallel", "arbitrary", "arbitrary")),
    )(*operands, w1, w3, w2)


def _combine_kernel(base_ref, rank_ref, aff_ref, x1_ref, mod_ref, lng_ref, lnb_ref, y_hbm,
                    out_ref, ybuf, sem, *, n_exp, n_tiles, cap, win):
    k_new = win - ROW_ALIGN
    b = pl.program_id(0)
    t = pl.program_id(1)
    g = b * n_tiles + t
    total = pl.num_programs(0) * n_tiles
    slot = g % 2
    tt = rank_ref.shape[2]

    def tile_info(step):
        off = ((step // n_tiles) * (n_tiles + 1) + step % n_tiles) * n_exp
        base = [base_ref[off + e] for e in range(n_exp)]
        cnt = [base_ref[off + n_exp + e] - base[e] for e in range(n_exp)]
        return base, cnt

    def window(base, r, e):
        low = base[e] + r * k_new
        start = pl.multiple_of(jnp.minimum(_align_down(low, ROW_ALIGN), cap - win), ROW_ALIGN)
        return low, start

    def copy(step, start, e, sl):
        return pltpu.make_async_copy(
            y_hbm.at[e, step // n_tiles, pl.ds(start, win), :],
            ybuf.at[sl, pl.ds(e * win, win), :], sem.at[sl, e])

    def issue(step, r, sl):
        base, cnt = tile_info(step)
        for e in range(n_exp):
            def go(e=e):
                copy(step, window(base, r, e)[1], e, sl).start()
            if isinstance(r, int) and r == 0:
                go()
            else:
                pl.when(cnt[e] > r * k_new)(go)

    def wait(step, r, sl):
        _, cnt = tile_info(step)
        for e in range(n_exp):
            def go(e=e):
                copy(step, 0, e, sl).wait()
            if isinstance(r, int) and r == 0:
                go()
            else:
                pl.when(cnt[e] > r * k_new)(go)

    @pl.when(g == 0)
    def _():
        ybuf[...] = jnp.zeros_like(ybuf)
        issue(g, 0, slot)

    @pl.when(g + 1 < total)
    def _():
        issue(g + 1, 0, 1 - slot)

    base, cnt = tile_info(g)
    kmax = functools.reduce(jnp.maximum, cnt)
    rank = rank_ref[0]
    aff = aff_ref[0]

    def scatter(r, f):
        lows, starts = zip(*[window(base, r, e) for e in range(n_exp)])
        gates = _slot_onehot(rank, starts, lows, k_new, win, n_exp, weights=aff)
        wait(g, r, slot)
        return f + jnp.dot(gates.T.astype(BF16), ybuf[slot], preferred_element_type=F32)

    def extra_round(r, f):
        issue(g, r, slot)
        return scatter(r, f)

    d = x1_ref.shape[2]
    f = scatter(0, jnp.zeros((tt, d), F32))
    f = lax.fori_loop(1, (kmax + k_new - 1) // k_new, extra_round, f)
    mod = mod_ref[0]
    out_ref[0] = _layer_norm(ALPHA * x1_ref[0] + mod[5:6] * f, lng_ref[...], lnb_ref[...])


def _combine(cum, rank, aff, x1, mod, lng, lnb, y, cap):
    b, n, d = x1.shape
    n_exp = rank.shape[1]
    n_tiles = n // COMBINE_T
    win = min(COMBINE_WIN, cap)
    assert cap % ROW_ALIGN == 0 and win > ROW_ALIGN
    kern = functools.partial(_combine_kernel, n_exp=n_exp, n_tiles=n_tiles, cap=cap, win=win)
    return pl.pallas_call(
        kern,
        grid_spec=pltpu.PrefetchScalarGridSpec(
            num_scalar_prefetch=1,
            grid=(b, n_tiles),
            in_specs=[
                pl.BlockSpec((1, n_exp, COMBINE_T), lambda i, t, s: (i, 0, t)),
                pl.BlockSpec((1, n_exp, COMBINE_T), lambda i, t, s: (i, 0, t)),
                pl.BlockSpec((1, COMBINE_T, d), lambda i, t, s: (i, t, 0)),
                pl.BlockSpec((1, 8, d), lambda i, t, s: (i, 0, 0)),
                pl.BlockSpec((1, d), lambda i, t, s: (0, 0)),
                pl.BlockSpec((1, d), lambda i, t, s: (0, 0)),
                pl.BlockSpec(memory_space=pl.ANY),
            ],
            out_specs=pl.BlockSpec((1, COMBINE_T, d), lambda i, t, s: (i, t, 0)),
            scratch_shapes=[pltpu.VMEM((2, n_exp * win, d), BF16),
                            pltpu.SemaphoreType.DMA((2, n_exp))],
        ),
        out_shape=jax.ShapeDtypeStruct((b, n, d), F32),
        compiler_params=_cparams(("arbitrary", "arbitrary")),
    )(_tile_bases(cum, COMBINE_T, cap), rank, aff, x1, mod, lng.reshape(1, d), lnb.reshape(1, d), y)


def _tile_bases(cum, tile, cap):
    b, n_exp, _ = cum.shape
    tile_base = jnp.swapaxes(cum[:, :, ::tile], 1, 2)
    return jnp.concatenate([tile_base, jnp.full((b, 1, n_exp), cap, I32)], axis=1).reshape(-1)


def _route(hf, aff):
    n = hf.shape[1]
    cap = EC_CAPACITY * n // aff.shape[1]
    rank, cum = _topk(aff, cap)
    return cum, rank, _gather(_tile_bases(cum, ROUTE_T, cap), rank, hf, cap), cap


def _moe(post_x, post_c, mod_x, mod_c, lng, lnb, w1, w3, w2, layer):
    x1, hf, aff = post_x
    cum, rank, xs, cap = _route(hf, aff)
    if post_c is None:
        (y,) = _ffn(xs, None, w1, w3, w2, layer, cap, None)
        return _combine(cum, rank, aff, x1, mod_x, lng, lnb, y, cap), None
    c1, hf_c, aff_c = post_c
    cum_c, rank_c, xs_c, cap_c = _route(hf_c, aff_c)
    y, y_c = _ffn(xs, xs_c, w1, w3, w2, layer, cap, cap_c)
    return (_combine(cum, rank, aff, x1, mod_x, lng, lnb, y, cap),
            _combine(cum_c, rank_c, aff_c, c1, mod_c, lng, lnb, y_c, cap_c))


def _gqa_layout(n_q_heads):
    group = n_q_heads // GQA_KV_HEADS
    half = HEAD_DIM // 2
    ev = np.arange(half) * 2
    od = ev + 1

    def slab(col_a, col_b):
        return np.concatenate([col_a + ev, col_b + ev, col_a + od, col_b + od])

    q_cols, k_cols, o_rows, gain_idx = [], [], [], []
    dq = n_q_heads * HEAD_DIM
    for p in range(GQA_KV_HEADS // 2):
        for i in range(group):
            a = (2 * p) * group + i
            c = (2 * p + 1) * group + i
            q_cols.append(slab(a * HEAD_DIM, c * HEAD_DIM))
            o_rows.append(np.concatenate([a * HEAD_DIM + np.arange(HEAD_DIM),
                                          c * HEAD_DIM + np.arange(HEAD_DIM)]))
    for p in range(GQA_KV_HEADS // 2):
        k_cols.append(dq + slab(2 * p * HEAD_DIM, (2 * p + 1) * HEAD_DIM))
    lane_dim = np.concatenate([ev, ev, od, od])
    return (np.concatenate(q_cols), np.concatenate(k_cols), np.concatenate(o_rows), lane_dim)


def _rope_tables(n):
    t = jnp.arange(n, dtype=I32)
    row = (t // GRID_W).astype(F32)
    col = (t % GRID_W).astype(F32)
    axis_dims = HEAD_DIM // 2
    inv_freq = jnp.power(ROPE_THETA, -jnp.arange(0, axis_dims, 2, dtype=F32) / axis_dims)
    ang = jnp.concatenate([row[:, None] * inv_freq, col[:, None] * inv_freq], axis=-1)
    cos, sin = jnp.cos(ang), jnp.sin(ang)
    cos_t = jnp.concatenate([cos] * 4, axis=1)
    sin_t = jnp.concatenate([-sin, -sin, sin, sin], axis=1)
    return cos_t, sin_t


def kernel(x, c, ctx, c_ctx, mod_w, mod_b, ln_g, ln_b, pool_w, pool_scale, na_wqkv, na_wo, na_rpb,
           gqa_wqkv, gqa_q_norm, gqa_k_norm, gqa_wo, moe_router, moe_w1, moe_w3, moe_w2):
    bsz, n, d = x.shape
    l = ctx.shape[1]
    n_exp = moe_router.shape[2]
    depth = mod_w.shape[0]
    cc = jnp.zeros((8, d), F32).at[:bsz].set(c).at[bsz].set(c_ctx)
    mod_all = _modulation(cc, mod_w, mod_b)

    for i in range(depth):
        m = i % N_MIXERS
        j = i // N_MIXERS
        update_ctx = any(k % N_MIXERS != 0 for k in range(i + 1, depth))
        ctx_keys = m != 0
        mod6 = mod_all[i].reshape(8, N_MOD, d)
        mod_x = jnp.zeros((bsz, 8, d), F32).at[:, :N_MOD].set(mod6[:bsz])
        mod_c = jnp.zeros((bsz, 8, d), F32).at[:, :N_MOD].set(
            jnp.broadcast_to(mod6[bsz][None], (bsz, N_MOD, d)))
        router_p = _pad_router(moe_router[i])
        lng0, lnb0, lng1, lnb1 = ln_g[i, 0], ln_b[i, 0], ln_g[i, 1], ln_b[i, 1]
        post_c = None
        if m == 0:
            post_x = _pool_layer(x, mod_x, pool_w[j], pool_scale[j], lng0, lnb0, router_p, n_exp)
            if update_ctx:
                post_c = _pool_layer(ctx, mod_c, pool_w[j], pool_scale[j], lng0, lnb0, router_p,
                                     n_exp)
        elif m == 1:
            wqkv = na_wqkv[j].astype(BF16)
            wo = na_wo[j].astype(BF16)
            qkv = _proj_in(x, mod_x, wqkv, d)
            qkv_c = _proj_in(ctx, mod_c, wqkv, d)
            o = _na_attention(qkv, qkv_c, _na_bias_table(na_rpb[j]))
            post_x = _proj_out(o, x, mod_x, wo, lng0, lnb0, router_p, n_exp)
            if update_ctx:
                oc = _ctx_attention(qkv_c)
                post_c = _proj_out(oc, ctx, mod_c, wo, lng0, lnb0, router_p, n_exp)
        else:
            n_q_heads = gqa_wo.shape[1] // HEAD_DIM
            dq = n_q_heads * HEAD_DIM
            dkv = GQA_KV_HEADS * HEAD_DIM
            q_cols, k_cols, o_rows, lane_dim = _gqa_layout(n_q_heads)
            v_cols = dq + dkv + np.arange(dkv)
            w_perm = gqa_wqkv[j][:, np.concatenate([q_cols, k_cols, v_cols])].astype(BF16)
            wo = gqa_wo[j][o_rows].astype(BF16)
            gain_row = jnp.concatenate(
                [jnp.tile(gqa_q_norm[j][lane_dim] * (HEAD_DIM ** -0.5 * LOG2_E), dq // LANES),
                 jnp.tile(gqa_k_norm[j][lane_dim], dkv // LANES),
                 jnp.ones((dkv,), F32)]).reshape(1, -1)
            lanes2 = np.arange(2 * LANES)
            lane_head = 2 * (lanes2 // LANES) + (lanes2 // (HEAD_DIM // 2)) % 2
            seg = jnp.asarray((lane_head[:, None] == lane_head[None, :]) / HEAD_DIM, BF16)
            cos_t, sin_t = _rope_tables(n)
            n_norm = (dq + dkv) // LANES
            qkv = _gqa_proj(x, mod_x, w_perm, gain_row, seg, cos_t, sin_t, n_norm)
            qkv_c = _gqa_proj(ctx, mod_c, w_perm, gain_row, seg, jnp.ones((l, LANES), F32),
                              jnp.zeros((l, LANES), F32), n_norm)
            q_gain_max = jnp.max(jnp.abs(gqa_q_norm[j])) * (HEAD_DIM ** -0.5 * LOG2_E)
            score_bound = jnp.full((1, LANES), 1.02 * HEAD_DIM, F32) * (
                q_gain_max * jnp.max(jnp.abs(gqa_k_norm[j])))
            o = _gqa_attention(qkv, qkv_c, dq, dkv, score_bound)
            post_x = _proj_out(o, x, mod_x, wo, lng0, lnb0, router_p, n_exp)
            if update_ctx:
                raise NotImplementedError("context update after a GQA layer is not part of this stack")
        x, ctx_new = _moe(post_x, post_c if update_ctx else None, mod_x, mod_c, lng1, lnb1,
                          moe_w1, moe_w3, moe_w2, i)
        if update_ctx:
            ctx = ctx_new
    return x
```

```python
import functools
import math

import jax
import jax.numpy as jnp
import numpy as np
from jax import lax
from jax.experimental import pallas as pl
from jax.experimental.pallas import tpu as pltpu

F32 = jnp.float32
BF16 = jnp.bfloat16
I32 = jnp.int32
HIGHEST = lax.Precision.HIGHEST

DEPTH = 4
N_MIXERS = 3
GRID_W = 64
HEAD_DIM = 64
POOL_WINDOWS = (2, 4, 8, 16)
POOL_HALO = 8
NA_KH = 8
NA_KW = 16
NA_QROWS = 4
NA_KBLOCKS = 3
GQA_KV_HEADS = 4
ROPE_THETA = 10000.0
EC_CAPACITY = 2
N_MOD = 6
LN_EPS = 1e-5
RMS_EPS = 1e-6
ALPHA = (2.0 * DEPTH) ** 0.25
LANES = 128
ROUTE_T = 256
GQA_UNROLL = 4
ROUTE_K = 48
COMBINE_T = 256
COMBINE_WIN = 64
ROW_ALIGN = 16
GATHER_ROWS = ROUTE_K + ROW_ALIGN
GATHER_CHUNK = 2
FFN_BLOCK = 1024
FFN_SUB = 256
VMEM_LIMIT = 56 * 1024 * 1024
NEG_INF = -1e30
LOG2_E = math.log2(math.e)
EXP2_HEADROOM = 64.0


def _cparams(sem, vmem=VMEM_LIMIT):
    return pltpu.CompilerParams(dimension_semantics=sem, vmem_limit_bytes=vmem)


def _silu(v):
    return v / (1.0 + jnp.exp(-v))


def _mod_kernel(c_ref, w_ref, b_ref, o_ref):
    s = _silu(c_ref[...])
    o_ref[0] = jnp.dot(s, w_ref[0], precision=HIGHEST, preferred_element_type=F32) + b_ref[0]


def _modulation(cc, mod_w, mod_b):
    depth, d, nd = mod_w.shape
    tn = nd // 4
    return pl.pallas_call(
        _mod_kernel,
        grid=(depth, nd // tn),
        in_specs=[
            pl.BlockSpec((8, d), lambda i, j: (0, 0)),
            pl.BlockSpec((1, d, tn), lambda i, j: (i, 0, j)),
            pl.BlockSpec((1, 1, tn), lambda i, j: (i, 0, j)),
        ],
        out_specs=pl.BlockSpec((1, 8, tn), lambda i, j: (i, 0, j)),
        out_shape=jax.ShapeDtypeStruct((depth, 8, nd), F32),
        compiler_params=_cparams(("parallel", "parallel")),
    )(cc, mod_w, mod_b.reshape(depth, 1, nd))


def _layer_norm(z, g, b):
    mu = jnp.mean(z, axis=-1, keepdims=True)
    zc = z - mu
    var = jnp.mean(zc * zc, axis=-1, keepdims=True)
    return zc * lax.rsqrt(var + LN_EPS) * g + b


def _post_mixer(x, y, mod, lng, lnb, router, n_exp):
    x1 = _layer_norm(ALPHA * x + mod[2:3] * y, lng, lnb)
    hf = x1 * (1.0 + mod[4:5]) + mod[3:4]
    hf_hi, hf_lo = _split_bf16(hf)
    both = jnp.dot(hf_hi, router, preferred_element_type=F32)
    logits = (both[:, :LANES] + both[:, LANES:]
              + jnp.dot(hf_lo, router[:, :LANES], preferred_element_type=F32))
    lt = logits.T[:n_exp]
    m = jnp.max(lt, axis=0, keepdims=True)
    p = jnp.exp(lt - m)
    aff = p / jnp.sum(p, axis=0, keepdims=True)
    return x1, hf_hi, aff


def _post_outs(b, n, d, n_exp):
    return (jax.ShapeDtypeStruct((b, n, d), F32),
            jax.ShapeDtypeStruct((b, n, d), BF16),
            jax.ShapeDtypeStruct((b, n_exp, n), F32))


def _post_out_specs(tm, d, n_exp):
    return (pl.BlockSpec((1, tm, d), lambda b, t: (b, t, 0)),
            pl.BlockSpec((1, tm, d), lambda b, t: (b, t, 0)),
            pl.BlockSpec((1, n_exp, tm), lambda b, t: (b, 0, t)))


def _pad_router(router):
    d, e = router.shape
    r = jnp.zeros((d, LANES), F32).at[:, :e].set(router)
    hi = lax.bitcast_convert_type(
        lax.bitcast_convert_type(r, jnp.uint32) & jnp.uint32(0xFFFF0000), F32)
    return jnp.concatenate([hi.astype(BF16), (r - hi).astype(BF16)], axis=1)


def _pool_kernel(x_ref, xp_ref, xn_ref, mod_ref, pw_ref, ps_ref, lng_ref, lnb_ref, r_ref,
                 x1_ref, hf_ref, aff_ref, buf, *, n, tm, n_exp):
    t = pl.program_id(1)
    nt = pl.num_programs(1)
    mod = mod_ref[0]
    x = x_ref[0]
    sc = 1.0 + mod[1:2]
    sh = mod[0:1]
    h = x * sc + sh
    hp = xp_ref[0] * sc + sh
    hn = xn_ref[0] * sc + sh
    buf[0:POOL_HALO, :] = jnp.where(t > 0, hp, 0.0)
    buf[POOL_HALO:POOL_HALO + tm, :] = h
    buf[POOL_HALO + tm:, :] = jnp.where(t < nt - 1, hn, 0.0)
    pos = t * tm + lax.broadcasted_iota(I32, (tm, 1), 0)
    ch = x.shape[1] // len(POOL_WINDOWS)
    parts = []
    for g, w in enumerate(POOL_WINDOWS):
        cols = slice(g * ch, (g + 1) * ch)
        acc = None
        for o in range(-(w // 2), w - w // 2):
            v = buf[POOL_HALO + o:POOL_HALO + o + tm, cols]
            acc = v if acc is None else acc + v
        lo = jnp.maximum(pos - w // 2, 0)
        hi = jnp.minimum(pos + (w - w // 2 - 1), n - 1)
        cnt = (hi - lo + 1).astype(F32)
        dlt = acc / cnt - h[:, cols]
        parts.append(jnp.dot(dlt.astype(BF16), pw_ref[g], preferred_element_type=F32))
    y = jnp.concatenate(parts, axis=1) * ps_ref[...]
    x1, hf, aff = _post_mixer(x, y, mod, lng_ref[...], lnb_ref[...], r_ref[...], n_exp)
    x1_ref[0] = x1
    hf_ref[0] = hf
    aff_ref[0] = aff


def _pool_layer(x, mod, pool_w, pool_scale, lng, lnb, router_p, n_exp):
    b, n, d = x.shape
    tm = min(512, n)
    hb = tm // POOL_HALO
    nhb = n // POOL_HALO
    g, ch, _ = pool_w.shape
    kern = functools.partial(_pool_kernel, n=n, tm=tm, n_exp=n_exp)
    return pl.pallas_call(
        kern,
        grid=(b, n // tm),
        in_specs=[
            pl.BlockSpec((1, tm, d), lambda i, t: (i, t, 0)),
            pl.BlockSpec((1, POOL_HALO, d), lambda i, t: (i, jnp.maximum(t * hb - 1, 0), 0)),
            pl.BlockSpec((1, POOL_HALO, d), lambda i, t: (i, jnp.minimum((t + 1) * hb, nhb - 1), 0)),
            pl.BlockSpec((1, 8, d), lambda i, t: (i, 0, 0)),
            pl.BlockSpec((g, ch, ch), lambda i, t: (0, 0, 0)),
            pl.BlockSpec((1, d), lambda i, t: (0, 0)),
            pl.BlockSpec((1, d), lambda i, t: (0, 0)),
            pl.BlockSpec((1, d), lambda i, t: (0, 0)),
            pl.BlockSpec((d, 2 * LANES), lambda i, t: (0, 0)),
        ],
        out_specs=_post_out_specs(tm, d, n_exp),
        out_shape=_post_outs(b, n, d, n_exp),
        scratch_shapes=[pltpu.VMEM((tm + 2 * POOL_HALO, d), F32)],
        compiler_params=_cparams(("parallel", "parallel")),
    )(x, x, x, mod, pool_w.astype(BF16), pool_scale.reshape(1, d), lng.reshape(1, d),
      lnb.reshape(1, d), router_p)


def _proj_in_kernel(x_ref, mod_ref, w_ref, o_ref, *, q_cols, tn):
    mod = mod_ref[0]
    h = (x_ref[0] * (1.0 + mod[1:2]) + mod[0:1]).astype(BF16)
    ncol = w_ref.shape[1]
    for j in range(ncol // tn):
        y = jnp.dot(h, w_ref[:, j * tn:(j + 1) * tn], preferred_element_type=F32)
        if (j + 1) * tn <= q_cols:
            y = y * (HEAD_DIM ** -0.5 * LOG2_E)
        o_ref[0, :, j * tn:(j + 1) * tn] = y.astype(o_ref.dtype)


def _proj_in(x, mod, w_bf16, q_cols):
    b, n, d = x.shape
    ncol = w_bf16.shape[1]
    tm = min(512, n)
    tn = 512
    kern = functools.partial(_proj_in_kernel, q_cols=q_cols, tn=tn)
    return pl.pallas_call(
        kern,
        grid=(b, n // tm),
        in_specs=[
            pl.BlockSpec((1, tm, d), lambda i, t: (i, t, 0)),
            pl.BlockSpec((1, 8, d), lambda i, t: (i, 0, 0)),
            pl.BlockSpec((d, ncol), lambda i, t: (0, 0)),
        ],
        out_specs=pl.BlockSpec((1, tm, ncol), lambda i, t: (i, t, 0)),
        out_shape=jax.ShapeDtypeStruct((b, n, ncol), BF16),
        compiler_params=_cparams(("parallel", "parallel")),
    )(x, mod, w_bf16)


def _gqa_proj_kernel(x_ref, mod_ref, w_ref, gain_ref, seg_ref, cos_ref, sin_ref, o_ref, *, n_norm):
    mod = mod_ref[0]
    h = (x_ref[0] * (1.0 + mod[1:2]) + mod[0:1]).astype(BF16)
    seg = seg_ref[...]
    cos = cos_ref[...]
    sin = sin_ref[...]
    ncol = w_ref.shape[1]
    wide = 2 * LANES
    y_all = jnp.dot(h, w_ref[...], preferred_element_type=F32)
    for j in range(ncol // wide):
        cols = slice(j * wide, (j + 1) * wide)
        y = y_all[:, cols]
        if 2 * j < n_norm:
            sq_hi, sq_lo = _split_bf16(y * y)
            ms = (jnp.dot(sq_hi, seg, preferred_element_type=F32)
                  + jnp.dot(sq_lo, seg, preferred_element_type=F32))
            yn = y * lax.rsqrt(ms + RMS_EPS) * gain_ref[:, cols]
            halves = []
            for k in range(2):
                part = yn[:, k * LANES:(k + 1) * LANES]
                halves.append(part * cos + pltpu.roll(part, LANES // 2, axis=1) * sin)
            y = jnp.concatenate(halves, axis=1)
        o_ref[0, :, cols] = y.astype(o_ref.dtype)


def _gqa_proj(x, mod, w_bf16, gain_row, seg, cos_t, sin_t, n_norm):
    b, n, d = x.shape
    ncol = w_bf16.shape[1]
    tm = min(512, n)
    kern = functools.partial(_gqa_proj_kernel, n_norm=n_norm)
    return pl.pallas_call(
        kern,
        grid=(b, n // tm),
        in_specs=[
            pl.BlockSpec((1, tm, d), lambda i, t: (i, t, 0)),
            pl.BlockSpec((1, 8, d), lambda i, t: (i, 0, 0)),
            pl.BlockSpec((d, ncol), lambda i, t: (0, 0)),
            pl.BlockSpec((1, ncol), lambda i, t: (0, 0)),
            pl.BlockSpec((2 * LANES, 2 * LANES), lambda i, t: (0, 0)),
            pl.BlockSpec((tm, LANES), lambda i, t: (t, 0)),
            pl.BlockSpec((tm, LANES), lambda i, t: (t, 0)),
        ],
        out_specs=pl.BlockSpec((1, tm, ncol), lambda i, t: (i, t, 0)),
        out_shape=jax.ShapeDtypeStruct((b, n, ncol), BF16),
        compiler_params=_cparams(("parallel", "parallel")),
    )(x, mod, w_bf16, gain_row, seg, cos_t, sin_t)


def _proj_out_kernel(o_ref, x_ref, mod_ref, w_ref, lng_ref, lnb_ref, r_ref,
                     x1_ref, hf_ref, aff_ref, *, n_exp):
    y = jnp.dot(o_ref[0], w_ref[...], preferred_element_type=F32)
    x1, hf, aff = _post_mixer(x_ref[0], y, mod_ref[0], lng_ref[...], lnb_ref[...], r_ref[...], n_exp)
    x1_ref[0] = x1
    hf_ref[0] = hf
    aff_ref[0] = aff


def _proj_out(o, x, mod, w_bf16, lng, lnb, router_p, n_exp):
    b, n, d = x.shape
    tm = min(512, n)
    kern = functools.partial(_proj_out_kernel, n_exp=n_exp)
    return pl.pallas_call(
        kern,
        grid=(b, n // tm),
        in_specs=[
            pl.BlockSpec((1, tm, o.shape[2]), lambda i, t: (i, t, 0)),
            pl.BlockSpec((1, tm, d), lambda i, t: (i, t, 0)),
            pl.BlockSpec((1, 8, d), lambda i, t: (i, 0, 0)),
            pl.BlockSpec(w_bf16.shape, lambda i, t: (0, 0)),
            pl.BlockSpec((1, d), lambda i, t: (0, 0)),
            pl.BlockSpec((1, d), lambda i, t: (0, 0)),
            pl.BlockSpec((d, 2 * LANES), lambda i, t: (0, 0)),
        ],
        out_specs=_post_out_specs(tm, d, n_exp),
        out_shape=_post_outs(b, n, d, n_exp),
        compiler_params=_cparams(("parallel", "parallel")),
    )(o, x, mod, w_bf16, lng.reshape(1, d), lnb.reshape(1, d), router_p)


def _lane_tiles(blocks):
    return [s[:, i * LANES:(i + 1) * LANES] for s in blocks for i in range(s.shape[1] // LANES)]


def _softmax_pv(scores, values):
    m = jnp.max(functools.reduce(jnp.maximum, _lane_tiles(scores)), axis=1, keepdims=True)
    ps = [jnp.exp2(s - m) for s in scores]
    l = jnp.sum(functools.reduce(lambda a, c: a + c, _lane_tiles(ps)), axis=1, keepdims=True)
    o = None
    for p, v in zip(ps, values):
        c = jnp.dot(p.astype(BF16), v, preferred_element_type=F32)
        o = c if o is None else o + c
    return o / l


def _na_kernel(q_ref, ka_ref, kb_ref, kc_ref, va_ref, vb_ref, vc_ref, kx_ref, vx_ref, bias_ref,
               o_ref, *, rows):
    i = pl.program_id(1)
    nb = pl.num_programs(1)
    tq = q_ref.shape[1]
    rq0 = i * NA_QROWS
    ks = jnp.clip(i - 1, 0, nb - NA_KBLOCKS) * NA_QROWS
    qi = lax.broadcasted_iota(I32, (tq, tq), 0)
    ki = lax.broadcasted_iota(I32, (tq, tq), 1)
    q_row = rq0 + qi // GRID_W
    q_col = qi % GRID_W
    k_col = ki % GRID_W
    r0 = jnp.clip(q_row - NA_KH // 2, 0, rows - NA_KH)
    c0 = jnp.clip(q_col - NA_KW // 2, 0, GRID_W - NA_KW)
    col_ok = (k_col >= c0) & (k_col < c0 + NA_KW)
    mask_add = []
    for blk in range(NA_KBLOCKS):
        k_row = ks + blk * NA_QROWS + ki // GRID_W
        ok = col_ok & (k_row >= r0) & (k_row < r0 + NA_KH)
        mask_add.append(jnp.where(ok, 0.0, NEG_INF))
    lane = lax.broadcasted_iota(I32, (1, LANES), 1)
    k_refs = (ka_ref, kb_ref, kc_ref)
    v_refs = (va_ref, vb_ref, vc_ref)
    n_pairs = q_ref.shape[2] // LANES
    for p in range(n_pairs):
        cols = slice(p * LANES, (p + 1) * LANES)
        q2 = q_ref[0, :, cols]
        kt = [r[0, :, cols] for r in k_refs] + [kx_ref[0, :, cols]]
        vt = [r[0, :, cols] for r in v_refs] + [vx_ref[0, :, cols]]
        qm = jnp.concatenate(
            [jnp.where((lane // HEAD_DIM) == hh, q2, jnp.zeros_like(q2)) for hh in range(2)], axis=0)
        scores = []
        for blk in range(NA_KBLOCKS):
            s = lax.dot_general(qm, kt[blk], (((1,), (1,)), ((), ())), preferred_element_type=F32)
            bands = []
            for hh in range(2):
                for qr in range(NA_QROWS):
                    halves = []
                    for kp in range(NA_QROWS // 2):
                        dr = ks + blk * NA_QROWS + 2 * kp - (rq0 + qr)
                        idx = jnp.clip(dr + NA_KH, 0, 2 * NA_KH - 1)
                        halves.append(bias_ref[2 * p + hh, idx] + mask_add[blk][
                            qr * GRID_W:(qr + 1) * GRID_W, kp * LANES:(kp + 1) * LANES])
                    bands.append(jnp.concatenate(halves, axis=1))
            scores.append(s + jnp.concatenate(bands, axis=0))
        scores.append(lax.dot_general(qm, kt[NA_KBLOCKS], (((1,), (1,)), ((), ())),
                                      preferred_element_type=F32))
        out = _softmax_pv(scores, vt)
        o_ref[0, :, cols] = jnp.where((lane // HEAD_DIM) == 0, out[:tq], out[tq:]).astype(o_ref.dtype)


def _na_bias_table(rpb):
    col = np.arange(GRID_W)
    dc = np.clip(col[None, :] - col[:, None] + (NA_KW - 1), 0, 2 * NA_KW - 2)
    t = rpb[:, :, dc]
    t_first = jnp.concatenate([t[:, :1], t], axis=1)
    t_next = jnp.concatenate([t, t[:, -1:]], axis=1)
    return jnp.concatenate([t_first, t_next], axis=-1).astype(F32) * LOG2_E


def _na_attention(qkv, qkv_c, bias_tab):
    b, n, d3 = qkv.shape
    d = d3 // 3
    l = qkv_c.shape[1]
    rows = n // GRID_W
    tq = NA_QROWS * GRID_W
    nb = n // tq
    heads = d // HEAD_DIM

    def kmap(off, col):
        return lambda i, t: (i, jnp.clip(t - 1, 0, nb - NA_KBLOCKS) + off, col)

    kern = functools.partial(_na_kernel, rows=rows)
    return pl.pallas_call(
        kern,
        grid=(b, nb),
        in_specs=[
            pl.BlockSpec((1, tq, d), lambda i, t: (i, t, 0)),
            pl.BlockSpec((1, tq, d), kmap(0, 1)),
            pl.BlockSpec((1, tq, d), kmap(1, 1)),
            pl.BlockSpec((1, tq, d), kmap(2, 1)),
            pl.BlockSpec((1, tq, d), kmap(0, 2)),
            pl.BlockSpec((1, tq, d), kmap(1, 2)),
            pl.BlockSpec((1, tq, d), kmap(2, 2)),
            pl.BlockSpec((1, l, d), lambda i, t: (i, 0, 1)),
            pl.BlockSpec((1, l, d), lambda i, t: (i, 0, 2)),
            pl.BlockSpec((heads, 2 * NA_KH, GRID_W, 2 * GRID_W), lambda i, t: (0, 0, 0, 0)),
        ],
        out_specs=pl.BlockSpec((1, tq, d), lambda i, t: (i, t, 0)),
        out_shape=jax.ShapeDtypeStruct((b, n, d), BF16),
        compiler_params=_cparams(("parallel", "parallel")),
    )(qkv, qkv, qkv, qkv, qkv, qkv, qkv, qkv_c, qkv_c, bias_tab)


def _ctx_attn_kernel(q_ref, k_ref, v_ref, o_ref):
    lane = lax.broadcasted_iota(I32, (1, LANES), 1)
    for p in range(q_ref.shape[2] // LANES):
        cols = slice(p * LANES, (p + 1) * LANES)
        q2 = q_ref[0, :, cols]
        k2 = k_ref[0, :, cols]
        v2 = v_ref[0, :, cols]
        outs = []
        for hh in range(2):
            qm = jnp.where((lane // HEAD_DIM) == hh, q2, jnp.zeros_like(q2))
            s = lax.dot_general(qm, k2, (((1,), (1,)), ((), ())), preferred_element_type=F32)
            outs.append(_softmax_pv([s], [v2]))
        o_ref[0, :, cols] = jnp.where((lane // HEAD_DIM) == 0, outs[0], outs[1]).astype(o_ref.dtype)


def _ctx_attention(qkv_c):
    b, l, d3 = qkv_c.shape
    d = d3 // 3
    return pl.pallas_call(
        _ctx_attn_kernel,
        grid=(b,),
        in_specs=[pl.BlockSpec((1, l, d), lambda i: (i, 0, 0)),
                  pl.BlockSpec((1, l, d), lambda i: (i, 0, 1)),
                  pl.BlockSpec((1, l, d), lambda i: (i, 0, 2))],
        out_specs=pl.BlockSpec((1, l, d), lambda i: (i, 0, 0)),
        out_shape=jax.ShapeDtypeStruct((b, l, d), BF16),
        compiler_params=_cparams(("parallel",)),
    )(qkv_c, qkv_c, qkv_c)


def _gqa_kernel(q_ref, k_ref, v_ref, kc_ref, vc_ref, bound_ref, o_ref, qm_sc, m_sc, l_sc, acc_sc, *,
                tk):
    nk = k_ref.shape[1] // tk
    tq = q_ref.shape[1]
    n_sl = q_ref.shape[2] // LANES
    lane = lax.broadcasted_iota(I32, (1, LANES), 1)
    for sl in range(n_sl):
        q2 = q_ref[0, :, sl * LANES:(sl + 1) * LANES]
        for hh in range(2):
            h = 2 * sl + hh
            qm_sc[h * tq:(h + 1) * tq, :] = jnp.where(((lane // (HEAD_DIM // 2)) % 2) == hh, q2,
                                                      jnp.zeros_like(q2))
    l_sc[...] = jnp.zeros_like(l_sc)
    acc_sc[...] = jnp.zeros_like(acc_sc)

    def scores(kt):
        return lax.dot_general(qm_sc[...], kt, (((1,), (1,)), ((), ())),
                               preferred_element_type=F32)

    def lane_partial_sum(p):
        return functools.reduce(lambda u, w: u + w, _lane_tiles([p]))

    def over_keys(update):
        def body(j, c):
            k0 = pl.multiple_of(j * tk, tk)
            update(k_ref[0, pl.ds(k0, tk), :], v_ref[0, pl.ds(k0, tk), :])
            return c

        lax.fori_loop(0, nk, body, 0, unroll=GQA_UNROLL if nk % GQA_UNROLL == 0 else 1)
        update(kc_ref[0], vc_ref[0])

    bound = bound_ref[0:1, 0:1]
    reference = bound - EXP2_HEADROOM
    safe = jnp.max(bound) <= EXP2_HEADROOM

    @pl.when(safe)
    def _():
        def update(kt, vt):
            p = jnp.exp2(scores(kt) - reference)
            l_sc[...] += lane_partial_sum(p)
            acc_sc[...] += jnp.dot(p.astype(BF16), vt, preferred_element_type=F32)

        over_keys(update)

    @pl.when(jnp.logical_not(safe))
    def _():
        m_sc[...] = jnp.full_like(m_sc, NEG_INF)

        def update(kt, vt):
            s = scores(kt)
            m_old = m_sc[...]
            m_new = jnp.maximum(m_old, jnp.max(s, axis=1, keepdims=True))
            a = jnp.exp2(m_old - m_new)
            p = jnp.exp2(s - m_new)
            l_sc[...] = a * l_sc[...] + lane_partial_sum(p)
            acc_sc[...] = a * acc_sc[...] + jnp.dot(p.astype(BF16), vt, preferred_element_type=F32)
            m_sc[...] = m_new

        over_keys(update)

    out = acc_sc[...] / jnp.sum(l_sc[...], axis=1, keepdims=True)
    for sl in range(n_sl):
        o0 = out[2 * sl * tq:(2 * sl + 1) * tq]
        o1 = out[(2 * sl + 1) * tq:(2 * sl + 2) * tq]
        o_ref[0, :, sl * LANES:(sl + 1) * LANES] = jnp.where(
            (lane // HEAD_DIM) == 0, o0, o1).astype(o_ref.dtype)


def _gqa_attention(qkv, qkv_c, dq, dkv, score_bound):
    b, n, _ = qkv.shape
    l = qkv_c.shape[1]
    n_slab = dkv // LANES
    q_per = dq // n_slab
    n_heads = 2 * q_per // LANES
    tq = min(256, n)
    tk = min(1024, n)
    k_blk = dq // LANES
    v_blk = (dq + dkv) // LANES
    kern = functools.partial(_gqa_kernel, tk=tk)
    return pl.pallas_call(
        kern,
        grid=(b, n_slab, n // tq),
        in_specs=[
            pl.BlockSpec((1, tq, q_per), lambda i, p, t: (i, t, p)),
            pl.BlockSpec((1, n, LANES), lambda i, p, t: (i, 0, k_blk + p)),
            pl.BlockSpec((1, n, LANES), lambda i, p, t: (i, 0, v_blk + p)),
            pl.BlockSpec((1, l, LANES), lambda i, p, t: (i, 0, k_blk + p)),
            pl.BlockSpec((1, l, LANES), lambda i, p, t: (i, 0, v_blk + p)),
            pl.BlockSpec((1, LANES), lambda i, p, t: (0, 0)),
        ],
        out_specs=pl.BlockSpec((1, tq, q_per), lambda i, p, t: (i, t, p)),
        out_shape=jax.ShapeDtypeStruct((b, n, dq), BF16),
        scratch_shapes=[pltpu.VMEM((n_heads * tq, LANES), BF16),
                        pltpu.VMEM((n_heads * tq, 1), F32),
                        pltpu.VMEM((n_heads * tq, LANES), F32),
                        pltpu.VMEM((n_heads * tq, LANES), F32)],
        compiler_params=_cparams(("parallel", "parallel", "parallel")),
    )(qkv, qkv, qkv, qkv_c, qkv_c, score_bound)


def _prefix_count(mask_f32, tri, out_cb):
    e, n = mask_f32.shape
    c = tri.shape[0]
    carry = jnp.zeros((e, 1), F32)
    for j in range(n // c):
        chunk = mask_f32[:, j * c:(j + 1) * c]
        inc = jnp.dot(chunk.astype(BF16), tri, preferred_element_type=F32) + carry
        out_cb(j, c, chunk, inc)
        carry = inc[:, c - 1:c]


def _topk_kernel(aff_ref, rank_ref, cum_ref, eqx_ref, *, cap):
    a = aff_ref[0]
    e, n = a.shape
    bits = lax.bitcast_convert_type(a, I32)

    def body(i, thr):
        cand = thr | (jnp.int32(1) << (30 - i))
        cnt = jnp.sum(jnp.where(bits >= cand, 1.0, 0.0), axis=1, keepdims=True)
        return jnp.where(cnt >= cap, cand, thr)

    thr = lax.fori_loop(0, 31, body, jnp.zeros((e, 1), I32))
    gt = bits > thr
    eq = bits == thr
    need = cap - jnp.sum(jnp.where(gt, 1.0, 0.0), axis=1, keepdims=True)
    c = min(256, n)
    tri = jnp.where(lax.broadcasted_iota(I32, (c, c), 0) <= lax.broadcasted_iota(I32, (c, c), 1),
                    1.0, 0.0).astype(BF16)

    def eq_cb(j, c, chunk, inc):
        eqx_ref[:, j * c:(j + 1) * c] = inc - chunk

    _prefix_count(jnp.where(eq, 1.0, 0.0), tri, eq_cb)
    sel = gt | (eq & (eqx_ref[...] < need))

    def sel_cb(j, c, chunk, inc):
        excl = (inc - chunk).astype(I32)
        cum_ref[0, :, j * c:(j + 1) * c] = excl
        rank_ref[0, :, j * c:(j + 1) * c] = jnp.where(chunk > 0.5, excl, -1)

    _prefix_count(jnp.where(sel, 1.0, 0.0), tri, sel_cb)


def _topk(aff, cap):
    b, e, n = aff.shape
    kern = functools.partial(_topk_kernel, cap=cap)
    return pl.pallas_call(
        kern,
        grid=(b,),
        in_specs=[pl.BlockSpec((1, e, n), lambda i: (i, 0, 0))],
        out_specs=(pl.BlockSpec((1, e, n), lambda i: (i, 0, 0)),
                   pl.BlockSpec((1, e, n), lambda i: (i, 0, 0))),
        out_shape=(jax.ShapeDtypeStruct((b, e, n), I32), jax.ShapeDtypeStruct((b, e, n), I32)),
        scratch_shapes=[pltpu.VMEM((e, n), F32)],
        compiler_params=_cparams(("parallel",)),
    )(aff)


def _slot_onehot(rank, starts, lows, k_new, rows_per, n_exp, weights=None):
    t = rank.shape[1]
    j_iota = lax.broadcasted_iota(I32, (rows_per, t), 0)
    rows = []
    for e in range(n_exp):
        r = rank[e:e + 1, :]
        local = jnp.where((r >= lows[e]) & (r < lows[e] + k_new), r - starts[e], -1)
        value = 1.0 if weights is None else weights[e:e + 1, :]
        rows.append(jnp.where(local == j_iota, value, 0.0))
    return jnp.concatenate(rows, axis=0)


def _align_down(v, a):
    return pl.multiple_of((v // a) * a, a)


def _gather_kernel(base_ref, rank_ref, hf_ref, xs_hbm, stage, carry, sem, pend, *, n_exp, n_tiles,
                   cap):
    b = pl.program_id(0)
    t = pl.program_id(1)
    off = (b * (n_tiles + 1) + t) * n_exp
    base = [base_ref[off + e] for e in range(n_exp)]
    cnt = [base_ref[off + n_exp + e] - base[e] for e in range(n_exp)]
    kmax = functools.reduce(jnp.maximum, cnt)
    rank = rank_ref[0]
    hf = hf_ref[0]
    slot = (b * n_tiles + t) % 2

    def window_copy(e, start, sl):
        return pltpu.make_async_copy(stage.at[sl, e * GATHER_ROWS:(e + 1) * GATHER_ROWS, :],
                                     xs_hbm.at[e, b, pl.ds(start, GATHER_ROWS), :], sem.at[e])

    def drain():
        @pl.when(pend[0] == 1)
        def _():
            for e in range(n_exp):
                window_copy(e, 0, slot).wait()
            pend[0] = 0

    @pl.when((b == 0) & (t == 0))
    def _():
        pend[0] = 0

    @pl.when(t == 0)
    def _():
        drain()
        carry[...] = jnp.zeros_like(carry)
        stage[1 - slot] = jnp.zeros(stage.shape[1:], stage.dtype)
        for e in range(n_exp):
            window_copy(e, cap, 1 - slot).start()
        pend[0] = 1

    def write_round(r, guarded):
        lows = [base[e] + r * ROUTE_K for e in range(n_exp)]
        starts = [_align_down(lows[e], ROW_ALIGN) for e in range(n_exp)]
        onehot = _slot_onehot(rank, starts, lows, ROUTE_K, GATHER_ROWS, n_exp).astype(BF16)
        if guarded:
            drain()
        step = GATHER_CHUNK * GATHER_ROWS
        for c0 in range(0, n_exp * GATHER_ROWS, step):
            stage[slot, c0:c0 + step, :] = jnp.dot(onehot[c0:c0 + step], hf,
                                                   preferred_element_type=F32).astype(BF16)
        for e in range(n_exp):
            r0 = e * GATHER_ROWS
            stage[slot, r0:r0 + ROW_ALIGN, :] += carry[e * ROW_ALIGN:(e + 1) * ROW_ALIGN, :]
        if not guarded:
            drain()

        def move(e):
            window_copy(e, starts[e], slot).start()
            filled = lows[e] + jnp.clip(cnt[e] - r * ROUTE_K, 0, ROUTE_K)
            shift = _align_down(filled, ROW_ALIGN) - starts[e]
            src = pl.multiple_of(e * GATHER_ROWS + shift, ROW_ALIGN)
            carry[e * ROW_ALIGN:(e + 1) * ROW_ALIGN, :] = stage[slot, pl.ds(src, ROW_ALIGN), :]

        if not guarded:
            for e in range(n_exp):
                move(e)
            pend[0] = 1
            return
        for e in range(n_exp):
            @pl.when(cnt[e] > r * ROUTE_K)
            def _(e=e):
                move(e)
        for e in range(n_exp):
            @pl.when(cnt[e] > r * ROUTE_K)
            def _(e=e):
                window_copy(e, 0, slot).wait()

    write_round(0, False)

    def extra_round(r, c):
        write_round(r, True)
        return c

    lax.fori_loop(1, (kmax + ROUTE_K - 1) // ROUTE_K, extra_round, 0)

    @pl.when((b == pl.num_programs(0) - 1) & (t == n_tiles - 1))
    def _():
        drain()


def _gather(base_flat, rank, hf, cap):
    b, n, d = hf.shape
    n_exp = rank.shape[1]
    n_tiles = n // ROUTE_T
    kern = functools.partial(_gather_kernel, n_exp=n_exp, n_tiles=n_tiles, cap=cap)
    return pl.pallas_call(
        kern,
        grid_spec=pltpu.PrefetchScalarGridSpec(
            num_scalar_prefetch=1,
            grid=(b, n_tiles),
            in_specs=[
                pl.BlockSpec((1, n_exp, ROUTE_T), lambda i, t, s: (i, 0, t)),
                pl.BlockSpec((1, ROUTE_T, d), lambda i, t, s: (i, t, 0)),
            ],
            out_specs=pl.BlockSpec(memory_space=pl.ANY),
            scratch_shapes=[pltpu.VMEM((2, n_exp * GATHER_ROWS, d), BF16),
                            pltpu.VMEM((n_exp * ROW_ALIGN, d), BF16),
                            pltpu.SemaphoreType.DMA((n_exp,)),
                            pltpu.SMEM((1,), I32)],
        ),
        out_shape=jax.ShapeDtypeStruct((n_exp, b, cap + GATHER_ROWS, d), BF16),
        compiler_params=_cparams(("arbitrary", "arbitrary")),
    )(base_flat, rank, hf)


def _split_bf16(v):
    hi = v.astype(BF16)
    return hi, (v - hi.astype(F32)).astype(BF16)


def _ffn_kernel(*refs, n_f, n_b, with_ctx):
    if with_ctx:
        x_ref, xc_ref, w1_ref, w3_ref, w2_ref, y_ref, yc_ref, acc = refs
    else:
        x_ref, w1_ref, w3_ref, w2_ref, y_ref, acc = refs
    i = pl.program_id(1)
    f = pl.program_id(2)
    tf = w1_ref.shape[3]
    sub = min(FFN_SUB, tf)

    def run(x, emit):
        rows = x.shape[0]
        y = None
        for c in range(tf // sub):
            cs = slice(c * sub, (c + 1) * sub)
            h1 = jnp.dot(x, w1_ref[0, 0, :, cs].astype(BF16), preferred_element_type=F32)
            h3 = jnp.dot(x, w3_ref[0, 0, :, cs].astype(BF16), preferred_element_type=F32)
            hid = (_silu(h1) * h3).astype(BF16)
            part = jnp.dot(hid, w2_ref[0, 0, cs, :].astype(BF16), preferred_element_type=F32)
            y = part if y is None else y + part
        if n_f == 1:
            emit(y.astype(BF16))
            return

        @pl.when(f == 0)
        def _():
            acc[0:rows, :] = y

        @pl.when((f > 0) & (f < n_f - 1))
        def _():
            acc[0:rows, :] += y

        @pl.when(f == n_f - 1)
        def _():
            emit((acc[0:rows, :] + y).astype(BF16))

    def emit_main(out):
        y_ref[0, 0] = out

    def run_main():
        run(x_ref[0, 0], emit_main)

    if not with_ctx:
        run_main()
        return

    bc, cap_c, d = xc_ref.shape[1:]
    cap = x_ref.shape[2]

    def emit_both(out):
        y_ref[0, 0] = out[:cap]
        yc_ref[0] = out[cap:].reshape(bc, cap_c, d)

    pl.when(i < n_b - 1)(run_main)

    @pl.when(i == n_b - 1)
    def _():
        run(jnp.concatenate([x_ref[0, 0], xc_ref[0].reshape(bc * cap_c, d)], axis=0), emit_both)


def _ffn(xs, xs_c, w1, w3, w2, layer, cap, cap_c):
    n_exp, b, _, d = xs.shape
    ff = w1.shape[3]
    tf = min(FFN_BLOCK, ff)
    n_f = ff // tf
    with_ctx = xs_c is not None
    kern = functools.partial(_ffn_kernel, n_f=n_f, n_b=b, with_ctx=with_ctx)

    def main_map(e, i, f):
        return (e, i, 0, 0)

    x_specs = [pl.BlockSpec((1, 1, cap, d), main_map)]
    out_specs = [pl.BlockSpec((1, 1, cap, d), main_map)]
    out_shape = [jax.ShapeDtypeStruct((n_exp, b, cap, d), BF16)]
    operands = [xs]
    acc_rows = cap
    if with_ctx:
        c_spec = pl.BlockSpec((1, b, cap_c, d), lambda e, i, f: (e, 0, 0, 0))
        x_specs.append(c_spec)
        out_specs.append(c_spec)
        out_shape.append(jax.ShapeDtypeStruct((n_exp, b, cap_c, d), BF16))
        operands.append(xs_c)
        acc_rows += b * cap_c
    return pl.pallas_call(
        kern,
        grid=(n_exp, b, n_f),
        in_specs=x_specs + [
            pl.BlockSpec((1, 1, d, tf), lambda e, i, f: (layer, e, 0, f)),
            pl.BlockSpec((1, 1, d, tf), lambda e, i, f: (layer, e, 0, f)),
            pl.BlockSpec((1, 1, tf, d), lambda e, i, f: (layer, e, f, 0)),
        ],
        out_specs=tuple(out_specs),
        out_shape=tuple(out_shape),
        scratch_shapes=[pltpu.VMEM((acc_rows, d), F32)],
        compiler_params=_cparams(("parallel", "arbitrary", "arbitrary")),
    )(*operands, w1, w3, w2)


def _combine_kernel(base_ref, rank_ref, aff_ref, x1_ref, mod_ref, lng_ref, lnb_ref, y_hbm,
                    out_ref, ybuf, sem, *, n_exp, n_tiles, cap, win):
    k_new = win - ROW_ALIGN
    b = pl.program_id(0)
    t = pl.program_id(1)
    g = b * n_tiles + t
    total = pl.num_programs(0) * n_tiles
    slot = g % 2
    tt = rank_ref.shape[2]

    def tile_info(step):
        off = ((step // n_tiles) * (n_tiles + 1) + step % n_tiles) * n_exp
        base = [base_ref[off + e] for e in range(n_exp)]
        cnt = [base_ref[off + n_exp + e] - base[e] for e in range(n_exp)]
        return base, cnt

    def window(base, r, e):
        low = base[e] + r * k_new
        start = pl.multiple_of(jnp.minimum(_align_down(low, ROW_ALIGN), cap - win), ROW_ALIGN)
        return low, start

    def copy(step, start, e, sl):
        return pltpu.make_async_copy(
            y_hbm.at[e, step // n_tiles, pl.ds(start, win), :],
            ybuf.at[sl, pl.ds(e * win, win), :], sem.at[sl, e])

    def issue(step, r, sl):
        base, cnt = tile_info(step)
        for e in range(n_exp):
            def go(e=e):
                copy(step, window(base, r, e)[1], e, sl).start()
            if isinstance(r, int) and r == 0:
                go()
            else:
                pl.when(cnt[e] > r * k_new)(go)

    def wait(step, r, sl):
        _, cnt = tile_info(step)
        for e in range(n_exp):
            def go(e=e):
                copy(step, 0, e, sl).wait()
            if isinstance(r, int) and r == 0:
                go()
            else:
                pl.when(cnt[e] > r * k_new)(go)

    @pl.when(g == 0)
    def _():
        ybuf[...] = jnp.zeros_like(ybuf)
        issue(g, 0, slot)

    @pl.when(g + 1 < total)
    def _():
        issue(g + 1, 0, 1 - slot)

    base, cnt = tile_info(g)
    kmax = functools.reduce(jnp.maximum, cnt)
    rank = rank_ref[0]
    aff = aff_ref[0]

    def scatter(r, f):
        lows, starts = zip(*[window(base, r, e) for e in range(n_exp)])
        gates = _slot_onehot(rank, starts, lows, k_new, win, n_exp, weights=aff)
        wait(g, r, slot)
        return f + jnp.dot(gates.T.astype(BF16), ybuf[slot], preferred_element_type=F32)

    def extra_round(r, f):
        issue(g, r, slot)
        return scatter(r, f)

    d = x1_ref.shape[2]
    f = scatter(0, jnp.zeros((tt, d), F32))
    f = lax.fori_loop(1, (kmax + k_new - 1) // k_new, extra_round, f)
    mod = mod_ref[0]
    out_ref[0] = _layer_norm(ALPHA * x1_ref[0] + mod[5:6] * f, lng_ref[...], lnb_ref[...])


def _combine(cum, rank, aff, x1, mod, lng, lnb, y, cap):
    b, n, d = x1.shape
    n_exp = rank.shape[1]
    n_tiles = n // COMBINE_T
    win = min(COMBINE_WIN, cap)
    assert cap % ROW_ALIGN == 0 and win > ROW_ALIGN
    kern = functools.partial(_combine_kernel, n_exp=n_exp, n_tiles=n_tiles, cap=cap, win=win)
    return pl.pallas_call(
        kern,
        grid_spec=pltpu.PrefetchScalarGridSpec(
            num_scalar_prefetch=1,
            grid=(b, n_tiles),
            in_specs=[
                pl.BlockSpec((1, n_exp, COMBINE_T), lambda i, t, s: (i, 0, t)),
                pl.BlockSpec((1, n_exp, COMBINE_T), lambda i, t, s: (i, 0, t)),
                pl.BlockSpec((1, COMBINE_T, d), lambda i, t, s: (i, t, 0)),
                pl.BlockSpec((1, 8, d), lambda i, t, s: (i, 0, 0)),
                pl.BlockSpec((1, d), lambda i, t, s: (0, 0)),
                pl.BlockSpec((1, d), lambda i, t, s: (0, 0)),
                pl.BlockSpec(memory_space=pl.ANY),
            ],
            out_specs=pl.BlockSpec((1, COMBINE_T, d), lambda i, t, s: (i, t, 0)),
            scratch_shapes=[pltpu.VMEM((2, n_exp * win, d), BF16),
                            pltpu.SemaphoreType.DMA((2, n_exp))],
        ),
        out_shape=jax.ShapeDtypeStruct((b, n, d), F32),
        compiler_params=_cparams(("arbitrary", "arbitrary")),
    )(_tile_bases(cum, COMBINE_T, cap), rank, aff, x1, mod, lng.reshape(1, d), lnb.reshape(1, d), y)


def _tile_bases(cum, tile, cap):
    b, n_exp, _ = cum.shape
    tile_base = jnp.swapaxes(cum[:, :, ::tile], 1, 2)
    return jnp.concatenate([tile_base, jnp.full((b, 1, n_exp), cap, I32)], axis=1).reshape(-1)


def _route(hf, aff):
    n = hf.shape[1]
    cap = EC_CAPACITY * n // aff.shape[1]
    rank, cum = _topk(aff, cap)
    return cum, rank, _gather(_tile_bases(cum, ROUTE_T, cap), rank, hf, cap), cap


def _moe(post_x, post_c, mod_x, mod_c, lng, lnb, w1, w3, w2, layer):
    x1, hf, aff = post_x
    cum, rank, xs, cap = _route(hf, aff)
    if post_c is None:
        (y,) = _ffn(xs, None, w1, w3, w2, layer, cap, None)
        return _combine(cum, rank, aff, x1, mod_x, lng, lnb, y, cap), None
    c1, hf_c, aff_c = post_c
    cum_c, rank_c, xs_c, cap_c = _route(hf_c, aff_c)
    y, y_c = _ffn(xs, xs_c, w1, w3, w2, layer, cap, cap_c)
    return (_combine(cum, rank, aff, x1, mod_x, lng, lnb, y, cap),
            _combine(cum_c, rank_c, aff_c, c1, mod_c, lng, lnb, y_c, cap_c))


def _gqa_layout(n_q_heads):
    group = n_q_heads // GQA_KV_HEADS
    half = HEAD_DIM // 2
    ev = np.arange(half) * 2
    od = ev + 1

    def slab(col_a, col_b):
        return np.concatenate([col_a + ev, col_b + ev, col_a + od, col_b + od])

    q_cols, k_cols, o_rows, gain_idx = [], [], [], []
    dq = n_q_heads * HEAD_DIM
    for p in range(GQA_KV_HEADS // 2):
        for i in range(group):
            a = (2 * p) * group + i
            c = (2 * p + 1) * group + i
            q_cols.append(slab(a * HEAD_DIM, c * HEAD_DIM))
            o_rows.append(np.concatenate([a * HEAD_DIM + np.arange(HEAD_DIM),
                                          c * HEAD_DIM + np.arange(HEAD_DIM)]))
    for p in range(GQA_KV_HEADS // 2):
        k_cols.append(dq + slab(2 * p * HEAD_DIM, (2 * p + 1) * HEAD_DIM))
    lane_dim = np.concatenate([ev, ev, od, od])
    return (np.concatenate(q_cols), np.concatenate(k_cols), np.concatenate(o_rows), lane_dim)


def _rope_tables(n):
    t = jnp.arange(n, dtype=I32)
    row = (t // GRID_W).astype(F32)
    col = (t % GRID_W).astype(F32)
    axis_dims = HEAD_DIM // 2
    inv_freq = jnp.power(ROPE_THETA, -jnp.arange(0, axis_dims, 2, dtype=F32) / axis_dims)
    ang = jnp.concatenate([row[:, None] * inv_freq, col[:, None] * inv_freq], axis=-1)
    cos, sin = jnp.cos(ang), jnp.sin(ang)
    cos_t = jnp.concatenate([cos] * 4, axis=1)
    sin_t = jnp.concatenate([-sin, -sin, sin, sin], axis=1)
    return cos_t, sin_t


def kernel(x, c, ctx, c_ctx, mod_w, mod_b, ln_g, ln_b, pool_w, pool_scale, na_wqkv, na_wo, na_rpb,
           gqa_wqkv, gqa_q_norm, gqa_k_norm, gqa_wo, moe_router, moe_w1, moe_w3, moe_w2):
    bsz, n, d = x.shape
    l = ctx.shape[1]
    n_exp = moe_router.shape[2]
    depth = mod_w.shape[0]
    cc = jnp.zeros((8, d), F32).at[:bsz].set(c).at[bsz].set(c_ctx)
    mod_all = _modulation(cc, mod_w, mod_b)

    for i in range(depth):
        m = i % N_MIXERS
        j = i // N_MIXERS
        update_ctx = any(k % N_MIXERS != 0 for k in range(i + 1, depth))
        ctx_keys = m != 0
        mod6 = mod_all[i].reshape(8, N_MOD, d)
        mod_x = jnp.zeros((bsz, 8, d), F32).at[:, :N_MOD].set(mod6[:bsz])
        mod_c = jnp.zeros((bsz, 8, d), F32).at[:, :N_MOD].set(
            jnp.broadcast_to(mod6[bsz][None], (bsz, N_MOD, d)))
        router_p = _pad_router(moe_router[i])
        lng0, lnb0, lng1, lnb1 = ln_g[i, 0], ln_b[i, 0], ln_g[i, 1], ln_b[i, 1]
        post_c = None
        if m == 0:
            post_x = _pool_layer(x, mod_x, pool_w[j], pool_scale[j], lng0, lnb0, router_p, n_exp)
            if update_ctx:
                post_c = _pool_layer(ctx, mod_c, pool_w[j], pool_scale[j], lng0, lnb0, router_p,
                                     n_exp)
        elif m == 1:
            wqkv = na_wqkv[j].astype(BF16)
            wo = na_wo[j].astype(BF16)
            qkv = _proj_in(x, mod_x, wqkv, d)
            qkv_c = _proj_in(ctx, mod_c, wqkv, d)
            o = _na_attention(qkv, qkv_c, _na_bias_table(na_rpb[j]))
            post_x = _proj_out(o, x, mod_x, wo, lng0, lnb0, router_p, n_exp)
            if update_ctx:
                oc = _ctx_attention(qkv_c)
                post_c = _proj_out(oc, ctx, mod_c, wo, lng0, lnb0, router_p, n_exp)
        else:
            n_q_heads = gqa_wo.shape[1] // HEAD_DIM
            dq = n_q_heads * HEAD_DIM
            dkv = GQA_KV_HEADS * HEAD_DIM
            q_cols, k_cols, o_rows, lane_dim = _gqa_layout(n_q_heads)
            v_cols = dq + dkv + np.arange(dkv)
            w_perm = gqa_wqkv[j][:, np.concatenate([q_cols, k_cols, v_cols])].astype(BF16)
            wo = gqa_wo[j][o_rows].astype(BF16)
            gain_row = jnp.concatenate(
                [jnp.tile(gqa_q_norm[j][lane_dim] * (HEAD_DIM ** -0.5 * LOG2_E), dq // LANES),
                 jnp.tile(gqa_k_norm[j][lane_dim], dkv // LANES),
                 jnp.ones((dkv,), F32)]).reshape(1, -1)
            lanes2 = np.arange(2 * LANES)
            lane_head = 2 * (lanes2 // LANES) + (lanes2 // (HEAD_DIM // 2)) % 2
            seg = jnp.asarray((lane_head[:, None] == lane_head[None, :]) / HEAD_DIM, BF16)
            cos_t, sin_t = _rope_tables(n)
            n_norm = (dq + dkv) // LANES
            qkv = _gqa_proj(x, mod_x, w_perm, gain_row, seg, cos_t, sin_t, n_norm)
            qkv_c = _gqa_proj(ctx, mod_c, w_perm, gain_row, seg, jnp.ones((l, LANES), F32),
                              jnp.zeros((l, LANES), F32), n_norm)
            q_gain_max = jnp.max(jnp.abs(gqa_q_norm[j])) * (HEAD_DIM ** -0.5 * LOG2_E)
            score_bound = jnp.full((1, LANES), 1.02 * HEAD_DIM, F32) * (
                q_gain_max * jnp.max(jnp.abs(gqa_k_norm[j])))
            o = _gqa_attention(qkv, qkv_c, dq, dkv, score_bound)
            post_x = _proj_out(o, x, mod_x, wo, lng0, lnb0, router_p, n_exp)
            if update_ctx:
                raise NotImplementedError("context update after a GQA layer is not part of this stack")
        x, ctx_new = _moe(post_x, post_c if update_ctx else None, mod_x, mod_c, lng1, lnb1,
                          moe_w1, moe_w3, moe_w2, i)
        if update_ctx:
            ctx = ctx_new
    return x
```
